```python
import math
import jax, jax.numpy as jnp
from jax import lax
import numpy as np

D_MODEL = 1024
BATCH = 1
SEQ = 16384
DEPTH = 2

N_HEADS_A = 8
HEAD_DIM_A = 64
D_A = N_HEADS_A * HEAD_DIM_A
ATTN_PATTERNS = ((128, 1), (512, 4), (2048, 16))
ATTN_BLOCK = 128
NUM_BUCKETS = 32
MAX_DISTANCE = 2048
N_HEADS_M = 4
HEAD_DIM_M = 128
D_M = N_HEADS_M * HEAD_DIM_M
CONV_K = 4
MLSTM_CHUNK = 128
P_IN = 3 * D_A + 2 * D_M
D_FF = 2816
N_EXPERTS = 8
TOP_K = 2
D_FF_E = 3584
N_DENSE = (DEPTH + 1) // 2
N_MOE = DEPTH // 2
ALPHA = (2.0 * DEPTH) ** 0.25
BETA = (8.0 * DEPTH) ** -0.25
LN_EPS = 1e-5

kernel_name = "hybrid_dilated_attn_mlstm_moe_deepnorm"


def layer_norm(x, g, b):
    xf = x.astype(jnp.float32)
    mu = xf.mean(-1, keepdims=True)
    var = jnp.square(xf - mu).mean(-1, keepdims=True)
    return ((xf - mu) * lax.rsqrt(var + LN_EPS) * g.astype(jnp.float32) + b.astype(jnp.float32)).astype(x.dtype)


def rel_bucket(dist):
    exact = NUM_BUCKETS // 2
    d = jnp.maximum(dist, exact).astype(jnp.float32)
    log_b = exact + (jnp.log(d / exact) / math.log(MAX_DISTANCE / exact) * (NUM_BUCKETS - exact)).astype(jnp.int32)
    return jnp.where(dist < exact, dist, jnp.minimum(log_b, NUM_BUCKETS - 1))


def strided_window_attention(q, k, v, rel_bias, window, dilation):
    B, S, H, hd = q.shape
    L = S // dilation
    W = window // dilation
    blk = ATTN_BLOCK
    nb = -(-L // blk)
    Lp = nb * blk

    def to_classes(t):
        t = t.reshape(B, L, dilation, H, hd).transpose(0, 2, 1, 3, 4).reshape(B * dilation, L, H, hd)
        return jnp.pad(t, ((0, 0), (0, Lp - L), (0, 0), (0, 0)))

    qc, kc, vc = to_classes(q), to_classes(k), to_classes(v)
    qb = qc.reshape(B * dilation, nb, blk, H, hd)

    def band(t):
        t = jnp.pad(t, ((0, 0), (blk, 0), (0, 0), (0, 0))).reshape(B * dilation, nb + 1, blk, H, hd)
        return jnp.concatenate([t[:, :-1], t[:, 1:]], axis=2)

    kb, vb = band(kc), band(vc)
    qi = jnp.arange(blk)[:, None]
    kj = jnp.arange(2 * blk)[None, :]
    rel = qi + blk - kj
    bias = rel_bias[rel_bucket(jnp.maximum(rel, 0) * dilation)]
    bias = bias.transpose(2, 0, 1).astype(jnp.float32)
    key_pos = jnp.arange(nb)[:, None] * blk - blk + kj
    mask = ((rel >= 0) & (rel <= W))[None] & (key_pos >= 0)[:, None, :]

    logits = jnp.einsum('bnqhd,bnkhd->bnhqk', qb.astype(jnp.float32), kb.astype(jnp.float32)) * (hd ** -0.5) + bias
    logits = jnp.where(mask[None, :, None], logits, -jnp.inf)
    m = logits.max(-1, keepdims=True)
    p = jnp.exp(logits - m)
    denom = p.sum(-1, keepdims=True)
    o = jnp.einsum('bnhqk,bnkhd->bnqhd', p / denom, vb.astype(jnp.float32))
    lse = (m + jnp.log(denom))[..., 0].transpose(0, 1, 3, 2)

    def from_classes(t):
        t = t.reshape((B * dilation, Lp) + t.shape[3:])[:, :L]
        t = t.reshape((B, dilation, L) + t.shape[2:])
        t = jnp.moveaxis(t, 1, 2)
        return t.reshape((B, S) + t.shape[3:])

    return from_classes(o), from_classes(lse)


def dilated_attention(q, k, v, rel_bias):
    outs, lses = [], []
    for window, dilation in ATTN_PATTERNS:
        o, lse = strided_window_attention(q, k, v, rel_bias, window, dilation)
        outs.append(o)
        lses.append(lse)
    wts = jax.nn.softmax(jnp.stack(lses), axis=0)
    return jnp.einsum('pbsh,pbshd->bshd', wts, jnp.stack(outs))


def mlstm_chunkwise(q, k, v, i_pre, log_f):
    B, S, H, hd = q.shape
    L = MLSTM_CHUNK
    nc = S // L
    k = k * (hd ** -0.5)
    tri = jnp.tril(jnp.ones((L, L), dtype=bool))

    def chunks(t):
        return jnp.moveaxis(t.reshape((B, nc, L) + t.shape[2:]), 1, 0)

    def step(carry, xs):
        C, n, m = carry
        qc, kc, vc, ic, fc = xs
        b = jnp.cumsum(fc, axis=1).transpose(0, 2, 1)
        ic = ic.transpose(0, 2, 1)
        Dm = jnp.where(tri, b[..., :, None] - b[..., None, :] + ic[..., None, :], -jnp.inf)
        inter = b + m[..., None]
        m_loc = jnp.maximum(inter, Dm.max(-1))
        Dexp = jnp.exp(Dm - m_loc[..., None])
        inter_w = jnp.exp(inter - m_loc)
        s = jnp.einsum('blhd,bshd->bhls', qc, kc) * Dexp
        num = inter_w[..., None] * jnp.einsum('blhd,bhde->bhle', qc, C) + jnp.einsum('bhls,bshe->bhle', s, vc)
        nq = inter_w * jnp.einsum('blhd,bhd->bhl', qc, n) + s.sum(-1)
        h = num / jnp.maximum(jnp.abs(nq), jnp.exp(-m_loc))[..., None]
        bL = b[..., -1]
        g = bL[..., None] - b + ic
        m_new = jnp.maximum(bL + m, g.max(-1))
        decay = jnp.exp(bL + m - m_new)
        w = jnp.exp(g - m_new[..., None])
        C = decay[..., None, None] * C + jnp.einsum('bhs,bshd,bshe->bhde', w, kc, vc)
        n = decay[..., None] * n + jnp.einsum('bhs,bshd->bhd', w, kc)
        return (C, n, m_new), h.transpose(0, 2, 1, 3)

    init = (jnp.zeros((B, H, hd, hd), jnp.float32), jnp.zeros((B, H, hd), jnp.float32),
            jnp.zeros((B, H), jnp.float32))
    _, hs = lax.scan(step, init, (chunks(q), chunks(k), chunks(v), chunks(i_pre), chunks(log_f)))
    return jnp.moveaxis(hs, 0, 1).reshape(B, S, H, hd)


def causal_conv(x, w, b):
    out = lax.conv_general_dilated(x, w[:, None, :].astype(x.dtype), window_strides=(1,),
                                   padding=[(CONV_K - 1, 0)], dimension_numbers=('NWC', 'WIO', 'NWC'),
                                   feature_group_count=x.shape[-1])
    return out + b


def token_mixer(x, rel_bias, w_in, w_gate, b_gate, conv_w, conv_b, w_qk_m, w_v_m, w_if, b_if,
                m_norm_g, w_br_a, w_br_m, w_o):
    B, S, _ = x.shape
    proj = x @ w_in
    q_a, k_a, v_a, x_m, z_m = jnp.split(proj, [D_A, 2 * D_A, 3 * D_A, 3 * D_A + D_M], axis=-1)

    ha = lambda t: t.reshape(B, S, N_HEADS_A, HEAD_DIM_A)
    y_a = dilated_attention(ha(q_a), ha(k_a), ha(v_a), rel_bias).reshape(B, S, D_A).astype(x.dtype)

    hm = lambda t: t.reshape(B, S, N_HEADS_M, HEAD_DIM_M)
    x_c = jax.nn.silu(causal_conv(x_m, conv_w, conv_b))
    q_m = jnp.einsum('bshd,hde->bshe', hm(x_c), w_qk_m[0])
    k_m = jnp.einsum('bshd,hde->bshe', hm(x_c), w_qk_m[1])
    v_m = jnp.einsum('bshd,hde->bshe', hm(x_m), w_v_m)
    qkv = jnp.concatenate([q_m.reshape(B, S, D_M), k_m.reshape(B, S, D_M), v_m.reshape(B, S, D_M)], axis=-1)
    gates = (qkv @ w_if + b_if).astype(jnp.float32)
    i_pre, f_pre = jnp.split(gates, 2, axis=-1)
    f32 = lambda t: t.astype(jnp.float32)
    h = mlstm_chunkwise(f32(q_m), f32(k_m), f32(v_m), i_pre, jax.nn.log_sigmoid(f_pre))
    mu = h.mean(-1, keepdims=True)
    var = jnp.square(h - mu).mean(-1, keepdims=True)
    h = ((h - mu) * lax.rsqrt(var + LN_EPS)).reshape(B, S, D_M) * m_norm_g.astype(jnp.float32)
    y_m = (jax.nn.sigmoid(f32(z_m)) * h).astype(x.dtype)

    g_a, g_m = jnp.split(jax.nn.sigmoid(x @ w_gate + b_gate), 2, axis=-1)
    merged = g_a * (y_a @ w_br_a) + g_m * (y_m @ w_br_m)
    return merged @ w_o


def swiglu(x, w13, w2):
    a, g = jnp.split(x @ w13, 2, axis=-1)
    return (jax.nn.silu(a) * g) @ w2


def moe_ffn(x, router_w, router_b, w13, w2):
    logits = (x @ router_w).astype(jnp.float32) + router_b.astype(jnp.float32)
    top_v, top_i = lax.top_k(logits, TOP_K)
    wts = jax.nn.softmax(top_v, axis=-1)
    gate = jnp.einsum('bsk,bske->bse', wts, jax.nn.one_hot(top_i, N_EXPERTS, dtype=jnp.float32)).astype(x.dtype)
    y = jnp.zeros_like(x)
    for e in range(N_EXPERTS):
        y = y + gate[..., e:e + 1] * swiglu(x, w13[e], w2[e])
    return y


def setup_inputs(seed: int = 0) -> dict:
    key = jax.random.key(seed)
    ks = jax.random.split(key, 24)
    nrm = lambda k, shape, scale: jax.random.normal(k, shape, jnp.float32) * scale
    b_if = jnp.concatenate([nrm(ks[10], (DEPTH, N_HEADS_M), 0.1),
                            3.0 + 3.0 * jax.random.uniform(ks[11], (DEPTH, N_HEADS_M), jnp.float32)], axis=-1)
    return {
        "x": nrm(ks[0], (BATCH, SEQ, D_MODEL), 1.0),
        "rel_bias": nrm(ks[1], (NUM_BUCKETS, N_HEADS_A), 0.2),
        "w_in": nrm(ks[2], (DEPTH, D_MODEL, P_IN), D_MODEL ** -0.5),
        "w_gate": nrm(ks[3], (DEPTH, D_MODEL, 2 * D_MODEL), D_MODEL ** -0.5),
        "b_gate": nrm(ks[4], (DEPTH, 2 * D_MODEL), 0.02),
        "conv_w": nrm(ks[5], (DEPTH, CONV_K, D_M), CONV_K ** -0.5),
        "conv_b": nrm(ks[6], (DEPTH, D_M), 0.02),
        "w_qk_m": nrm(ks[7], (DEPTH, 2, N_HEADS_M, HEAD_DIM_M, HEAD_DIM_M), HEAD_DIM_M ** -0.5),
        "w_v_m": nrm(ks[8], (DEPTH, N_HEADS_M, HEAD_DIM_M, HEAD_DIM_M), HEAD_DIM_M ** -0.5),
        "w_if": nrm(ks[9], (DEPTH, 3 * D_M, 2 * N_HEADS_M), (3 * D_M) ** -0.5),
        "b_if": b_if,
        "m_norm_g": 1.0 + nrm(ks[12], (DEPTH, D_M), 0.02),
        "w_br_a": nrm(ks[13], (DEPTH, D_A, D_MODEL), D_A ** -0.5),
        "w_br_m": nrm(ks[14], (DEPTH, D_M, D_MODEL), D_M ** -0.5),
        "w_o": nrm(ks[15], (DEPTH, D_MODEL, D_MODEL), BETA * D_MODEL ** -0.5),
        "ln_g": 1.0 + nrm(ks[16], (DEPTH, 2, D_MODEL), 0.02),
        "ln_b": nrm(ks[17], (DEPTH, 2, D_MODEL), 0.02),
        "ffn_w13": nrm(ks[18], (N_DENSE, D_MODEL, 2 * D_FF), D_MODEL ** -0.5),
        "ffn_w2": nrm(ks[19], (N_DENSE, D_FF, D_MODEL), BETA * D_FF ** -0.5),
        "router_w": nrm(ks[20], (N_MOE, D_MODEL, N_EXPERTS), D_MODEL ** -0.5),
        "router_b": nrm(ks[21], (N_MOE, N_EXPERTS), 0.01),
        "exp_w13": nrm(ks[22], (N_MOE, N_EXPERTS, D_MODEL, 2 * D_FF_E), D_MODEL ** -0.5),
        "exp_w2": nrm(ks[23], (N_MOE, N_EXPERTS, D_FF_E, D_MODEL), BETA * D_FF_E ** -0.5),
    }


def reference(x, rel_bias, w_in, w_gate, b_gate, conv_w, conv_b, w_qk_m, w_v_m, w_if, b_if, m_norm_g,
              w_br_a, w_br_m, w_o, ln_g, ln_b, ffn_w13, ffn_w2, router_w, router_b, exp_w13, exp_w2):
    for l in range(DEPTH):
        y = token_mixer(x, rel_bias, w_in[l], w_gate[l], b_gate[l], conv_w[l], conv_b[l], w_qk_m[l],
                        w_v_m[l], w_if[l], b_if[l], m_norm_g[l], w_br_a[l], w_br_m[l], w_o[l])
        x = layer_norm(ALPHA * x + y, ln_g[l, 0], ln_b[l, 0])
        j = l // 2
        if l % 2 == 0:
            y = swiglu(x, ffn_w13[j], ffn_w2[j])
        else:
            y = moe_ffn(x, router_w[j], router_b[j], exp_w13[j], exp_w2[j])
        x = layer_norm(ALPHA * x + y, ln_g[l, 1], ln_b[l, 1])
    return x
```

```python
import functools
import math

import jax
import jax.numpy as jnp
from jax import lax
from jax.experimental import pallas as pl
from jax.experimental.pallas import tpu as pltpu

F32 = jnp.float32
BF16 = jnp.bfloat16
I32 = jnp.int32

D_MODEL = 1024
DEPTH = 2
N_HEADS_A = 8
HEAD_DIM_A = 64
D_A = N_HEADS_A * HEAD_DIM_A
ATTN_PATTERNS = ((128, 1), (512, 4), (2048, 16))
ATTN_BLOCK = 128
NUM_BUCKETS = 32
MAX_DISTANCE = 2048
N_HEADS_M = 4
HEAD_DIM_M = 128
D_M = N_HEADS_M * HEAD_DIM_M
CONV_K = 4
MLSTM_CHUNK = 128
N_PROJ = 5
P_IN = 3 * D_A + 2 * D_M
D_FF = 2816
N_EXPERTS = 8
TOP_K = 2
D_FF_E = 3584
ALPHA = (2.0 * DEPTH) ** 0.25
LN_EPS = 1e-5

NEG = -1e30
LANES = 128
HALO_ROWS = 16
MIB = 1024 * 1024

MOE_ROW_TILE = 256
MOE_GROUP_TILE = 1024
MOE_FF_CHUNK = 512
FFN_ROW_TILE = 1024
FFN_FF_CHUNK = 256


def _params(sem, vmem_mib):
    return pltpu.CompilerParams(dimension_semantics=sem, vmem_limit_bytes=vmem_mib * MIB)


def _sigmoid(x):
    return 1.0 / (1.0 + jnp.exp(-x))


def _layer_norm(r, g, b):
    mu = jnp.mean(r, axis=-1, keepdims=True)
    c = r - mu
    var = jnp.mean(c * c, axis=-1, keepdims=True)
    return c * lax.rsqrt(var + LN_EPS) * g + b


def _split3(a):
    hi = a.astype(BF16)
    r1 = a - hi.astype(F32)
    mid = r1.astype(BF16)
    lo = (r1 - mid.astype(F32)).astype(BF16)
    return hi, mid, lo


def _dot(a, b):
    return jnp.dot(a, b, preferred_element_type=F32)


def _dot_nt(a, b):
    return lax.dot_general(a, b, (((1,), (1,)), ((), ())), preferred_element_type=F32)


def _dot_tn(a, b):
    return lax.dot_general(a, b, (((0,), (0,)), ((), ())), preferred_element_type=F32)


def _in_proj_kernel(x_ref, w_ref, o_ref, wb_ref):
    @pl.when(pl.program_id(0) == 0)
    def _cast_weights():
        wb_ref[...] = w_ref[...].astype(BF16)

    xb = x_ref[...].astype(BF16)
    for j in range(N_PROJ):
        y = _dot(xb, wb_ref[:, j * D_A:(j + 1) * D_A])
        if j == 0:
            y = y * (HEAD_DIM_A ** -0.5)
        o_ref[j] = y.astype(o_ref.dtype)


def _in_proj(x, w_in):
    s = x.shape[0]
    tm = 512
    return pl.pallas_call(
        _in_proj_kernel,
        out_shape=jax.ShapeDtypeStruct((N_PROJ, s, D_A), BF16),
        grid=(s // tm,),
        in_specs=[
            pl.BlockSpec((tm, D_MODEL), lambda i: (i, 0)),
            pl.BlockSpec((D_MODEL, P_IN), lambda i: (0, 0), pipeline_mode=pl.Buffered(1)),
        ],
        out_specs=pl.BlockSpec((N_PROJ, tm, D_A), lambda i: (0, i, 0)),
        scratch_shapes=[pltpu.VMEM((D_MODEL, P_IN), BF16)],
        compiler_params=_params(("arbitrary",), 40),
        name="in_proj",
    )(x, w_in)


def _rel_bucket(dist):
    exact = NUM_BUCKETS // 2
    d = jnp.maximum(dist, exact).astype(F32)
    log_b = exact + (jnp.log(d / exact) / math.log(MAX_DISTANCE / exact) * (NUM_BUCKETS - exact)).astype(I32)
    return jnp.where(dist < exact, dist, jnp.minimum(log_b, NUM_BUCKETS - 1))


def _attn_bias(rel_bias, window, dilation):
    blk = ATTN_BLOCK
    qi = jnp.arange(blk)[:, None]
    kj = jnp.arange(2 * blk)[None, :]
    rel = qi + blk - kj
    bias = rel_bias[_rel_bucket(jnp.maximum(rel, 0) * dilation)].transpose(2, 0, 1).astype(F32)
    mask = (rel >= 0) & (rel <= window // dilation)
    return jnp.where(mask[None], bias, NEG)


def _attn_kernel(q_ref, kp_ref, kc_ref, vp_ref, vc_ref, bias_ref, o_ref, lse_ref, kf_ref, vf_ref, *, qb):
    blk = ATTN_BLOCK
    n = pl.program_id(1)
    kf_ref[0:blk, :] = kp_ref[...]
    kf_ref[blk:, :] = kc_ref[...]
    vf_ref[0:blk, :] = vp_ref[...]
    vf_ref[blk:, :] = vc_ref[...]
    first_pen = jnp.where(n == 0, NEG, 0.0).astype(F32)
    kj = lax.broadcasted_iota(I32, (blk, 2 * blk), 1)
    prev_pen = jnp.where(kj < blk, first_pen, 0.0)
    lane = lax.broadcasted_iota(I32, (blk, LANES), 1)
    for s in range(qb):
        q = q_ref[s * blk:(s + 1) * blk, :]
        k = kf_ref[s * blk:(s + 2) * blk, :]
        v = vf_ref[s * blk:(s + 2) * blk, :]
        lse_blk = jnp.zeros((blk, LANES), F32)
        for h in range(N_HEADS_A):
            sl = slice(h * HEAD_DIM_A, (h + 1) * HEAD_DIM_A)
            logits = _dot_nt(q[:, sl], k[:, sl]) + bias_ref[h]
            if s == 0:
                logits = logits + prev_pen
            m = jnp.max(logits, axis=-1, keepdims=True)
            p = jnp.exp(logits - m)
            l = jnp.sum(p, axis=-1, keepdims=True)
            o = _dot(p.astype(BF16), v[:, sl]) / l
            o_ref[s * blk:(s + 1) * blk, sl] = o.astype(o_ref.dtype)
            lse_blk = jnp.where(lane == h, m + jnp.log(l), lse_blk)
        lse_ref[s * blk:(s + 1) * blk, :] = lse_blk


def _attn_pattern(q, k, v, bias, dilation):
    s = q.shape[0]
    blk = ATTN_BLOCK
    length = s // dilation
    qb = min(4, length // blk)
    rows = qb * blk
    view = lambda t: t.reshape(length, dilation * D_A)
    cur = pl.BlockSpec((rows, D_A), lambda r, n: (n, r))
    prev = pl.BlockSpec((blk, D_A), lambda r, n: (jnp.maximum(n * qb - 1, 0), r))
    o, lse = pl.pallas_call(
        functools.partial(_attn_kernel, qb=qb),
        out_shape=(jax.ShapeDtypeStruct((length, dilation * D_A), BF16),
                   jax.ShapeDtypeStruct((length, dilation * LANES), F32)),
        grid=(dilation, length // rows),
        in_specs=[cur, prev, cur, prev, cur,
                  pl.BlockSpec((N_HEADS_A, blk, 2 * blk), lambda r, n: (0, 0, 0))],
        out_specs=(pl.BlockSpec((rows, D_A), lambda r, n: (n, r)),
                   pl.BlockSpec((rows, LANES), lambda r, n: (n, r))),
        scratch_shapes=[pltpu.VMEM((rows + blk, D_A), BF16), pltpu.VMEM((rows + blk, D_A), BF16)],
        compiler_params=_params(("arbitrary", "arbitrary"), 32),
        name=f"attn_d{dilation}",
    )(view(q), view(k), view(k), view(v), view(v), bias)
    return o.reshape(s, D_A), lse.reshape(s, LANES)


def _log_sigmoid(x):
    return jnp.minimum(x, 0.0) - jnp.log(1.0 + jnp.exp(-jnp.abs(x)))


def _mlstm_prep_kernel(xm_ref, halo_ref, cw_ref, cb_ref, wqk_ref, wv_ref, wif_ref, wift_ref, bif_ref, bift_ref,
                       q_ref, k_ref, v_ref, gc_ref, gr_ref):
    tm = xm_ref.shape[0]
    hm, hd, lc = N_HEADS_M, HEAD_DIM_M, MLSTM_CHUNK
    xmb = xm_ref[...]
    halo = jnp.where(pl.program_id(0) == 0, 0.0, halo_ref[...].astype(F32))
    xx = jnp.concatenate([halo, xmb.astype(F32)], axis=0)
    conv = jnp.zeros((tm, D_M), F32) + cb_ref[...]
    for j in range(CONV_K):
        start = HALO_ROWS - (CONV_K - 1) + j
        conv = conv + cw_ref[j:j + 1, :] * xx[start:start + tm, :]
    xcb = (conv * _sigmoid(conv)).astype(BF16)

    qs, ks, vs = [], [], []
    for h in range(hm):
        sl = slice(h * hd, (h + 1) * hd)
        qs.append(_dot(xcb[:, sl], wqk_ref[0, h].astype(BF16)))
        ks.append(_dot(xcb[:, sl], wqk_ref[1, h].astype(BF16)))
        vs.append(_dot(xmb[:, sl], wv_ref[h].astype(BF16)))
    q = jnp.concatenate(qs, axis=1)
    k = jnp.concatenate(ks, axis=1)
    v = jnp.concatenate(vs, axis=1)
    q_ref[...] = q.astype(BF16)
    k_ref[...] = (k * (hd ** -0.5)).astype(BF16)
    v_ref[...] = v.astype(BF16)

    qkv = jnp.concatenate([q, k, v], axis=1).astype(BF16)
    gates_c = _dot(qkv, wif_ref[...].astype(BF16)) + bif_ref[...]
    gates_r = _dot_nt(wift_ref[...].astype(BF16), qkv) + bift_ref[...]
    lane = lax.broadcasted_iota(I32, gates_c.shape, 1)
    row = lax.broadcasted_iota(I32, gates_r.shape, 0)
    gc = jnp.where(lane < hm, gates_c, _log_sigmoid(gates_c))
    gr = jnp.where(row < hm, gates_r, _log_sigmoid(gates_r))

    ri = lax.broadcasted_iota(I32, (lc, lc), 0)
    ci = lax.broadcasted_iota(I32, (lc, lc), 1)
    lower = (ri >= ci).astype(BF16)
    upper = (ri <= ci).astype(BF16)
    gc_ref[...] = gc
    gr_ref[...] = gr
    lane_c = lax.broadcasted_iota(I32, (lc, 2 * hm), 1)
    row_c = lax.broadcasted_iota(I32, (2 * hm, lc), 0)
    for c in range(tm // lc):
        rs = slice(c * lc, (c + 1) * lc)
        gcc = gc_ref[rs, :]
        grc = gr_ref[:, rs]
        cum_c = sum(_dot(lower, part) for part in _split3(gcc))
        cum_r = sum(_dot(part, upper) for part in _split3(grc))
        gc_ref[rs, :] = jnp.where(lane_c < hm, gcc, cum_c)
        gr_ref[:, rs] = jnp.where(row_c < hm, grc, cum_r)


def _mlstm_prep(xm, conv_w, conv_b, w_qk_m, w_v_m, w_if, b_if):
    s = xm.shape[0]
    tm = 512
    hpt = tm // HALO_ROWS
    full = lambda shape: pl.BlockSpec(shape, lambda i: (0,) * len(shape))
    row_spec = pl.BlockSpec((tm, D_M), lambda i: (i, 0))
    return pl.pallas_call(
        _mlstm_prep_kernel,
        out_shape=(jax.ShapeDtypeStruct((s, D_M), BF16),) * 3
        + (jax.ShapeDtypeStruct((s, 2 * N_HEADS_M), F32), jax.ShapeDtypeStruct((2 * N_HEADS_M, s), F32)),
        grid=(s // tm,),
        in_specs=[
            row_spec,
            pl.BlockSpec((HALO_ROWS, D_M), lambda i: (jnp.maximum(i * hpt - 1, 0), 0)),
            full((CONV_K, D_M)), full((1, D_M)),
            full((2, N_HEADS_M, HEAD_DIM_M, HEAD_DIM_M)), full((N_HEADS_M, HEAD_DIM_M, HEAD_DIM_M)),
            full((3 * D_M, 2 * N_HEADS_M)), full((2 * N_HEADS_M, 3 * D_M)),
            full((1, 2 * N_HEADS_M)), full((2 * N_HEADS_M, 1)),
        ],
        out_specs=(row_spec, row_spec, row_spec,
                   pl.BlockSpec((tm, 2 * N_HEADS_M), lambda i: (i, 0)),
                   pl.BlockSpec((2 * N_HEADS_M, tm), lambda i: (0, i))),
        compiler_params=_params(("arbitrary",), 32),
        name="mlstm_prep",
    )(xm, xm, conv_w, conv_b.reshape(1, D_M), w_qk_m, w_v_m, w_if, w_if.T,
      b_if.reshape(1, -1), b_if.reshape(-1, 1))


def _mlstm_scan_kernel(q_ref, k_ref, v_ref, gc_ref, gr_ref, z_ref, g_ref, y_ref, c_ref, n_ref, m_ref, *, chunks):
    hm, hd, lc = N_HEADS_M, HEAD_DIM_M, MLSTM_CHUNK

    @pl.when(pl.program_id(0) == 0)
    def _init():
        c_ref[...] = jnp.zeros_like(c_ref)
        n_ref[...] = jnp.zeros_like(n_ref)
        m_ref[...] = jnp.zeros_like(m_ref)

    ri = lax.broadcasted_iota(I32, (lc, lc), 0)
    ci = lax.broadcasted_iota(I32, (lc, lc), 1)
    causal = ri >= ci
    for c in range(chunks):
        rs = slice(c * lc, (c + 1) * lc)
        gc = gc_ref[rs, :]
        gr = gr_ref[:, rs]
        for h in range(hm):
            hs = slice(h * hd, (h + 1) * hd)
            q = q_ref[rs, hs]
            k = k_ref[rs, hs]
            v = v_ref[rs, hs]
            i_col, b_col = gc[:, h:h + 1], gc[:, hm + h:hm + h + 1]
            i_row, b_row = gr[h:h + 1, :], gr[hm + h:hm + h + 1, :]
            m_prev = m_ref[h:h + 1, 0:1]
            n_prev = n_ref[h:h + 1, :]
            c_prev = c_ref[h]

            dm = jnp.where(causal, b_col - b_row + i_row, NEG)
            inter = b_col + m_prev
            m_loc = jnp.maximum(inter, jnp.max(dm, axis=-1, keepdims=True))
            dexp = jnp.exp(dm - m_loc)
            inter_w = jnp.exp(inter - m_loc)
            sc = _dot_nt(q, k) * dexp
            num = inter_w * _dot(q, c_prev.astype(BF16)) + _dot(sc.astype(BF16), v)
            nq = (inter_w * jnp.sum(q.astype(F32) * n_prev, axis=-1, keepdims=True)
                  + jnp.sum(sc, axis=-1, keepdims=True))
            hval = num / jnp.maximum(jnp.abs(nq), jnp.exp(-m_loc))

            b_last = b_col[lc - 1:lc, :]
            g_col = b_last - b_col + i_col
            m_new = jnp.maximum(b_last + m_prev, jnp.max(g_col, axis=0, keepdims=True))
            decay = jnp.exp(b_last + m_prev - m_new)
            kw = k.astype(F32) * jnp.exp(g_col - m_new)
            c_ref[h] = decay * c_prev + _dot_tn(kw.astype(BF16), v)
            n_ref[h:h + 1, :] = decay * n_prev + jnp.sum(kw, axis=0, keepdims=True)
            m_ref[h:h + 1, :] = jnp.broadcast_to(m_new, (1, LANES))

            mu = jnp.mean(hval, axis=-1, keepdims=True)
            cen = hval - mu
            var = jnp.mean(cen * cen, axis=-1, keepdims=True)
            hn = cen * lax.rsqrt(var + LN_EPS) * g_ref[:, hs]
            y_ref[rs, hs] = (_sigmoid(z_ref[rs, hs].astype(F32)) * hn).astype(y_ref.dtype)


def _mlstm_scan(q, k, v, gc, gr, z, m_norm_g):
    s = q.shape[0]
    chunks = 2
    tm = chunks * MLSTM_CHUNK
    row_spec = pl.BlockSpec((tm, D_M), lambda i: (i, 0))
    return pl.pallas_call(
        functools.partial(_mlstm_scan_kernel, chunks=chunks),
        out_shape=jax.ShapeDtypeStruct((s, D_M), BF16),
        grid=(s // tm,),
        in_specs=[row_spec, row_spec, row_spec,
                  pl.BlockSpec((tm, 2 * N_HEADS_M), lambda i: (i, 0)),
                  pl.BlockSpec((2 * N_HEADS_M, tm), lambda i: (0, i)),
                  row_spec,
                  pl.BlockSpec((1, D_M), lambda i: (0, 0))],
        out_specs=row_spec,
        scratch_shapes=[pltpu.VMEM((N_HEADS_M, HEAD_DIM_M, HEAD_DIM_M), F32),
                        pltpu.VMEM((8, HEAD_DIM_M), F32),
                        pltpu.VMEM((8, LANES), F32)],
        compiler_params=_params(("arbitrary",), 32),
        name="mlstm_scan",
    )(q, k, v, gc, gr, z, m_norm_g.reshape(1, D_M))


def _merge_kernel(x_ref, o1_ref, o2_ref, o3_ref, l1_ref, l2_ref, l3_ref, ym_ref,
                  wg_ref, bg_ref, wa_ref, wm_ref, wo_ref, lng_ref, lnb_ref, out_ref,
                  wgb_ref, wab_ref, wmb_ref, wob_ref):
    @pl.when(pl.program_id(0) == 0)
    def _cast_weights():
        wgb_ref[...] = wg_ref[...].astype(BF16)
        wab_ref[...] = wa_ref[...].astype(BF16)
        wmb_ref[...] = wm_ref[...].astype(BF16)
        wob_ref[...] = wo_ref[...].astype(BF16)

    x = x_ref[...]
    xb = x.astype(BF16)
    l1, l2, l3 = l1_ref[...], l2_ref[...], l3_ref[...]
    mx = jnp.maximum(jnp.maximum(l1, l2), l3)
    e1, e2, e3 = jnp.exp(l1 - mx), jnp.exp(l2 - mx), jnp.exp(l3 - mx)
    inv = 1.0 / (e1 + e2 + e3)
    hrow = lax.broadcasted_iota(I32, (LANES, D_A), 0)
    hcol = lax.broadcasted_iota(I32, (LANES, D_A), 1) // HEAD_DIM_A
    expand = (hrow == hcol).astype(BF16)
    ya = jnp.zeros((x.shape[0], D_A), F32)
    for e, o_ref in ((e1, o1_ref), (e2, o2_ref), (e3, o3_ref)):
        w = sum(_dot(part, expand) for part in _split3(e * inv))
        ya = ya + w * o_ref[...].astype(F32)

    gate = _sigmoid(_dot(xb, wgb_ref[...]) + bg_ref[...])
    merged = (gate[:, :D_MODEL] * _dot(ya.astype(BF16), wab_ref[...])
              + gate[:, D_MODEL:] * _dot(ym_ref[...], wmb_ref[...]))
    y = _dot(merged.astype(BF16), wob_ref[...])
    out_ref[...] = _layer_norm(ALPHA * x + y, lng_ref[...], lnb_ref[...])


def _merge(x, outs, lses, ym, w_gate, b_gate, w_br_a, w_br_m, w_o, ln_g, ln_b):
    s = x.shape[0]
    tm = 512
    res = lambda shape: pl.BlockSpec(shape, lambda i: (0, 0), pipeline_mode=pl.Buffered(1))
    small = lambda n: pl.BlockSpec((1, n), lambda i: (0, 0))
    rows = lambda n: pl.BlockSpec((tm, n), lambda i: (i, 0))
    return pl.pallas_call(
        _merge_kernel,
        out_shape=jax.ShapeDtypeStruct((s, D_MODEL), F32),
        grid=(s // tm,),
        in_specs=[rows(D_MODEL), rows(D_A), rows(D_A), rows(D_A), rows(LANES), rows(LANES), rows(LANES), rows(D_M),
                  res((D_MODEL, 2 * D_MODEL)), small(2 * D_MODEL), res((D_A, D_MODEL)), res((D_M, D_MODEL)),
                  res((D_MODEL, D_MODEL)), small(D_MODEL), small(D_MODEL)],
        out_specs=rows(D_MODEL),
        scratch_shapes=[pltpu.VMEM((D_MODEL, 2 * D_MODEL), BF16), pltpu.VMEM((D_A, D_MODEL), BF16),
                        pltpu.VMEM((D_M, D_MODEL), BF16), pltpu.VMEM((D_MODEL, D_MODEL), BF16)],
        compiler_params=_params(("arbitrary",), 56),
        name="merge",
    )(x, *outs, *lses, ym, w_gate, b_gate.reshape(1, -1), w_br_a, w_br_m, w_o,
      ln_g.reshape(1, -1), ln_b.reshape(1, -1))


def _token_mixer_layer(x, rel_bias, w_in, w_gate, b_gate, conv_w, conv_b, w_qk_m, w_v_m, w_if, b_if, m_norm_g,
                       w_br_a, w_br_m, w_o, ln_g, ln_b):
    proj = _in_proj(x, w_in)
    q, k, v, xm, zm = (proj[j] for j in range(N_PROJ))
    outs, lses = [], []
    for window, dilation in ATTN_PATTERNS:
        o, lse = _attn_pattern(q, k, v, _attn_bias(rel_bias, window, dilation), dilation)
        outs.append(o)
        lses.append(lse)
    qm, km, vm, gc, gr = _mlstm_prep(xm, conv_w, conv_b, w_qk_m, w_v_m, w_if, b_if)
    ym = _mlstm_scan(qm, km, vm, gc, gr, zm, m_norm_g)
    return _merge(x, outs, lses, ym, w_gate, b_gate, w_br_a, w_br_m, w_o, ln_g, ln_b)


def _ffn_kernel(x_ref, w1_ref, w3_ref, w2_ref, lng_ref, lnb_ref, out_ref, xb_ref, acc_ref):
    f = pl.program_id(1)

    @pl.when(f == 0)
    def _start():
        xb_ref[...] = x_ref[...].astype(BF16)
        acc_ref[...] = jnp.zeros_like(acc_ref)

    xb = xb_ref[...]
    a = _dot(xb, w1_ref[...].astype(BF16))
    g = _dot(xb, w3_ref[...].astype(BF16))
    hidden = (a * _sigmoid(a) * g).astype(BF16)
    acc_ref[...] += _dot(hidden, w2_ref[...].astype(BF16))

    @pl.when(f == pl.num_programs(1) - 1)
    def _finish():
        out_ref[...] = _layer_norm(ALPHA * x_ref[...] + acc_ref[...], lng_ref[...], lnb_ref[...])


def _dense_ffn(x, w13, w2, ln_g, ln_b):
    s = x.shape[0]
    tm = min(FFN_ROW_TILE, s)
    fc = FFN_FF_CHUNK
    nf = D_FF // fc
    small = pl.BlockSpec((1, D_MODEL), lambda i, f: (0, 0))
    return pl.pallas_call(
        _ffn_kernel,
        out_shape=jax.ShapeDtypeStruct((s, D_MODEL), F32),
        grid=(s // tm, nf),
        in_specs=[pl.BlockSpec((tm, D_MODEL), lambda i, f: (i, 0)),
                  pl.BlockSpec((D_MODEL, fc), lambda i, f: (0, f)),
                  pl.BlockSpec((D_MODEL, fc), lambda i, f: (0, nf + f)),
                  pl.BlockSpec((fc, D_MODEL), lambda i, f: (f, 0)),
                  small, small],
        out_specs=pl.BlockSpec((tm, D_MODEL), lambda i, f: (i, 0)),
        scratch_shapes=[pltpu.VMEM((tm, D_MODEL), BF16), pltpu.VMEM((tm, D_MODEL), F32)],
        compiler_params=_params(("arbitrary", "arbitrary"), 48),
        name="dense_ffn",
    )(x, w13, w13, w2, ln_g.reshape(1, -1), ln_b.reshape(1, -1))


def _router_kernel(x_ref, rw_ref, rb_ref, gate_ref, rank_ref, cnt_ref, carry_ref):
    tm = x_ref.shape[0]
    ne = N_EXPERTS

    @pl.when(pl.program_id(0) == 0)
    def _init():
        carry_ref[...] = jnp.zeros_like(carry_ref)

    xs = _split3(x_ref[...])
    ws = _split3(rw_ref[...])
    logits = rb_ref[...] + sum(_dot(xs[a], ws[b]) for a, b in ((2, 0), (0, 2), (1, 1), (1, 0), (0, 1), (0, 0)))
    lane = lax.broadcasted_iota(I32, (tm, ne), 1)
    v1 = jnp.max(logits, axis=-1, keepdims=True)
    i1 = jnp.min(jnp.where(logits == v1, lane, ne), axis=-1, keepdims=True)
    rest = jnp.where(lane == i1, -jnp.inf, logits)
    v2 = jnp.max(rest, axis=-1, keepdims=True)
    i2 = jnp.min(jnp.where(rest == v2, lane, ne), axis=-1, keepdims=True)
    e2 = jnp.exp(v2 - v1)
    den = 1.0 + e2
    sel1, sel2 = lane == i1, lane == i2
    gate_ref[...] = jnp.where(sel1, 1.0 / den, 0.0) + jnp.where(sel2, e2 / den, 0.0)
    sel = jnp.where(sel1 | sel2, 1.0, 0.0)
    ri = lax.broadcasted_iota(I32, (tm, tm), 0)
    ci = lax.broadcasted_iota(I32, (tm, tm), 1)
    before = (ri > ci).astype(BF16)
    carry = carry_ref[0:1, 0:ne]
    rank = _dot(before, sel.astype(BF16)) + carry
    rank_ref[...] = jnp.where(sel > 0.0, rank, -1.0)
    total = carry + jnp.sum(sel, axis=0, keepdims=True)
    carry_ref[0:1, 0:ne] = total
    cnt_ref[...] = total


def _router(x, router_w, router_b):
    s = x.shape[0]
    tm = 512
    ne = N_EXPERTS
    return pl.pallas_call(
        _router_kernel,
        out_shape=(jax.ShapeDtypeStruct((s, ne), F32), jax.ShapeDtypeStruct((s, ne), F32),
                   jax.ShapeDtypeStruct((1, ne), F32)),
        grid=(s // tm,),
        in_specs=[pl.BlockSpec((tm, D_MODEL), lambda i: (i, 0)),
                  pl.BlockSpec((D_MODEL, ne), lambda i: (0, 0)),
                  pl.BlockSpec((1, ne), lambda i: (0, 0))],
        out_specs=(pl.BlockSpec((tm, ne), lambda i: (i, 0)), pl.BlockSpec((tm, ne), lambda i: (i, 0)),
                   pl.BlockSpec((1, ne), lambda i: (0, 0))),
        scratch_shapes=[pltpu.VMEM((8, LANES), F32)],
        compiler_params=_params(("arbitrary",), 32),
        name="moe_router",
    )(x, router_w, router_b.reshape(1, ne))


def _dispatch_kernel(tile_ref, run_ref, x_ref, pos_ref, gate_ref, xs_ref, gs_ref, acc_ref, gacc_ref, *, n_items):
    rt = MOE_ROW_TILE
    e = pl.program_id(0)
    idx = (e * pl.num_programs(1) + pl.program_id(1)) * pl.num_programs(2) + pl.program_id(2)
    tile = tile_ref[idx]
    first = jnp.logical_or(idx == 0, tile_ref[jnp.maximum(idx - 1, 0)] != tile)
    last = jnp.logical_or(idx == n_items - 1, tile_ref[jnp.minimum(idx + 1, n_items - 1)] != tile)

    @pl.when(first)
    def _zero():
        acc_ref[...] = jnp.zeros_like(acc_ref)
        gacc_ref[...] = jnp.zeros_like(gacc_ref)

    @pl.when(run_ref[idx] == 1)
    def _accumulate():
        pos = pos_ref[pl.ds(e, 1), :]
        row = (tile * rt).astype(F32) + lax.broadcasted_iota(I32, (rt, rt), 0).astype(F32)
        hit = row == pos
        acc_ref[...] += _dot(jnp.where(hit, 1.0, 0.0).astype(BF16), x_ref[...].astype(BF16))
        gacc_ref[...] += jnp.sum(jnp.where(hit, gate_ref[pl.ds(e, 1), :], 0.0), axis=-1, keepdims=True)

    @pl.when(last)
    def _write():
        xs_ref[...] = acc_ref[...].astype(xs_ref.dtype)
        gs_ref[...] = gacc_ref[...]


def _combine_kernel(tile_ref, run_ref, x_ref, y_ref, pos_ref, lng_ref, lnb_ref, out_ref, acc_ref):
    rt = MOE_ROW_TILE
    e, kk = pl.program_id(1), pl.program_id(2)
    idx = (pl.program_id(0) * pl.num_programs(1) + e) * pl.num_programs(2) + kk

    @pl.when(jnp.logical_and(e == 0, kk == 0))
    def _zero():
        acc_ref[...] = jnp.zeros_like(acc_ref)

    @pl.when(run_ref[idx] == 1)
    def _accumulate():
        lane = lax.broadcasted_iota(I32, pos_ref.shape, 1)
        pos = jnp.max(jnp.where(lane == e, pos_ref[...], -1.0), axis=-1, keepdims=True)
        col = (tile_ref[idx] * rt).astype(F32) + lax.broadcasted_iota(I32, (rt, rt), 1).astype(F32)
        acc_ref[...] += _dot(jnp.where(pos == col, 1.0, 0.0).astype(BF16), y_ref[...])

    @pl.when(jnp.logical_and(e == pl.num_programs(1) - 1, kk == pl.num_programs(2) - 1))
    def _finish():
        out_ref[...] = _layer_norm(ALPHA * x_ref[...] + acc_ref[...], lng_ref[...], lnb_ref[...])


def _moe_gemm_kernel(te_ref, nu_ref, xs_ref, gs_ref, w1_ref, w3_ref, w2_ref, y_ref, acc_ref):
    f = pl.program_id(1)

    @pl.when(pl.program_id(0) < nu_ref[0])
    def _tile():
        @pl.when(f == 0)
        def _zero():
            acc_ref[...] = jnp.zeros_like(acc_ref)

        xb = xs_ref[...]
        a = _dot(xb, w1_ref[0].astype(BF16))
        g = _dot(xb, w3_ref[0].astype(BF16))
        hidden = (a * _sigmoid(a) * g).astype(BF16)
        acc_ref[...] += _dot(hidden, w2_ref[0].astype(BF16))

        @pl.when(f == pl.num_programs(1) - 1)
        def _finish():
            y_ref[...] = (acc_ref[...] * gs_ref[...]).astype(y_ref.dtype)


def _moe_layer(x, router_w, router_b, w13, w2, ln_g, ln_b):
    s = x.shape[0]
    ne, rt, gt = N_EXPERTS, MOE_ROW_TILE, MOE_GROUP_TILE
    sub = gt // rt
    nj = s // rt
    n_group_tiles = (TOP_K * s) // gt + ne
    n_row_tiles = n_group_tiles * sub

    gate, rank, counts = _router(x, router_w, router_b)

    cnt = counts[0].astype(I32)
    padded = ((cnt + gt - 1) // gt) * gt
    off = jnp.cumsum(padded) - padded
    n_used = (jnp.sum(padded) // gt).astype(I32)
    tile_expert = jnp.clip(
        jnp.searchsorted(jnp.cumsum(padded), jnp.arange(n_group_tiles, dtype=I32) * gt, side="right"), 0, ne - 1
    ).astype(I32)
    pos = jnp.where(rank >= 0.0, rank + off.astype(F32)[None, :], -1.0)
    pos_t = pos.T
    gate_t = gate.T

    sel_cnt = jnp.cumsum(jnp.sum((rank >= 0.0).reshape(nj, rt, ne), axis=1), axis=0).astype(I32)
    cum = jnp.concatenate([jnp.zeros((1, ne), I32), sel_cnt], axis=0).T
    lo = off[:, None] + cum[:, :-1]
    hi = off[:, None] + cum[:, 1:]
    nonempty = hi > lo
    last_tile = jnp.maximum((off + padded) // rt - 1, 0)
    data_tile = jnp.where(cnt > 0, (off + cnt - 1) // rt, last_tile)
    t0 = jnp.minimum(lo // rt, last_tile[:, None])
    straddle = nonempty & ((hi - 1) // rt > lo // rt)
    t1 = jnp.where(straddle, t0 + 1, t0)

    d_tiles = jnp.stack([t0, t1], axis=-1)
    d_run = jnp.stack([nonempty, straddle], axis=-1)
    n_fill = sub - 1
    fill = jnp.minimum(data_tile[:, None] + 1 + jnp.arange(n_fill, dtype=I32)[None, :], last_tile[:, None])
    fill = jnp.repeat(fill[:, :, None], 2, axis=-1)
    d_tiles = jnp.concatenate([d_tiles, fill], axis=1).reshape(-1).astype(I32)
    d_run = jnp.concatenate([d_run, jnp.zeros((ne, n_fill, 2), bool)], axis=1).reshape(-1).astype(I32)
    nj_d = nj + n_fill
    n_items = ne * nj_d * 2

    xs, gs = pl.pallas_call(
        functools.partial(_dispatch_kernel, n_items=n_items),
        out_shape=(jax.ShapeDtypeStruct((n_row_tiles * rt, D_MODEL), BF16),
                   jax.ShapeDtypeStruct((n_row_tiles * rt, 1), F32)),
        grid_spec=pltpu.PrefetchScalarGridSpec(
            num_scalar_prefetch=2,
            grid=(ne, nj_d, 2),
            in_specs=[pl.BlockSpec((rt, D_MODEL), lambda e, j, k, t, r: (jnp.minimum(j, nj - 1), 0)),
                      pl.BlockSpec((ne, rt), lambda e, j, k, t, r: (0, jnp.minimum(j, nj - 1))),
                      pl.BlockSpec((ne, rt), lambda e, j, k, t, r: (0, jnp.minimum(j, nj - 1)))],
            out_specs=(pl.BlockSpec((rt, D_MODEL), lambda e, j, k, t, r: (t[(e * nj_d + j) * 2 + k], 0)),
                       pl.BlockSpec((rt, 1), lambda e, j, k, t, r: (t[(e * nj_d + j) * 2 + k], 0))),
            scratch_shapes=[pltpu.VMEM((rt, D_MODEL), F32), pltpu.VMEM((rt, 1), F32)]),
        compiler_params=_params(("arbitrary", "arbitrary", "arbitrary"), 32),
        name="moe_dispatch",
    )(d_tiles, d_run, x, pos_t, gate_t)

    fc = MOE_FF_CHUNK
    nf = D_FF_E // fc
    tile_of = lambda i, nu: jnp.minimum(i, nu[0] - 1)
    chunk_of = lambda i, f, nu: jnp.where(i < nu[0], f, nf - 1)
    ys = pl.pallas_call(
        _moe_gemm_kernel,
        out_shape=jax.ShapeDtypeStruct((n_row_tiles * rt, D_MODEL), BF16),
        grid_spec=pltpu.PrefetchScalarGridSpec(
            num_scalar_prefetch=2,
            grid=(n_group_tiles, nf),
            in_specs=[pl.BlockSpec((gt, D_MODEL), lambda i, f, te, nu: (tile_of(i, nu), 0)),
                      pl.BlockSpec((gt, 1), lambda i, f, te, nu: (tile_of(i, nu), 0)),
                      pl.BlockSpec((1, D_MODEL, fc), lambda i, f, te, nu: (te[tile_of(i, nu)], 0, chunk_of(i, f, nu))),
                      pl.BlockSpec((1, D_MODEL, fc),
                                   lambda i, f, te, nu: (te[tile_of(i, nu)], 0, nf + chunk_of(i, f, nu))),
                      pl.BlockSpec((1, fc, D_MODEL), lambda i, f, te, nu: (te[tile_of(i, nu)], chunk_of(i, f, nu), 0))],
            out_specs=pl.BlockSpec((gt, D_MODEL), lambda i, f, te, nu: (tile_of(i, nu), 0)),
            scratch_shapes=[pltpu.VMEM((gt, D_MODEL), F32)]),
        compiler_params=_params(("arbitrary", "arbitrary"), 48),
        name="moe_gemm",
    )(tile_expert, n_used.reshape(1), xs, gs, w13, w13, w2)

    c0 = jnp.minimum(lo // rt, last_tile[:, None])
    c_tiles = jnp.stack([c0, jnp.where(straddle, c0 + 1, c0)], axis=-1).transpose(1, 0, 2).reshape(-1).astype(I32)
    c_run = jnp.stack([nonempty, straddle], axis=-1).transpose(1, 0, 2).reshape(-1).astype(I32)
    small = pl.BlockSpec((1, D_MODEL), lambda j, e, k, t, r: (0, 0))
    return pl.pallas_call(
        _combine_kernel,
        out_shape=jax.ShapeDtypeStruct((s, D_MODEL), F32),
        grid_spec=pltpu.PrefetchScalarGridSpec(
            num_scalar_prefetch=2,
            grid=(nj, ne, 2),
            in_specs=[pl.BlockSpec((rt, D_MODEL), lambda j, e, k, t, r: (j, 0)),
                      pl.BlockSpec((rt, D_MODEL), lambda j, e, k, t, r: (t[(j * ne + e) * 2 + k], 0)),
                      pl.BlockSpec((rt, ne), lambda j, e, k, t, r: (j, 0)),
                      small, small],
            out_specs=pl.BlockSpec((rt, D_MODEL), lambda j, e, k, t, r: (j, 0)),
            scratch_shapes=[pltpu.VMEM((rt, D_MODEL), F32)]),
        compiler_params=_params(("arbitrary", "arbitrary", "arbitrary"), 32),
        name="moe_combine",
    )(c_tiles, c_run, x, ys, pos, ln_g.reshape(1, -1), ln_b.reshape(1, -1))


def kernel(x, rel_bias, w_in, w_gate, b_gate, conv_w, conv_b, w_qk_m, w_v_m, w_if, b_if, m_norm_g, w_br_a, w_br_m,
           w_o, ln_g, ln_b, ffn_w13, ffn_w2, router_w, router_b, exp_w13, exp_w2):
    batch, seq, _ = x.shape
    assert batch == 1
    h = x.reshape(seq, D_MODEL)
    for l in range(DEPTH):
        h = _token_mixer_layer(h, rel_bias, w_in[l], w_gate[l], b_gate[l], conv_w[l], conv_b[l], w_qk_m[l],
                               w_v_m[l], w_if[l], b_if[l], m_norm_g[l], w_br_a[l], w_br_m[l], w_o[l],
                               ln_g[l, 0], ln_b[l, 0])
        j = l // 2
        if l % 2 == 0:
            h = _dense_ffn(h, ffn_w13[j], ffn_w2[j], ln_g[l, 1], ln_b[l, 1])
        else:
            h = _moe_layer(h, router_w[j], router_b[j], exp_w13[j], exp_w2[j], ln_g[l, 1], ln_b[l, 1])
    return h.reshape(batch, seq, D_MODEL)
```

```python
import functools
import math

import jax
import jax.numpy as jnp
from jax import lax
from jax.experimental import pallas as pl
from jax.experimental.pallas import tpu as pltpu

F32 = jnp.float32
BF16 = jnp.bfloat16
I32 = jnp.int32

D_MODEL = 1024
DEPTH = 2
N_HEADS_A = 8
HEAD_DIM_A = 64
D_A = N_HEADS_A * HEAD_DIM_A
ATTN_PATTERNS = ((128, 1), (512, 4), (2048, 16))
ATTN_BLOCK = 128
NUM_BUCKETS = 32
MAX_DISTANCE = 2048
N_HEADS_M = 4
HEAD_DIM_M = 128
D_M = N_HEADS_M * HEAD_DIM_M
CONV_K = 4
MLSTM_CHUNK = 128
N_PROJ = 5
P_IN = 3 * D_A + 2 * D_M
D_FF = 2816
N_EXPERTS = 8
TOP_K = 2
D_FF_E = 3584
ALPHA = (2.0 * DEPTH) ** 0.25
LN_EPS = 1e-5

NEG = -1e30
LANES = 128
HALO_ROWS = 16
MIB = 1024 * 1024

MOE_ROW_TILE = 512
MOE_GROUP_TILE = 1024
MOE_FF_CHUNK = 512
FFN_ROW_TILE = 1024
FFN_FF_CHUNK = 256


def _params(sem, vmem_mib):
    return pltpu.CompilerParams(dimension_semantics=sem, vmem_limit_bytes=vmem_mib * MIB)


def _sigmoid(x):
    return 1.0 / (1.0 + jnp.exp(-x))


def _layer_norm(r, g, b):
    mu = jnp.mean(r, axis=-1, keepdims=True)
    c = r - mu
    var = jnp.mean(c * c, axis=-1, keepdims=True)
    return c * lax.rsqrt(var + LN_EPS) * g + b


def _split3(a):
    hi = a.astype(BF16)
    r1 = a - hi.astype(F32)
    mid = r1.astype(BF16)
    lo = (r1 - mid.astype(F32)).astype(BF16)
    return hi, mid, lo


def _dot(a, b):
    return jnp.dot(a, b, preferred_element_type=F32)


def _dot_nt(a, b):
    return lax.dot_general(a, b, (((1,), (1,)), ((), ())), preferred_element_type=F32)


def _dot_tn(a, b):
    return lax.dot_general(a, b, (((0,), (0,)), ((), ())), preferred_element_type=F32)


IN_PROJ_ROWS = 1024
DILATIONS = tuple(d for _, d in ATTN_PATTERNS if d > 1)


def _in_proj_kernel(x_ref, w_ref, *refs):
    nat = refs[:N_PROJ]
    perm = refs[N_PROJ:N_PROJ + 3 * len(DILATIONS)]
    wb_ref, y_ref = refs[-2:]
    tm = x_ref.shape[0]

    @pl.when(pl.program_id(0) == 0)
    def _cast_weights():
        wb_ref[...] = w_ref[...].astype(BF16)

    xb = x_ref[...].astype(BF16)
    for j in range(N_PROJ):
        y = _dot(xb, wb_ref[:, j * D_A:(j + 1) * D_A])
        if j == 0:
            y = y * (HEAD_DIM_A ** -0.5)
        nat[j][...] = y.astype(BF16)
        if j >= 3:
            continue
        for c in range(D_A // LANES):
            y_ref[c] = y[:, c * LANES:(c + 1) * LANES]
        for di, d in enumerate(DILATIONS):
            out = perm[di * 3 + j]
            tiles, _, rpc, _ = out.shape
            for t in range(tiles):
                for r in range(d):
                    for c in range(D_A // LANES):
                        out[t, r, :, c * LANES:(c + 1) * LANES] = (
                            y_ref[c, pl.ds(t * d * rpc + r, rpc, stride=d), :].astype(BF16))


def _in_proj(x, w_in, layer):
    s = x.shape[0]
    tm = IN_PROJ_ROWS
    blk = ATTN_BLOCK
    out_shape = [jax.ShapeDtypeStruct((s, D_A), BF16)] * N_PROJ
    out_specs = [pl.BlockSpec((tm, D_A), lambda i: (i, 0))] * N_PROJ
    for d in DILATIONS:
        tile = d * blk
        if tile <= tm:
            spec = pl.BlockSpec((tm // tile, d, blk, D_A), lambda i: (i, 0, 0, 0))
        else:
            parts = tile // tm
            spec = pl.BlockSpec((1, d, blk // parts, D_A), lambda i, parts=parts: (i // parts, 0, i % parts, 0))
        out_shape += [jax.ShapeDtypeStruct((s // tile, d, blk, D_A), BF16)] * 3
        out_specs += [spec] * 3
    outs = pl.pallas_call(
        _in_proj_kernel,
        out_shape=tuple(out_shape),
        grid=(s // tm,),
        in_specs=[
            pl.BlockSpec((tm, D_MODEL), lambda i: (i, 0)),
            pl.BlockSpec((None, D_MODEL, P_IN), lambda i: (layer, 0, 0), pipeline_mode=pl.Buffered(1)),
        ],
        out_specs=tuple(out_specs),
        scratch_shapes=[pltpu.VMEM((D_MODEL, P_IN), BF16), pltpu.VMEM((D_A // LANES, tm, LANES), F32)],
        compiler_params=_params(("arbitrary",), 56),
        name="in_proj",
    )(x, w_in)
    nat = outs[:N_PROJ]
    perm = [tuple(t.reshape(s, D_A) for t in outs[N_PROJ + 3 * i:N_PROJ + 3 * i + 3]) for i in range(len(DILATIONS))]
    return nat, perm


ATTN_STEP_BLOCKS = 16


def _rel_bucket(dist):
    exact = NUM_BUCKETS // 2
    d = jnp.maximum(dist, exact).astype(F32)
    log_b = exact + (jnp.log(d / exact) / math.log(MAX_DISTANCE / exact) * (NUM_BUCKETS - exact)).astype(I32)
    return jnp.where(dist < exact, dist, jnp.minimum(log_b, NUM_BUCKETS - 1))


def _attn_bias(rel_bias, window, dilation):
    blk = ATTN_BLOCK
    qi = jnp.arange(blk)[:, None]
    kj = jnp.arange(2 * blk)[None, :]
    rel = qi + blk - kj
    bucket = _rel_bucket(jnp.maximum(rel, 0) * dilation)
    onehot = (bucket[..., None] == jnp.arange(NUM_BUCKETS)).astype(F32)
    bias = jnp.einsum("qkb,bh->hqk", onehot, rel_bias.astype(F32), precision=lax.Precision.HIGHEST)
    mask = (rel >= 0) & (rel <= window // dilation)
    return jnp.where(mask[None], bias, NEG)


def _attn_kernel(q_ref, kp_ref, kc_ref, vp_ref, vc_ref, bias_ref, o_ref, st_ref, kf_ref, vf_ref, on_ref, sn_ref,
                 *, dilation):
    blk = ATTN_BLOCK
    nb = ATTN_STEP_BLOCKS
    rows = nb * blk
    step = pl.program_id(0)
    kf_ref[0:rows, :] = kp_ref[...]
    kf_ref[rows:, :] = kc_ref[...]
    vf_ref[0:rows, :] = vp_ref[...]
    vf_ref[rows:, :] = vc_ref[...]
    kj = lax.broadcasted_iota(I32, (blk, 2 * blk), 1)
    lane = lax.broadcasted_iota(I32, (blk, LANES), 1)
    lane_k = lax.broadcasted_iota(I32, (2 * blk, LANES), 1)
    ones = jnp.ones((2 * blk, LANES), BF16)

    def block(g, carry):
        cur = pl.multiple_of(g * blk, blk)
        prev = pl.multiple_of((nb + g - dilation) * blk, blk)
        pen = jnp.where(jnp.logical_and(step == 0, g < dilation), NEG, 0.0).astype(F32)
        prev_pen = jnp.where(kj < blk, pen, 0.0)
        stats = jnp.zeros((blk, LANES), F32)
        tile_i, cls = g // dilation, g % dilation
        dst = pl.ds(tile_i * (blk * dilation) + cls, blk, stride=dilation)
        for hp in range(N_HEADS_A // 2):
            ls = slice(hp * LANES, (hp + 1) * LANES)
            qb = q_ref[pl.ds(cur, blk), ls]
            kb = jnp.concatenate([kf_ref[pl.ds(prev, blk), ls], kf_ref[pl.ds(rows + cur, blk), ls]], axis=0)
            vb = jnp.concatenate([vf_ref[pl.ds(prev, blk), ls], vf_ref[pl.ds(rows + cur, blk), ls]], axis=0)
            acc = jnp.zeros((blk, LANES), F32)
            for par in range(2):
                h = 2 * hp + par
                own_q = (lane // HEAD_DIM_A) == par
                own_k = (lane_k // HEAD_DIM_A) == par
                qm = jnp.where(own_q, qb, jnp.zeros_like(qb))
                logits = _dot_nt(qm, kb) + bias_ref[h] + prev_pen
                m = jnp.max(logits, axis=-1, keepdims=True)
                p = jnp.exp(logits - m).astype(BF16)
                vm = jnp.where(own_k, vb, jnp.zeros_like(vb))
                ol = _dot(p, jnp.concatenate([vm, ones], axis=1))
                acc = acc + ol[:, :LANES]
                stats = jnp.where(lane == h, m, stats)
                stats = jnp.where(lane == N_HEADS_A + h, ol[:, LANES:], stats)
            on_ref[hp, dst, :] = acc
        sn_ref[dst, :] = stats
        return carry

    lax.fori_loop(0, nb, block, 0)
    for hp in range(N_HEADS_A // 2):
        o_ref[:, hp * LANES:(hp + 1) * LANES] = on_ref[hp].astype(o_ref.dtype)
    st_ref[...] = sn_ref[...]


def _attn_pattern(q, k, v, bias, dilation):
    s = q.shape[0]
    rows = ATTN_STEP_BLOCKS * ATTN_BLOCK
    cur = pl.BlockSpec((rows, D_A), lambda i: (i, 0))
    prev = pl.BlockSpec((rows, D_A), lambda i: (jnp.maximum(i - 1, 0), 0))
    return pl.pallas_call(
        functools.partial(_attn_kernel, dilation=dilation),
        out_shape=(jax.ShapeDtypeStruct((s, D_A), BF16), jax.ShapeDtypeStruct((s, LANES), F32)),
        grid=(s // rows,),
        in_specs=[cur, prev, cur, prev, cur,
                  pl.BlockSpec((N_HEADS_A, ATTN_BLOCK, 2 * ATTN_BLOCK), lambda i: (0, 0, 0))],
        out_specs=(pl.BlockSpec((rows, D_A), lambda i: (i, 0)), pl.BlockSpec((rows, LANES), lambda i: (i, 0))),
        scratch_shapes=[pltpu.VMEM((2 * rows, D_A), BF16), pltpu.VMEM((2 * rows, D_A), BF16),
                        pltpu.VMEM((D_A // LANES, rows, LANES), F32), pltpu.VMEM((rows, LANES), F32)],
        compiler_params=_params(("arbitrary",), 48),
        name=f"attn_d{dilation}",
    )(q, k, k, v, v, bias)


def _log_sigmoid(x):
    return jnp.minimum(x, 0.0) - jnp.log(1.0 + jnp.exp(-jnp.abs(x)))


def _mlstm_prep_kernel(xm_ref, halo_ref, cw_ref, cb_ref, wqk_ref, wv_ref, wif_ref, wift_ref, bif_ref, bift_ref,
                       q_ref, k_ref, v_ref, gc_ref, gr_ref):
    tm = xm_ref.shape[0]
    hm, hd, lc = N_HEADS_M, HEAD_DIM_M, MLSTM_CHUNK
    xmb = xm_ref[...]
    halo = jnp.where(pl.program_id(0) == 0, 0.0, halo_ref[...].astype(F32))
    xx = jnp.concatenate([halo, xmb.astype(F32)], axis=0)
    conv = jnp.zeros((tm, D_M), F32) + cb_ref[...]
    for j in range(CONV_K):
        start = HALO_ROWS - (CONV_K - 1) + j
        conv = conv + cw_ref[j:j + 1, :] * xx[start:start + tm, :]
    xcb = (conv * _sigmoid(conv)).astype(BF16)

    qs, ks, vs = [], [], []
    for h in range(hm):
        sl = slice(h * hd, (h + 1) * hd)
        qs.append(_dot(xcb[:, sl], wqk_ref[0, h].astype(BF16)))
        ks.append(_dot(xcb[:, sl], wqk_ref[1, h].astype(BF16)))
        vs.append(_dot(xmb[:, sl], wv_ref[h].astype(BF16)))
    q = jnp.concatenate(qs, axis=1)
    k = jnp.concatenate(ks, axis=1)
    v = jnp.concatenate(vs, axis=1)
    q_ref[...] = q.astype(BF16)
    k_ref[...] = (k * (hd ** -0.5)).astype(BF16)
    v_ref[...] = v.astype(BF16)

    qkv = jnp.concatenate([q, k, v], axis=1).astype(BF16)
    gates_c = _dot(qkv, wif_ref[...].astype(BF16)) + bif_ref[...]
    gates_r = _dot_nt(wift_ref[...].astype(BF16), qkv) + bift_ref[...]
    lane = lax.broadcasted_iota(I32, gates_c.shape, 1)
    row = lax.broadcasted_iota(I32, gates_r.shape, 0)
    gc = jnp.where(lane < hm, gates_c, _log_sigmoid(gates_c))
    gr = jnp.where(row < hm, gates_r, _log_sigmoid(gates_r))

    ri = lax.broadcasted_iota(I32, (lc, lc), 0)
    ci = lax.broadcasted_iota(I32, (lc, lc), 1)
    lower = (ri >= ci).astype(BF16)
    upper = (ri <= ci).astype(BF16)
    gc_ref[...] = gc
    gr_ref[...] = gr
    lane_c = lax.broadcasted_iota(I32, (lc, 2 * hm), 1)
    row_c = lax.broadcasted_iota(I32, (2 * hm, lc), 0)
    for c in range(tm // lc):
        rs = slice(c * lc, (c + 1) * lc)
        gcc = gc_ref[rs, :]
        grc = gr_ref[:, rs]
        cum_c = sum(_dot(lower, part) for part in _split3(gcc))
        cum_r = sum(_dot(part, upper) for part in _split3(grc))
        gc_ref[rs, :] = jnp.where(lane_c < hm, gcc, cum_c)
        gr_ref[:, rs] = jnp.where(row_c < hm, grc, cum_r)


def _mlstm_prep(xm, conv_w, conv_b, w_qk_m, w_v_m, w_if, b_if, layer):
    s = xm.shape[0]
    tm = 512
    hpt = tm // HALO_ROWS
    full = lambda shape: pl.BlockSpec(shape, lambda i: (0,) * len(shape))
    stacked = lambda shape: pl.BlockSpec((None,) + shape, lambda i: (layer,) + (0,) * len(shape))
    row_spec = pl.BlockSpec((tm, D_M), lambda i: (i, 0))
    return pl.pallas_call(
        _mlstm_prep_kernel,
        out_shape=(jax.ShapeDtypeStruct((s, D_M), BF16),) * 3
        + (jax.ShapeDtypeStruct((s, 2 * N_HEADS_M), F32), jax.ShapeDtypeStruct((2 * N_HEADS_M, s), F32)),
        grid=(s // tm,),
        in_specs=[
            row_spec,
            pl.BlockSpec((HALO_ROWS, D_M), lambda i: (jnp.maximum(i * hpt - 1, 0), 0)),
            full((CONV_K, D_M)), full((1, D_M)),
            stacked((2, N_HEADS_M, HEAD_DIM_M, HEAD_DIM_M)), stacked((N_HEADS_M, HEAD_DIM_M, HEAD_DIM_M)),
            full((3 * D_M, 2 * N_HEADS_M)), full((2 * N_HEADS_M, 3 * D_M)),
            full((1, 2 * N_HEADS_M)), full((2 * N_HEADS_M, 1)),
        ],
        out_specs=(row_spec, row_spec, row_spec,
                   pl.BlockSpec((tm, 2 * N_HEADS_M), lambda i: (i, 0)),
                   pl.BlockSpec((2 * N_HEADS_M, tm), lambda i: (0, i))),
        compiler_params=_params(("arbitrary",), 32),
        name="mlstm_prep",
    )(xm, xm, conv_w, conv_b.reshape(1, D_M), w_qk_m, w_v_m, w_if, w_if.T,
      b_if.reshape(1, -1), b_if.reshape(-1, 1))


def _mlstm_scan_kernel(q_ref, k_ref, v_ref, gc_ref, gr_ref, z_ref, g_ref, y_ref, c_ref, n_ref, m_ref, *, chunks):
    hm, hd, lc = N_HEADS_M, HEAD_DIM_M, MLSTM_CHUNK

    @pl.when(pl.program_id(0) == 0)
    def _init():
        c_ref[...] = jnp.zeros_like(c_ref)
        n_ref[...] = jnp.zeros_like(n_ref)
        m_ref[...] = jnp.zeros_like(m_ref)

    ri = lax.broadcasted_iota(I32, (lc, lc), 0)
    ci = lax.broadcasted_iota(I32, (lc, lc), 1)
    causal = ri >= ci
    for c in range(chunks):
        rs = slice(c * lc, (c + 1) * lc)
        gc = gc_ref[rs, :]
        gr = gr_ref[:, rs]
        for h in range(hm):
            hs = slice(h * hd, (h + 1) * hd)
            q = q_ref[rs, hs]
            k = k_ref[rs, hs]
            v = v_ref[rs, hs]
            i_col, b_col = gc[:, h:h + 1], gc[:, hm + h:hm + h + 1]
            i_row, b_row = gr[h:h + 1, :], gr[hm + h:hm + h + 1, :]
            m_prev = m_ref[h:h + 1, 0:1]
            n_prev = n_ref[h:h + 1, :]
            c_prev = c_ref[h]

            dm = jnp.where(causal, b_col - b_row + i_row, NEG)
            inter = b_col + m_prev
            m_loc = jnp.maximum(inter, jnp.max(dm, axis=-1, keepdims=True))
            dexp = jnp.exp(dm - m_loc)
            inter_w = jnp.exp(inter - m_loc)
            sc = _dot_nt(q, k) * dexp
            num = inter_w * _dot(q, c_prev.astype(BF16)) + _dot(sc.astype(BF16), v)
            nq = (inter_w * jnp.sum(q.astype(F32) * n_prev, axis=-1, keepdims=True)
                  + jnp.sum(sc, axis=-1, keepdims=True))
            hval = num / jnp.maximum(jnp.abs(nq), jnp.exp(-m_loc))

            b_last = b_col[lc - 1:lc, :]
            g_col = b_last - b_col + i_col
            m_new = jnp.maximum(b_last + m_prev, jnp.max(g_col, axis=0, keepdims=True))
            decay = jnp.exp(b_last + m_prev - m_new)
            kw = k.astype(F32) * jnp.exp(g_col - m_new)
            c_ref[h] = decay * c_prev + _dot_tn(kw.astype(BF16), v)
            n_ref[h:h + 1, :] = decay * n_prev + jnp.sum(kw, axis=0, keepdims=True)
            m_ref[h:h + 1, :] = jnp.broadcast_to(m_new, (1, LANES))

            mu = jnp.mean(hval, axis=-1, keepdims=True)
            cen = hval - mu
            var = jnp.mean(cen * cen, axis=-1, keepdims=True)
            hn = cen * lax.rsqrt(var + LN_EPS) * g_ref[:, hs]
            y_ref[rs, hs] = (_sigmoid(z_ref[rs, hs].astype(F32)) * hn).astype(y_ref.dtype)


def _mlstm_scan(q, k, v, gc, gr, z, m_norm_g):
    s = q.shape[0]
    chunks = 2
    tm = chunks * MLSTM_CHUNK
    row_spec = pl.BlockSpec((tm, D_M), lambda i: (i, 0))
    return pl.pallas_call(
        functools.partial(_mlstm_scan_kernel, chunks=chunks),
        out_shape=jax.ShapeDtypeStruct((s, D_M), BF16),
        grid=(s // tm,),
        in_specs=[row_spec, row_spec, row_spec,
                  pl.BlockSpec((tm, 2 * N_HEADS_M), lambda i: (i, 0)),
                  pl.BlockSpec((2 * N_HEADS_M, tm), lambda i: (0, i)),
                  row_spec,
                  pl.BlockSpec((1, D_M), lambda i: (0, 0))],
        out_specs=row_spec,
        scratch_shapes=[pltpu.VMEM((N_HEADS_M, HEAD_DIM_M, HEAD_DIM_M), F32),
                        pltpu.VMEM((8, HEAD_DIM_M), F32),
                        pltpu.VMEM((8, LANES), F32)],
        compiler_params=_params(("arbitrary",), 32),
        name="mlstm_scan",
    )(q, k, v, gc, gr, z, m_norm_g.reshape(1, D_M))


def _merge_kernel(x_ref, o1_ref, o2_ref, o3_ref, l1_ref, l2_ref, l3_ref, ym_ref,
                  wg_ref, bg_ref, wa_ref, wm_ref, wo_ref, lng_ref, lnb_ref, out_ref,
                  wgb_ref, wab_ref, wmb_ref, wob_ref):
    @pl.when(pl.program_id(0) == 0)
    def _cast_weights():
        wgb_ref[...] = wg_ref[...].astype(BF16)
        wab_ref[...] = wa_ref[...].astype(BF16)
        wmb_ref[...] = wm_ref[...].astype(BF16)
        wob_ref[...] = wo_ref[...].astype(BF16)

    x = x_ref[...]
    xb = x.astype(BF16)
    stats = (l1_ref[...], l2_ref[...], l3_ref[...])
    mx = jnp.maximum(jnp.maximum(stats[0], stats[1]), stats[2])
    es = [jnp.exp(st - mx) for st in stats]
    ls = [pltpu.roll(st, LANES - N_HEADS_A, 1) for st in stats]
    den = es[0] * ls[0] + es[1] * ls[1] + es[2] * ls[2]
    head_lane = lax.broadcasted_iota(I32, den.shape, 1) < N_HEADS_A
    inv = jnp.where(head_lane, 1.0 / den, 0.0)
    hrow = lax.broadcasted_iota(I32, (LANES, D_A), 0)
    hcol = lax.broadcasted_iota(I32, (LANES, D_A), 1) // HEAD_DIM_A
    expand = (hrow == hcol).astype(BF16)
    ya = jnp.zeros((x.shape[0], D_A), F32)
    for e, o_ref in zip(es, (o1_ref, o2_ref, o3_ref)):
        w = sum(_dot(part, expand) for part in _split3(e * inv))
        ya = ya + w * o_ref[...].astype(F32)

    gate = _sigmoid(_dot(xb, wgb_ref[...]) + bg_ref[...])
    merged = (gate[:, :D_MODEL] * _dot(ya.astype(BF16), wab_ref[...])
              + gate[:, D_MODEL:] * _dot(ym_ref[...], wmb_ref[...]))
    y = _dot(merged.astype(BF16), wob_ref[...])
    out_ref[...] = _layer_norm(ALPHA * x + y, lng_ref[...], lnb_ref[...])


def _merge(x, outs, lses, ym, w_gate, b_gate, w_br_a, w_br_m, w_o, ln_g, ln_b, layer):
    s = x.shape[0]
    tm = 512
    res = lambda shape: pl.BlockSpec((None,) + shape, lambda i: (layer, 0, 0), pipeline_mode=pl.Buffered(1))
    small = lambda n: pl.BlockSpec((1, n), lambda i: (0, 0))
    rows = lambda n: pl.BlockSpec((tm, n), lambda i: (i, 0))
    return pl.pallas_call(
        _merge_kernel,
        out_shape=jax.ShapeDtypeStruct((s, D_MODEL), F32),
        grid=(s // tm,),
        in_specs=[rows(D_MODEL), rows(D_A), rows(D_A), rows(D_A), rows(LANES), rows(LANES), rows(LANES), rows(D_M),
                  res((D_MODEL, 2 * D_MODEL)), small(2 * D_MODEL), res((D_A, D_MODEL)), res((D_M, D_MODEL)),
                  res((D_MODEL, D_MODEL)), small(D_MODEL), small(D_MODEL)],
        out_specs=rows(D_MODEL),
        scratch_shapes=[pltpu.VMEM((D_MODEL, 2 * D_MODEL), BF16), pltpu.VMEM((D_A, D_MODEL), BF16),
                        pltpu.VMEM((D_M, D_MODEL), BF16), pltpu.VMEM((D_MODEL, D_MODEL), BF16)],
        compiler_params=_params(("arbitrary",), 56),
        name="merge",
    )(x, *outs, *lses, ym, w_gate, b_gate.reshape(1, -1), w_br_a, w_br_m, w_o,
      ln_g.reshape(1, -1), ln_b.reshape(1, -1))


def _token_mixer_layer(x, biases, w_in, w_gate, b_gate, conv_w, conv_b, w_qk_m, w_v_m, w_if, b_if, m_norm_g,
                       w_br_a, w_br_m, w_o, ln_g, ln_b, layer):
    (q, k, v, xm, zm), perm = _in_proj(x, w_in, layer)
    outs, stats = [], []
    for (_, dilation), bias in zip(ATTN_PATTERNS, biases):
        qd, kd, vd = (q, k, v) if dilation == 1 else perm[DILATIONS.index(dilation)]
        o, st = _attn_pattern(qd, kd, vd, bias, dilation)
        outs.append(o)
        stats.append(st)
    qm, km, vm, gc, gr = _mlstm_prep(xm, conv_w, conv_b, w_qk_m, w_v_m, w_if, b_if, layer)
    ym = _mlstm_scan(qm, km, vm, gc, gr, zm, m_norm_g)
    return _merge(x, outs, stats, ym, w_gate, b_gate, w_br_a, w_br_m, w_o, ln_g, ln_b, layer)


def _ffn_kernel(x_ref, w1_ref, w3_ref, w2_ref, lng_ref, lnb_ref, out_ref, xb_ref, acc_ref):
    f = pl.program_id(1)

    @pl.when(f == 0)
    def _start():
        xb_ref[...] = x_ref[...].astype(BF16)
        acc_ref[...] = jnp.zeros_like(acc_ref)

    xb = xb_ref[...]
    a = _dot(xb, w1_ref[...].astype(BF16))
    g = _dot(xb, w3_ref[...].astype(BF16))
    hidden = (a * _sigmoid(a) * g).astype(BF16)
    acc_ref[...] += _dot(hidden, w2_ref[...].astype(BF16))

    @pl.when(f == pl.num_programs(1) - 1)
    def _finish():
        out_ref[...] = _layer_norm(ALPHA * x_ref[...] + acc_ref[...], lng_ref[...], lnb_ref[...])


def _dense_ffn(x, w13, w2, ln_g, ln_b, j):
    s = x.shape[0]
    tm = min(FFN_ROW_TILE, s)
    fc = FFN_FF_CHUNK
    nf = D_FF // fc
    small = pl.BlockSpec((1, D_MODEL), lambda i, f: (0, 0))
    return pl.pallas_call(
        _ffn_kernel,
        out_shape=jax.ShapeDtypeStruct((s, D_MODEL), F32),
        grid=(s // tm, nf),
        in_specs=[pl.BlockSpec((tm, D_MODEL), lambda i, f: (i, 0)),
                  pl.BlockSpec((None, D_MODEL, fc), lambda i, f: (j, 0, f)),
                  pl.BlockSpec((None, D_MODEL, fc), lambda i, f: (j, 0, nf + f)),
                  pl.BlockSpec((None, fc, D_MODEL), lambda i, f: (j, f, 0)),
                  small, small],
        out_specs=pl.BlockSpec((tm, D_MODEL), lambda i, f: (i, 0)),
        scratch_shapes=[pltpu.VMEM((tm, D_MODEL), BF16), pltpu.VMEM((tm, D_MODEL), F32)],
        compiler_params=_params(("arbitrary", "arbitrary"), 48),
        name="dense_ffn",
    )(x, w13, w13, w2, ln_g.reshape(1, -1), ln_b.reshape(1, -1))


def _router_kernel(x_ref, rw_ref, rb_ref, gate_ref, rank_ref, cnt_ref, carry_ref):
    tm = x_ref.shape[0]
    ne = N_EXPERTS

    @pl.when(pl.program_id(0) == 0)
    def _init():
        carry_ref[...] = jnp.zeros_like(carry_ref)

    xs = _split3(x_ref[...])
    ws = _split3(rw_ref[...])
    logits = rb_ref[...] + sum(_dot(xs[a], ws[b]) for a, b in ((2, 0), (0, 2), (1, 1), (1, 0), (0, 1), (0, 0)))
    lane = lax.broadcasted_iota(I32, (tm, ne), 1)
    v1 = jnp.max(logits, axis=-1, keepdims=True)
    i1 = jnp.min(jnp.where(logits == v1, lane, ne), axis=-1, keepdims=True)
    rest = jnp.where(lane == i1, -jnp.inf, logits)
    v2 = jnp.max(rest, axis=-1, keepdims=True)
    i2 = jnp.min(jnp.where(rest == v2, lane, ne), axis=-1, keepdims=True)
    e2 = jnp.exp(v2 - v1)
    den = 1.0 + e2
    sel1, sel2 = lane == i1, lane == i2
    gate_ref[...] = jnp.where(sel1, 1.0 / den, 0.0) + jnp.where(sel2, e2 / den, 0.0)
    sel = jnp.where(sel1 | sel2, 1.0, 0.0)
    ri = lax.broadcasted_iota(I32, (tm, tm), 0)
    ci = lax.broadcasted_iota(I32, (tm, tm), 1)
    before = (ri > ci).astype(BF16)
    carry = carry_ref[0:1, 0:ne]
    rank = _dot(before, sel.astype(BF16)) + carry
    rank_ref[...] = jnp.where(sel > 0.0, rank, -1.0)
    total = carry + jnp.sum(sel, axis=0, keepdims=True)
    carry_ref[0:1, 0:ne] = total
    cnt_ref[...] = total


def _router(x, router_w, router_b):
    s = x.shape[0]
    tm = 512
    ne = N_EXPERTS
    return pl.pallas_call(
        _router_kernel,
        out_shape=(jax.ShapeDtypeStruct((s, ne), F32), jax.ShapeDtypeStruct((s, ne), F32),
                   jax.ShapeDtypeStruct((1, ne), F32)),
        grid=(s // tm,),
        in_specs=[pl.BlockSpec((tm, D_MODEL), lambda i: (i, 0)),
                  pl.BlockSpec((D_MODEL, ne), lambda i: (0, 0)),
                  pl.BlockSpec((1, ne), lambda i: (0, 0))],
        out_specs=(pl.BlockSpec((tm, ne), lambda i: (i, 0)), pl.BlockSpec((tm, ne), lambda i: (i, 0)),
                   pl.BlockSpec((1, ne), lambda i: (0, 0))),
        scratch_shapes=[pltpu.VMEM((8, LANES), F32)],
        compiler_params=_params(("arbitrary",), 32),
        name="moe_router",
    )(x, router_w, router_b.reshape(1, ne))


def _dispatch_kernel(tile_ref, run_ref, x_ref, pos_ref, gate_ref, xs_ref, gs_ref, acc_ref, gacc_ref, *, n_items):
    rt = MOE_ROW_TILE
    e = pl.program_id(0)
    idx = (e * pl.num_programs(1) + pl.program_id(1)) * pl.num_programs(2) + pl.program_id(2)
    tile = tile_ref[idx]
    first = jnp.logical_or(idx == 0, tile_ref[jnp.maximum(idx - 1, 0)] != tile)
    last = jnp.logical_or(idx == n_items - 1, tile_ref[jnp.minimum(idx + 1, n_items - 1)] != tile)

    @pl.when(first)
    def _zero():
        acc_ref[...] = jnp.zeros_like(acc_ref)
        gacc_ref[...] = jnp.zeros_like(gacc_ref)

    @pl.when(run_ref[idx] == 1)
    def _accumulate():
        pos = pos_ref[pl.ds(e, 1), :]
        row = (tile * rt).astype(F32) + lax.broadcasted_iota(I32, (rt, rt), 0).astype(F32)
        hit = row == pos
        acc_ref[...] += _dot(jnp.where(hit, 1.0, 0.0).astype(BF16), x_ref[...].astype(BF16))
        gacc_ref[...] += jnp.sum(jnp.where(hit, gate_ref[pl.ds(e, 1), :], 0.0), axis=-1, keepdims=True)

    @pl.when(last)
    def _write():
        xs_ref[...] = acc_ref[...].astype(xs_ref.dtype)
        gs_ref[...] = gacc_ref[...]


def _combine_kernel(tile_ref, run_ref, x_ref, y_ref, pos_ref, lng_ref, lnb_ref, out_ref, acc_ref):
    rt = MOE_ROW_TILE
    e, kk = pl.program_id(1), pl.program_id(2)
    idx = (pl.program_id(0) * pl.num_programs(1) + e) * pl.num_programs(2) + kk

    @pl.when(jnp.logical_and(e == 0, kk == 0))
    def _zero():
        acc_ref[...] = jnp.zeros_like(acc_ref)

    @pl.when(run_ref[idx] == 1)
    def _accumulate():
        lane = lax.broadcasted_iota(I32, pos_ref.shape, 1)
        pos = jnp.max(jnp.where(lane == e, pos_ref[...], -1.0), axis=-1, keepdims=True)
        col = (tile_ref[idx] * rt).astype(F32) + lax.broadcasted_iota(I32, (rt, rt), 1).astype(F32)
        acc_ref[...] += _dot(jnp.where(pos == col, 1.0, 0.0).astype(BF16), y_ref[...])

    @pl.when(jnp.logical_and(e == pl.num_programs(1) - 1, kk == pl.num_programs(2) - 1))
    def _finish():
        out_ref[...] = _layer_norm(ALPHA * x_ref[...] + acc_ref[...], lng_ref[...], lnb_ref[...])


def _moe_gemm_kernel(te_ref, nu_ref, xs_ref, gs_ref, w1_ref, w3_ref, w2_ref, y_ref, acc_ref):
    f = pl.program_id(1)

    @pl.when(pl.program_id(0) < nu_ref[0])
    def _tile():
        @pl.when(f == 0)
        def _zero():
            acc_ref[...] = jnp.zeros_like(acc_ref)

        xb = xs_ref[...]
        a = _dot(xb, w1_ref[0].astype(BF16))
        g = _dot(xb, w3_ref[0].astype(BF16))
        hidden = (a * _sigmoid(a) * g).astype(BF16)
        acc_ref[...] += _dot(hidden, w2_ref[0].astype(BF16))

        @pl.when(f == pl.num_programs(1) - 1)
        def _finish():
            y_ref[...] = (acc_ref[...] * gs_ref[...]).astype(y_ref.dtype)


def _moe_layer(x, router_w, router_b, w13, w2, ln_g, ln_b):
    s = x.shape[0]
    ne, rt, gt = N_EXPERTS, MOE_ROW_TILE, MOE_GROUP_TILE
    sub = gt // rt
    nj = s // rt
    n_group_tiles = (TOP_K * s) // gt + ne
    n_row_tiles = n_group_tiles * sub

    gate, rank, counts = _router(x, router_w, router_b)

    cnt = counts[0].astype(I32)
    padded = ((cnt + gt - 1) // gt) * gt
    off = jnp.cumsum(padded) - padded
    n_used = (jnp.sum(padded) // gt).astype(I32)
    tile_expert = jnp.clip(
        jnp.searchsorted(jnp.cumsum(padded), jnp.arange(n_group_tiles, dtype=I32) * gt, side="right"), 0, ne - 1
    ).astype(I32)
    pos = jnp.where(rank >= 0.0, rank + off.astype(F32)[None, :], -1.0)
    pos_t = pos.T
    gate_t = gate.T

    sel_cnt = jnp.cumsum(jnp.sum((rank >= 0.0).reshape(nj, rt, ne), axis=1), axis=0).astype(I32)
    cum = jnp.concatenate([jnp.zeros((1, ne), I32), sel_cnt], axis=0).T
    lo = off[:, None] + cum[:, :-1]
    hi = off[:, None] + cum[:, 1:]
    nonempty = hi > lo
    last_tile = jnp.maximum((off + padded) // rt - 1, 0)
    data_tile = jnp.where(cnt > 0, (off + cnt - 1) // rt, last_tile)
    t0 = jnp.minimum(lo // rt, last_tile[:, None])
    straddle = nonempty & ((hi - 1) // rt > lo // rt)
    t1 = jnp.where(straddle, t0 + 1, t0)

    d_tiles = jnp.stack([t0, t1], axis=-1)
    d_run = jnp.stack([nonempty, straddle], axis=-1)
    n_fill = sub - 1
    fill = jnp.minimum(data_tile[:, None] + 1 + jnp.arange(n_fill, dtype=I32)[None, :], last_tile[:, None])
    fill = jnp.repeat(fill[:, :, None], 2, axis=-1)
    d_tiles = jnp.concatenate([d_tiles, fill], axis=1).reshape(-1).astype(I32)
    d_run = jnp.concatenate([d_run, jnp.zeros((ne, n_fill, 2), bool)], axis=1).reshape(-1).astype(I32)
    nj_d = nj + n_fill
    n_items = ne * nj_d * 2

    xs, gs = pl.pallas_call(
        functools.partial(_dispatch_kernel, n_items=n_items),
        out_shape=(jax.ShapeDtypeStruct((n_row_tiles * rt, D_MODEL), BF16),
                   jax.ShapeDtypeStruct((n_row_tiles * rt, 1), F32)),
        grid_spec=pltpu.PrefetchScalarGridSpec(
            num_scalar_prefetch=2,
            grid=(ne, nj_d, 2),
            in_specs=[pl.BlockSpec((rt, D_MODEL), lambda e, j, k, t, r: (jnp.minimum(j, nj - 1), 0)),
                      pl.BlockSpec((ne, rt), lambda e, j, k, t, r: (0, jnp.minimum(j, nj - 1))),
                      pl.BlockSpec((ne, rt), lambda e, j, k, t, r: (0, jnp.minimum(j, nj - 1)))],
            out_specs=(pl.BlockSpec((rt, D_MODEL), lambda e, j, k, t, r: (t[(e * nj_d + j) * 2 + k], 0)),
                       pl.BlockSpec((rt, 1), lambda e, j, k, t, r: (t[(e * nj_d + j) * 2 + k], 0))),
            scratch_shapes=[pltpu.VMEM((rt, D_MODEL), F32), pltpu.VMEM((rt, 1), F32)]),
        compiler_params=_params(("arbitrary", "arbitrary", "arbitrary"), 32),
        name="moe_dispatch",
    )(d_tiles, d_run, x, pos_t, gate_t)

    fc = MOE_FF_CHUNK
    nf = D_FF_E // fc
    tile_of = lambda i, nu: jnp.minimum(i, nu[0] - 1)
    chunk_of = lambda i, f, nu: jnp.where(i < nu[0], f, nf - 1)
    ys = pl.pallas_call(
        _moe_gemm_kernel,
        out_shape=jax.ShapeDtypeStruct((n_row_tiles * rt, D_MODEL), BF16),
        grid_spec=pltpu.PrefetchScalarGridSpec(
            num_scalar_prefetch=2,
            grid=(n_group_tiles, nf),
            in_specs=[pl.BlockSpec((gt, D_MODEL), lambda i, f, te, nu: (tile_of(i, nu), 0)),
                      pl.BlockSpec((gt, 1), lambda i, f, te, nu: (tile_of(i, nu), 0)),
                      pl.BlockSpec((1, D_MODEL, fc), lambda i, f, te, nu: (te[tile_of(i, nu)], 0, chunk_of(i, f, nu))),
                      pl.BlockSpec((1, D_MODEL, fc),
                                   lambda i, f, te, nu: (te[tile_of(i, nu)], 0, nf + chunk_of(i, f, nu))),
                      pl.BlockSpec((1, fc, D_MODEL), lambda i, f, te, nu: (te[tile_of(i, nu)], chunk_of(i, f, nu), 0))],
            out_specs=pl.BlockSpec((gt, D_MODEL), lambda i, f, te, nu: (tile_of(i, nu), 0)),
            scratch_shapes=[pltpu.VMEM((gt, D_MODEL), F32)]),
        compiler_params=_params(("arbitrary", "arbitrary"), 48),
        name="moe_gemm",
    )(tile_expert, n_used.reshape(1), xs, gs, w13, w13, w2)

    c0 = jnp.minimum(lo // rt, last_tile[:, None])
    c_tiles = jnp.stack([c0, jnp.where(straddle, c0 + 1, c0)], axis=-1).transpose(1, 0, 2).reshape(-1).astype(I32)
    c_run = jnp.stack([nonempty, straddle], axis=-1).transpose(1, 0, 2).reshape(-1).astype(I32)
    small = pl.BlockSpec((1, D_MODEL), lambda j, e, k, t, r: (0, 0))
    return pl.pallas_call(
        _combine_kernel,
        out_shape=jax.ShapeDtypeStruct((s, D_MODEL), F32),
        grid_spec=pltpu.PrefetchScalarGridSpec(
            num_scalar_prefetch=2,
            grid=(nj, ne, 2),
            in_specs=[pl.BlockSpec((rt, D_MODEL), lambda j, e, k, t, r: (j, 0)),
                      pl.BlockSpec((rt, D_MODEL), lambda j, e, k, t, r: (t[(j * ne + e) * 2 + k], 0)),
                      pl.BlockSpec((rt, ne), lambda j, e, k, t, r: (j, 0)),
                      small, small],
            out_specs=pl.BlockSpec((rt, D_MODEL), lambda j, e, k, t, r: (j, 0)),
            scratch_shapes=[pltpu.VMEM((rt, D_MODEL), F32)]),
        compiler_params=_params(("arbitrary", "arbitrary", "arbitrary"), 32),
        name="moe_combine",
    )(c_tiles, c_run, x, ys, pos, ln_g.reshape(1, -1), ln_b.reshape(1, -1))


def kernel(x, rel_bias, w_in, w_gate, b_gate, conv_w, conv_b, w_qk_m, w_v_m, w_if, b_if, m_norm_g, w_br_a, w_br_m,
           w_o, ln_g, ln_b, ffn_w13, ffn_w2, router_w, router_b, exp_w13, exp_w2):
    batch, seq, _ = x.shape
    assert batch == 1
    h = x.reshape(seq, D_MODEL)
    biases = [_attn_bias(rel_bias, window, dilation) for window, dilation in ATTN_PATTERNS]
    for l in range(DEPTH):
        h = _token_mixer_layer(h, biases, w_in, w_gate, b_gate[l], conv_w[l], conv_b[l], w_qk_m, w_v_m, w_if[l],
                               b_if[l], m_norm_g[l], w_br_a, w_br_m, w_o, ln_g[l, 0], ln_b[l, 0], l)
        j = l // 2
        if l % 2 == 0:
            h = _dense_ffn(h, ffn_w13, ffn_w2, ln_g[l, 1], ln_b[l, 1], j)
        else:
            h = _moe_layer(h, router_w[j], router_b[j], exp_w13[j], exp_w2[j], ln_g[l, 1], ln_b[l, 1])
    return h.reshape(batch, seq, D_MODEL)
```

```python
import functools
import math

import jax
import jax.numpy as jnp
from jax import lax
from jax.experimental import pallas as pl
from jax.experimental.pallas import tpu as pltpu

F32 = jnp.float32
BF16 = jnp.bfloat16
I32 = jnp.int32

D_MODEL = 1024
DEPTH = 2
N_HEADS_A = 8
HEAD_DIM_A = 64
D_A = N_HEADS_A * HEAD_DIM_A
ATTN_PATTERNS = ((128, 1), (512, 4), (2048, 16))
ATTN_BLOCK = 128
NUM_BUCKETS = 32
MAX_DISTANCE = 2048
N_HEADS_M = 4
HEAD_DIM_M = 128
D_M = N_HEADS_M * HEAD_DIM_M
CONV_K = 4
MLSTM_CHUNK = 128
N_PROJ = 5
P_IN = 3 * D_A + 2 * D_M
D_FF = 2816
N_EXPERTS = 8
TOP_K = 2
D_FF_E = 3584
ALPHA = (2.0 * DEPTH) ** 0.25
LN_EPS = 1e-5

NEG = -1e30
LANES = 128
HALO_ROWS = 16
MIB = 1024 * 1024

MOE_GROUP_TILE = 1024
MOE_FF_CHUNK = 512
MOE_GATHER_ROWS = 256
MOE_GATHER_TOKENS = 256
MOE_COMBINE_TOKENS = 512
MOE_COMBINE_ROWS = 256
FFN_ROW_TILE = 1024
FFN_FF_CHUNK = 256


def _params(sem, vmem_mib):
    return pltpu.CompilerParams(dimension_semantics=sem, vmem_limit_bytes=vmem_mib * MIB)


def _sigmoid(x):
    return 1.0 / (1.0 + jnp.exp(-x))


def _layer_norm(r, g, b):
    mu = jnp.mean(r, axis=-1, keepdims=True)
    c = r - mu
    var = jnp.mean(c * c, axis=-1, keepdims=True)
    return c * lax.rsqrt(var + LN_EPS) * g + b


def _split3(a):
    hi = a.astype(BF16)
    r1 = a - hi.astype(F32)
    mid = r1.astype(BF16)
    lo = (r1 - mid.astype(F32)).astype(BF16)
    return hi, mid, lo


def _dot(a, b):
    return jnp.dot(a, b, preferred_element_type=F32)


def _dot_nt(a, b):
    return lax.dot_general(a, b, (((1,), (1,)), ((), ())), preferred_element_type=F32)


def _dot_tn(a, b):
    return lax.dot_general(a, b, (((0,), (0,)), ((), ())), preferred_element_type=F32)


IN_PROJ_ROWS = 1024
DILATIONS = tuple(d for _, d in ATTN_PATTERNS if d > 1)


def _in_proj_kernel(x_ref, w_ref, *refs):
    nat = refs[:N_PROJ]
    perm = refs[N_PROJ:N_PROJ + 3 * len(DILATIONS)]
    wb_ref, y_ref = refs[-2:]

    @pl.when(pl.program_id(0) == 0)
    def _cast_weights():
        wb_ref[...] = w_ref[...].astype(BF16)

    xb = x_ref[...].astype(BF16)
    for j in range(N_PROJ):
        y = _dot(xb, wb_ref[:, j * D_A:(j + 1) * D_A])
        if j == 0:
            y = y * (HEAD_DIM_A ** -0.5)
        nat[j][...] = y.astype(BF16)
        if j >= 3:
            continue
        for c in range(D_A // LANES):
            y_ref[c] = y[:, c * LANES:(c + 1) * LANES]
        for di, d in enumerate(DILATIONS):
            out = perm[di * 3 + j]
            tiles, _, rpc, _ = out.shape
            for t in range(tiles):
                for r in range(d):
                    for c in range(D_A // LANES):
                        out[t, r, :, c * LANES:(c + 1) * LANES] = (
                            y_ref[c, pl.ds(t * d * rpc + r, rpc, stride=d), :].astype(BF16))


def _in_proj(x, w_in, layer):
    s = x.shape[0]
    tm = IN_PROJ_ROWS
    blk = ATTN_BLOCK
    out_shape = [jax.ShapeDtypeStruct((s, D_A), BF16)] * N_PROJ
    out_specs = [pl.BlockSpec((tm, D_A), lambda i: (i, 0))] * N_PROJ
    for d in DILATIONS:
        tile = d * blk
        if tile <= tm:
            spec = pl.BlockSpec((tm // tile, d, blk, D_A), lambda i: (i, 0, 0, 0))
        else:
            parts = tile // tm
            spec = pl.BlockSpec((1, d, blk // parts, D_A), lambda i, parts=parts: (i // parts, 0, i % parts, 0))
        out_shape += [jax.ShapeDtypeStruct((s // tile, d, blk, D_A), BF16)] * 3
        out_specs += [spec] * 3
    outs = pl.pallas_call(
        _in_proj_kernel,
        out_shape=tuple(out_shape),
        grid=(s // tm,),
        in_specs=[
            pl.BlockSpec((tm, D_MODEL), lambda i: (i, 0)),
            pl.BlockSpec((None, D_MODEL, P_IN), lambda i: (layer, 0, 0), pipeline_mode=pl.Buffered(1)),
        ],
        out_specs=tuple(out_specs),
        scratch_shapes=[pltpu.VMEM((D_MODEL, P_IN), BF16), pltpu.VMEM((D_A // LANES, tm, LANES), F32)],
        compiler_params=_params(("arbitrary",), 56),
        name="in_proj",
    )(x, w_in)
    nat = outs[:N_PROJ]
    perm = [tuple(t.reshape(s, D_A) for t in outs[N_PROJ + 3 * i:N_PROJ + 3 * i + 3]) for i in range(len(DILATIONS))]
    return nat, perm


ATTN_STEP_BLOCKS = 16


def _rel_bucket(dist):
    exact = NUM_BUCKETS // 2
    d = jnp.maximum(dist, exact).astype(F32)
    log_b = exact + (jnp.log(d / exact) / math.log(MAX_DISTANCE / exact) * (NUM_BUCKETS - exact)).astype(I32)
    return jnp.where(dist < exact, dist, jnp.minimum(log_b, NUM_BUCKETS - 1))


def _attn_bias(rel_bias, window, dilation):
    blk = ATTN_BLOCK
    qi = jnp.arange(blk)[:, None]
    kj = jnp.arange(2 * blk)[None, :]
    rel = qi + blk - kj
    bucket = _rel_bucket(jnp.maximum(rel, 0) * dilation)
    onehot = (bucket[..., None] == jnp.arange(NUM_BUCKETS)).astype(F32)
    bias = jnp.einsum("qkb,bh->hqk", onehot, rel_bias.astype(F32), precision=lax.Precision.HIGHEST)
    mask = (rel >= 0) & (rel <= window // dilation)
    return jnp.where(mask[None], bias, NEG)


def _attn_kernel(q_ref, kp_ref, kc_ref, vp_ref, vc_ref, bias_ref, o_ref, st_ref, kf_ref, vf_ref, on_ref, sn_ref,
                 *, dilation):
    blk = ATTN_BLOCK
    nb = ATTN_STEP_BLOCKS
    rows = nb * blk
    step = pl.program_id(0)
    kf_ref[0:rows, :] = kp_ref[...]
    kf_ref[rows:, :] = kc_ref[...]
    vf_ref[0:rows, :] = vp_ref[...]
    vf_ref[rows:, :] = vc_ref[...]
    kj = lax.broadcasted_iota(I32, (blk, 2 * blk), 1)
    lane = lax.broadcasted_iota(I32, (blk, LANES), 1)
    lane_k = lax.broadcasted_iota(I32, (2 * blk, LANES), 1)
    ones = jnp.ones((2 * blk, LANES), BF16)

    def block(g, carry):
        cur = pl.multiple_of(g * blk, blk)
        prev = pl.multiple_of((nb + g - dilation) * blk, blk)
        pen = jnp.where(jnp.logical_and(step == 0, g < dilation), NEG, 0.0).astype(F32)
        prev_pen = jnp.where(kj < blk, pen, 0.0)
        stats = jnp.zeros((blk, LANES), F32)
        tile_i, cls = g // dilation, g % dilation
        dst = pl.ds(tile_i * (blk * dilation) + cls, blk, stride=dilation)
        for hp in range(N_HEADS_A // 2):
            ls = slice(hp * LANES, (hp + 1) * LANES)
            qb = q_ref[pl.ds(cur, blk), ls]
            kb = jnp.concatenate([kf_ref[pl.ds(prev, blk), ls], kf_ref[pl.ds(rows + cur, blk), ls]], axis=0)
            vb = jnp.concatenate([vf_ref[pl.ds(prev, blk), ls], vf_ref[pl.ds(rows + cur, blk), ls]], axis=0)
            acc = jnp.zeros((blk, LANES), F32)
            for par in range(2):
                h = 2 * hp + par
                own_q = (lane // HEAD_DIM_A) == par
                own_k = (lane_k // HEAD_DIM_A) == par
                qm = jnp.where(own_q, qb, jnp.zeros_like(qb))
                logits = _dot_nt(qm, kb) + bias_ref[h] + prev_pen
                m = jnp.max(logits, axis=-1, keepdims=True)
                p = jnp.exp(logits - m).astype(BF16)
                vm = jnp.where(own_k, vb, jnp.zeros_like(vb))
                ol = _dot(p, jnp.concatenate([vm, ones], axis=1))
                acc = acc + ol[:, :LANES]
                stats = jnp.where(lane == h, m, stats)
                stats = jnp.where(lane == N_HEADS_A + h, ol[:, LANES:], stats)
            on_ref[hp, dst, :] = acc
        sn_ref[dst, :] = stats
        return carry

    lax.fori_loop(0, nb, block, 0)
    for hp in range(N_HEADS_A // 2):
        o_ref[:, hp * LANES:(hp + 1) * LANES] = on_ref[hp].astype(o_ref.dtype)
    st_ref[...] = sn_ref[...]


def _attn_pattern(q, k, v, bias, dilation):
    s = q.shape[0]
    rows = ATTN_STEP_BLOCKS * ATTN_BLOCK
    cur = pl.BlockSpec((rows, D_A), lambda i: (i, 0))
    prev = pl.BlockSpec((rows, D_A), lambda i: (jnp.maximum(i - 1, 0), 0))
    return pl.pallas_call(
        functools.partial(_attn_kernel, dilation=dilation),
        out_shape=(jax.ShapeDtypeStruct((s, D_A), BF16), jax.ShapeDtypeStruct((s, LANES), F32)),
        grid=(s // rows,),
        in_specs=[cur, prev, cur, prev, cur,
                  pl.BlockSpec((N_HEADS_A, ATTN_BLOCK, 2 * ATTN_BLOCK), lambda i: (0, 0, 0))],
        out_specs=(pl.BlockSpec((rows, D_A), lambda i: (i, 0)), pl.BlockSpec((rows, LANES), lambda i: (i, 0))),
        scratch_shapes=[pltpu.VMEM((2 * rows, D_A), BF16), pltpu.VMEM((2 * rows, D_A), BF16),
                        pltpu.VMEM((D_A // LANES, rows, LANES), F32), pltpu.VMEM((rows, LANES), F32)],
        compiler_params=_params(("arbitrary",), 48),
        name=f"attn_d{dilation}",
    )(q, k, k, v, v, bias)


def _log_sigmoid(x):
    return jnp.minimum(x, 0.0) - jnp.log(1.0 + jnp.exp(-jnp.abs(x)))


def _mlstm_prep_kernel(xm_ref, halo_ref, cw_ref, cb_ref, wqk_ref, wv_ref, wif_ref, wift_ref, bif_ref, bift_ref,
                       q_ref, k_ref, v_ref, gc_ref, gr_ref):
    tm = xm_ref.shape[0]
    hm, hd, lc = N_HEADS_M, HEAD_DIM_M, MLSTM_CHUNK
    xmb = xm_ref[...]
    halo = jnp.where(pl.program_id(0) == 0, 0.0, halo_ref[...].astype(F32))
    xx = jnp.concatenate([halo, xmb.astype(F32)], axis=0)
    conv = jnp.zeros((tm, D_M), F32) + cb_ref[...]
    for j in range(CONV_K):
        start = HALO_ROWS - (CONV_K - 1) + j
        conv = conv + cw_ref[j:j + 1, :] * xx[start:start + tm, :]
    xcb = (conv * _sigmoid(conv)).astype(BF16)

    qs, ks, vs = [], [], []
    for h in range(hm):
        sl = slice(h * hd, (h + 1) * hd)
        qs.append(_dot(xcb[:, sl], wqk_ref[0, h].astype(BF16)))
        ks.append(_dot(xcb[:, sl], wqk_ref[1, h].astype(BF16)))
        vs.append(_dot(xmb[:, sl], wv_ref[h].astype(BF16)))
    q = jnp.concatenate(qs, axis=1)
    k = jnp.concatenate(ks, axis=1)
    v = jnp.concatenate(vs, axis=1)
    q_ref[...] = q.astype(BF16)
    k_ref[...] = (k * (hd ** -0.5)).astype(BF16)
    v_ref[...] = v.astype(BF16)

    qkv = jnp.concatenate([q, k, v], axis=1).astype(BF16)
    gates_c = _dot(qkv, wif_ref[...].astype(BF16)) + bif_ref[...]
    gates_r = _dot_nt(wift_ref[...].astype(BF16), qkv) + bift_ref[...]
    lane = lax.broadcasted_iota(I32, gates_c.shape, 1)
    row = lax.broadcasted_iota(I32, gates_r.shape, 0)
    gc_ref[...] = jnp.where(lane < hm, gates_c, _log_sigmoid(gates_c))
    gr_ref[...] = jnp.where(row < hm, gates_r, _log_sigmoid(gates_r))

    ri = lax.broadcasted_iota(I32, (lc, lc), 0)
    ci = lax.broadcasted_iota(I32, (lc, lc), 1)
    lower = (ri >= ci).astype(BF16)
    upper = (ri <= ci).astype(BF16)
    lane_c = lax.broadcasted_iota(I32, (lc, 2 * hm), 1)
    row_c = lax.broadcasted_iota(I32, (2 * hm, lc), 0)
    for c in range(tm // lc):
        rs = slice(c * lc, (c + 1) * lc)
        gcc = gc_ref[rs, :]
        grc = gr_ref[:, rs]
        cum_c = sum(_dot(lower, part) for part in _split3(gcc))
        cum_r = sum(_dot(part, upper) for part in _split3(grc))
        gc_ref[rs, :] = jnp.where(lane_c < hm, gcc, cum_c)
        gr_ref[:, rs] = jnp.where(row_c < hm, grc, cum_r)


def _mlstm_prep(xm, conv_w, conv_b, w_qk_m, w_v_m, w_if, b_if, layer):
    s = xm.shape[0]
    tm = 512
    hpt = tm // HALO_ROWS
    full = lambda shape: pl.BlockSpec(shape, lambda i: (0,) * len(shape))
    stacked = lambda shape: pl.BlockSpec((None,) + shape, lambda i: (layer,) + (0,) * len(shape))
    row_spec = pl.BlockSpec((tm, D_M), lambda i: (i, 0))
    return pl.pallas_call(
        _mlstm_prep_kernel,
        out_shape=(jax.ShapeDtypeStruct((s, D_M), BF16),) * 3
        + (jax.ShapeDtypeStruct((s, 2 * N_HEADS_M), F32), jax.ShapeDtypeStruct((2 * N_HEADS_M, s), F32)),
        grid=(s // tm,),
        in_specs=[
            row_spec,
            pl.BlockSpec((HALO_ROWS, D_M), lambda i: (jnp.maximum(i * hpt - 1, 0), 0)),
            full((CONV_K, D_M)), full((1, D_M)),
            stacked((2, N_HEADS_M, HEAD_DIM_M, HEAD_DIM_M)), stacked((N_HEADS_M, HEAD_DIM_M, HEAD_DIM_M)),
            full((3 * D_M, 2 * N_HEADS_M)), full((2 * N_HEADS_M, 3 * D_M)),
            full((1, 2 * N_HEADS_M)), full((2 * N_HEADS_M, 1)),
        ],
        out_specs=(row_spec, row_spec, row_spec,
                   pl.BlockSpec((tm, 2 * N_HEADS_M), lambda i: (i, 0)),
                   pl.BlockSpec((2 * N_HEADS_M, tm), lambda i: (0, i))),
        compiler_params=_params(("arbitrary",), 32),
        name="mlstm_prep",
    )(xm, xm, conv_w, conv_b.reshape(1, D_M), w_qk_m, w_v_m, w_if, w_if.T,
      b_if.reshape(1, -1), b_if.reshape(-1, 1))


def _mlstm_scan_kernel(q_ref, k_ref, v_ref, gc_ref, gr_ref, z_ref, g_ref, y_ref, c_ref, n_ref, m_ref, *, chunks):
    hm, hd, lc = N_HEADS_M, HEAD_DIM_M, MLSTM_CHUNK

    @pl.when(pl.program_id(0) == 0)
    def _init():
        c_ref[...] = jnp.zeros_like(c_ref)
        n_ref[...] = jnp.zeros_like(n_ref)
        m_ref[...] = jnp.zeros_like(m_ref)

    ri = lax.broadcasted_iota(I32, (lc, lc), 0)
    ci = lax.broadcasted_iota(I32, (lc, lc), 1)
    causal = ri >= ci
    for c in range(chunks):
        rs = slice(c * lc, (c + 1) * lc)
        gc = gc_ref[rs, :]
        gr = gr_ref[:, rs]
        for h in range(hm):
            hs = slice(h * hd, (h + 1) * hd)
            q = q_ref[rs, hs]
            k = k_ref[rs, hs]
            v = v_ref[rs, hs]
            i_col, b_col = gc[:, h:h + 1], gc[:, hm + h:hm + h + 1]
            i_row, b_row = gr[h:h + 1, :], gr[hm + h:hm + h + 1, :]
            m_prev = m_ref[h:h + 1, 0:1]
            n_prev = n_ref[h:h + 1, :]
            c_prev = c_ref[h]

            dm = jnp.where(causal, b_col - b_row + i_row, NEG)
            inter = b_col + m_prev
            m_loc = jnp.maximum(inter, jnp.max(dm, axis=-1, keepdims=True))
            dexp = jnp.exp(dm - m_loc)
            inter_w = jnp.exp(inter - m_loc)
            sc = _dot_nt(q, k) * dexp
            num = inter_w * _dot(q, c_prev.astype(BF16)) + _dot(sc.astype(BF16), v)
            nq = (inter_w * jnp.sum(q.astype(F32) * n_prev, axis=-1, keepdims=True)
                  + jnp.sum(sc, axis=-1, keepdims=True))
            hval = num / jnp.maximum(jnp.abs(nq), jnp.exp(-m_loc))

            b_last = b_col[lc - 1:lc, :]
            g_col = b_last - b_col + i_col
            m_new = jnp.maximum(b_last + m_prev, jnp.max(g_col, axis=0, keepdims=True))
            decay = jnp.exp(b_last + m_prev - m_new)
            kw = k.astype(F32) * jnp.exp(g_col - m_new)
            c_ref[h] = decay * c_prev + _dot_tn(kw.astype(BF16), v)
            n_ref[h:h + 1, :] = decay * n_prev + jnp.sum(kw, axis=0, keepdims=True)
            m_ref[h:h + 1, :] = jnp.broadcast_to(m_new, (1, LANES))

            mu = jnp.mean(hval, axis=-1, keepdims=True)
            cen = hval - mu
            var = jnp.mean(cen * cen, axis=-1, keepdims=True)
            hn = cen * lax.rsqrt(var + LN_EPS) * g_ref[:, hs]
            y_ref[rs, hs] = (_sigmoid(z_ref[rs, hs].astype(F32)) * hn).astype(y_ref.dtype)


def _mlstm_scan(q, k, v, gc, gr, z, m_norm_g):
    s = q.shape[0]
    chunks = 2
    tm = chunks * MLSTM_CHUNK
    row_spec = pl.BlockSpec((tm, D_M), lambda i: (i, 0))
    return pl.pallas_call(
        functools.partial(_mlstm_scan_kernel, chunks=chunks),
        out_shape=jax.ShapeDtypeStruct((s, D_M), BF16),
        grid=(s // tm,),
        in_specs=[row_spec, row_spec, row_spec,
                  pl.BlockSpec((tm, 2 * N_HEADS_M), lambda i: (i, 0)),
                  pl.BlockSpec((2 * N_HEADS_M, tm), lambda i: (0, i)),
                  row_spec,
                  pl.BlockSpec((1, D_M), lambda i: (0, 0))],
        out_specs=row_spec,
        scratch_shapes=[pltpu.VMEM((N_HEADS_M, HEAD_DIM_M, HEAD_DIM_M), F32),
                        pltpu.VMEM((8, HEAD_DIM_M), F32),
                        pltpu.VMEM((8, LANES), F32)],
        compiler_params=_params(("arbitrary",), 32),
        name="mlstm_scan",
    )(q, k, v, gc, gr, z, m_norm_g.reshape(1, D_M))


def _merge_kernel(x_ref, o1_ref, o2_ref, o3_ref, l1_ref, l2_ref, l3_ref, ym_ref,
                  wg_ref, bg_ref, wa_ref, wm_ref, wo_ref, lng_ref, lnb_ref, out_ref,
                  wgb_ref, wab_ref, wmb_ref, wob_ref):
    @pl.when(pl.program_id(0) == 0)
    def _cast_weights():
        wgb_ref[...] = wg_ref[...].astype(BF16)
        wab_ref[...] = wa_ref[...].astype(BF16)
        wmb_ref[...] = wm_ref[...].astype(BF16)
        wob_ref[...] = wo_ref[...].astype(BF16)

    x = x_ref[...]
    xb = x.astype(BF16)
    stats = (l1_ref[...], l2_ref[...], l3_ref[...])
    mx = jnp.maximum(jnp.maximum(stats[0], stats[1]), stats[2])
    es = [jnp.exp(st - mx) for st in stats]
    ls = [pltpu.roll(st, LANES - N_HEADS_A, 1) for st in stats]
    den = es[0] * ls[0] + es[1] * ls[1] + es[2] * ls[2]
    head_lane = lax.broadcasted_iota(I32, den.shape, 1) < N_HEADS_A
    inv = jnp.where(head_lane, 1.0 / den, 0.0)
    hrow = lax.broadcasted_iota(I32, (LANES, D_A), 0)
    hcol = lax.broadcasted_iota(I32, (LANES, D_A), 1) // HEAD_DIM_A
    expand = (hrow == hcol).astype(BF16)
    ya = jnp.zeros((x.shape[0], D_A), F32)
    for e, o_ref in zip(es, (o1_ref, o2_ref, o3_ref)):
        w = sum(_dot(part, expand) for part in _split3(e * inv))
        ya = ya + w * o_ref[...].astype(F32)

    gate = _sigmoid(_dot(xb, wgb_ref[...]) + bg_ref[...])
    merged = (gate[:, :D_MODEL] * _dot(ya.astype(BF16), wab_ref[...])
              + gate[:, D_MODEL:] * _dot(ym_ref[...], wmb_ref[...]))
    y = _dot(merged.astype(BF16), wob_ref[...])
    out_ref[...] = _layer_norm(ALPHA * x + y, lng_ref[...], lnb_ref[...])


def _merge(x, outs, lses, ym, w_gate, b_gate, w_br_a, w_br_m, w_o, ln_g, ln_b, layer):
    s = x.shape[0]
    tm = 512
    res = lambda shape: pl.BlockSpec((None,) + shape, lambda i: (layer, 0, 0), pipeline_mode=pl.Buffered(1))
    small = lambda n: pl.BlockSpec((1, n), lambda i: (0, 0))
    rows = lambda n: pl.BlockSpec((tm, n), lambda i: (i, 0))
    return pl.pallas_call(
        _merge_kernel,
        out_shape=jax.ShapeDtypeStruct((s, D_MODEL), F32),
        grid=(s // tm,),
        in_specs=[rows(D_MODEL), rows(D_A), rows(D_A), rows(D_A), rows(LANES), rows(LANES), rows(LANES), rows(D_M),
                  res((D_MODEL, 2 * D_MODEL)), small(2 * D_MODEL), res((D_A, D_MODEL)), res((D_M, D_MODEL)),
                  res((D_MODEL, D_MODEL)), small(D_MODEL), small(D_MODEL)],
        out_specs=rows(D_MODEL),
        scratch_shapes=[pltpu.VMEM((D_MODEL, 2 * D_MODEL), BF16), pltpu.VMEM((D_A, D_MODEL), BF16),
                        pltpu.VMEM((D_M, D_MODEL), BF16), pltpu.VMEM((D_MODEL, D_MODEL), BF16)],
        compiler_params=_params(("arbitrary",), 56),
        name="merge",
    )(x, *outs, *lses, ym, w_gate, b_gate.reshape(1, -1), w_br_a, w_br_m, w_o,
      ln_g.reshape(1, -1), ln_b.reshape(1, -1))


def _token_mixer_layer(x, biases, w_in, w_gate, b_gate, conv_w, conv_b, w_qk_m, w_v_m, w_if, b_if, m_norm_g,
                       w_br_a, w_br_m, w_o, ln_g, ln_b, layer):
    (q, k, v, xm, zm), perm = _in_proj(x, w_in, layer)
    outs, stats = [], []
    for (_, dilation), bias in zip(ATTN_PATTERNS, biases):
        qd, kd, vd = (q, k, v) if dilation == 1 else perm[DILATIONS.index(dilation)]
        o, st = _attn_pattern(qd, kd, vd, bias, dilation)
        outs.append(o)
        stats.append(st)
    qm, km, vm, gc, gr = _mlstm_prep(xm, conv_w, conv_b, w_qk_m, w_v_m, w_if, b_if, layer)
    ym = _mlstm_scan(qm, km, vm, gc, gr, zm, m_norm_g)
    return _merge(x, outs, stats, ym, w_gate, b_gate, w_br_a, w_br_m, w_o, ln_g, ln_b, layer)


def _ffn_kernel(x_ref, w1_ref, w3_ref, w2_ref, lng_ref, lnb_ref, out_ref, xb_ref, acc_ref):
    f = pl.program_id(1)

    @pl.when(f == 0)
    def _start():
        xb_ref[...] = x_ref[...].astype(BF16)
        acc_ref[...] = jnp.zeros_like(acc_ref)

    xb = xb_ref[...]
    a = _dot(xb, w1_ref[...].astype(BF16))
    g = _dot(xb, w3_ref[...].astype(BF16))
    hidden = (a * _sigmoid(a) * g).astype(BF16)
    acc_ref[...] += _dot(hidden, w2_ref[...].astype(BF16))

    @pl.when(f == pl.num_programs(1) - 1)
    def _finish():
        out_ref[...] = _layer_norm(ALPHA * x_ref[...] + acc_ref[...], lng_ref[...], lnb_ref[...])


def _dense_ffn(x, w13, w2, ln_g, ln_b, j):
    s = x.shape[0]
    tm = min(FFN_ROW_TILE, s)
    fc = FFN_FF_CHUNK
    nf = D_FF // fc
    small = pl.BlockSpec((1, D_MODEL), lambda i, f: (0, 0))
    return pl.pallas_call(
        _ffn_kernel,
        out_shape=jax.ShapeDtypeStruct((s, D_MODEL), F32),
        grid=(s // tm, nf),
        in_specs=[pl.BlockSpec((tm, D_MODEL), lambda i, f: (i, 0)),
                  pl.BlockSpec((None, D_MODEL, fc), lambda i, f: (j, 0, f)),
                  pl.BlockSpec((None, D_MODEL, fc), lambda i, f: (j, 0, nf + f)),
                  pl.BlockSpec((None, fc, D_MODEL), lambda i, f: (j, f, 0)),
                  small, small],
        out_specs=pl.BlockSpec((tm, D_MODEL), lambda i, f: (i, 0)),
        scratch_shapes=[pltpu.VMEM((tm, D_MODEL), BF16), pltpu.VMEM((tm, D_MODEL), F32)],
        compiler_params=_params(("arbitrary", "arbitrary"), 48),
        name="dense_ffn",
    )(x, w13, w13, w2, ln_g.reshape(1, -1), ln_b.reshape(1, -1))


def _router_kernel(x_ref, rw_ref, rb_ref, gate_ref, rank_ref, cnt_ref, xb_ref, carry_ref):
    tm = x_ref.shape[0]
    ne = N_EXPERTS

    @pl.when(pl.program_id(0) == 0)
    def _init():
        carry_ref[...] = jnp.zeros_like(carry_ref)

    x = x_ref[...]
    xb_ref[...] = x.astype(BF16)
    xs = _split3(x)
    ws = _split3(rw_ref[...])
    logits = rb_ref[...] + sum(_dot(xs[a], ws[b]) for a, b in ((2, 0), (0, 2), (1, 1), (1, 0), (0, 1), (0, 0)))
    lane = lax.broadcasted_iota(I32, (tm, ne), 1)
    v1 = jnp.max(logits, axis=-1, keepdims=True)
    i1 = jnp.min(jnp.where(logits == v1, lane, ne), axis=-1, keepdims=True)
    rest = jnp.where(lane == i1, -jnp.inf, logits)
    v2 = jnp.max(rest, axis=-1, keepdims=True)
    i2 = jnp.min(jnp.where(rest == v2, lane, ne), axis=-1, keepdims=True)
    e2 = jnp.exp(v2 - v1)
    den = 1.0 + e2
    sel1, sel2 = lane == i1, lane == i2
    gate_ref[...] = jnp.where(sel1, 1.0 / den, 0.0) + jnp.where(sel2, e2 / den, 0.0)
    sel = jnp.where(sel1 | sel2, 1.0, 0.0)
    ri = lax.broadcasted_iota(I32, (tm, tm), 0)
    ci = lax.broadcasted_iota(I32, (tm, tm), 1)
    before = (ri > ci).astype(BF16)
    carry = carry_ref[0:1, 0:ne]
    rank = _dot(before, sel.astype(BF16)) + carry
    rank_ref[...] = jnp.where(sel > 0.0, rank, -1.0)
    total = carry + jnp.sum(sel, axis=0, keepdims=True)
    carry_ref[0:1, 0:ne] = total
    cnt_ref[...] = total


def _router(x, router_w, router_b):
    s = x.shape[0]
    tm = 512
    ne = N_EXPERTS
    return pl.pallas_call(
        _router_kernel,
        out_shape=(jax.ShapeDtypeStruct((s, ne), F32), jax.ShapeDtypeStruct((s, ne), F32),
                   jax.ShapeDtypeStruct((1, ne), F32), jax.ShapeDtypeStruct((s, D_MODEL), BF16)),
        grid=(s // tm,),
        in_specs=[pl.BlockSpec((tm, D_MODEL), lambda i: (i, 0)),
                  pl.BlockSpec((D_MODEL, ne), lambda i: (0, 0)),
                  pl.BlockSpec((1, ne), lambda i: (0, 0))],
        out_specs=(pl.BlockSpec((tm, ne), lambda i: (i, 0)), pl.BlockSpec((tm, ne), lambda i: (i, 0)),
                   pl.BlockSpec((1, ne), lambda i: (0, 0)), pl.BlockSpec((tm, D_MODEL), lambda i: (i, 0))),
        scratch_shapes=[pltpu.VMEM((8, LANES), F32)],
        compiler_params=_params(("arbitrary",), 32),
        name="moe_router",
    )(x, router_w, router_b.reshape(1, ne))


def _chunk_copy(src_hbm, buf_ref, sem_ref, chunk, slot):
    rows = buf_ref.shape[1]
    start = pl.multiple_of(chunk * rows, rows)
    return pltpu.make_async_copy(src_hbm.at[pl.ds(start, rows), :], buf_ref.at[slot], sem_ref.at[slot])


def _moe_gemm_kernel(te_ref, nu_ref, off_ref, ist_ref, clo_ref, xb_hbm, rank_ref, w1_ref, w3_ref, w2_ref,
                     y_ref, xs_ref, acc_ref, buf_ref, sem_ref):
    gr, ck = MOE_GATHER_ROWS, MOE_GATHER_TOKENS
    sub = MOE_GROUP_TILE // gr
    i, f = pl.program_id(0), pl.program_id(1)
    last_f = pl.num_programs(1) - 1

    @pl.when(i < nu_ref[0])
    def _tile():
        @pl.when(f == 0)
        def _gather_rows():
            e = te_ref[i]
            base = i * sub
            k0, k1 = ist_ref[base], ist_ref[base + sub]

            def item(k):
                s = sum((k >= ist_ref[base + j]).astype(I32) for j in range(1, sub))
                return s, clo_ref[base + s] + (k - ist_ref[base + s])

            acc_ref[...] = jnp.zeros_like(acc_ref)

            @pl.when(k1 > k0)
            def _first():
                _chunk_copy(xb_hbm, buf_ref, sem_ref, item(k0)[1], 0).start()

            def step(k, carry):
                slot = (k - k0) % 2
                s, chunk = item(k)
                _chunk_copy(xb_hbm, buf_ref, sem_ref, chunk, slot).wait()

                @pl.when(k + 1 < k1)
                def _next():
                    _chunk_copy(xb_hbm, buf_ref, sem_ref, item(k + 1)[1], 1 - slot).start()

                rank0 = ((base + s) * gr - off_ref[e]).astype(F32)
                row_rank = rank0 + lax.broadcasted_iota(I32, (gr, ck), 0).astype(F32)
                hit = row_rank == rank_ref[e, pl.ds(chunk, 1), :]
                rows = pl.ds(pl.multiple_of(s * gr, gr), gr)
                acc_ref[rows, :] += _dot(jnp.where(hit, 1.0, 0.0).astype(BF16), buf_ref[slot])
                return carry

            lax.fori_loop(k0, k1, step, 0)
            xs_ref[...] = acc_ref[...].astype(BF16)
            acc_ref[...] = jnp.zeros_like(acc_ref)

        xb = xs_ref[...]
        a = _dot(xb, w1_ref[0].astype(BF16))
        g = _dot(xb, w3_ref[0].astype(BF16))
        hidden = (a * _sigmoid(a) * g).astype(BF16)
        acc_ref[...] += _dot(hidden, w2_ref[0].astype(BF16))

        @pl.when(f == last_f)
        def _finish():
            y_ref[...] = acc_ref[...].astype(y_ref.dtype)

    @pl.when(jnp.logical_and(i >= nu_ref[0], f == last_f))
    def _unused_tile():
        y_ref[...] = jnp.zeros_like(y_ref)


def _combine_kernel(off_ref, cn_ref, cie_ref, cic_ref, x_ref, ys_hbm, rank_ref, gate_ref, lng_ref, lnb_ref,
                    out_ref, acc_ref, buf_ref, sem_ref, *, slots):
    tt, cr = MOE_COMBINE_TOKENS, MOE_COMBINE_ROWS
    j = pl.program_id(0)
    n = cn_ref[j]
    base = j * slots
    acc_ref[...] = jnp.zeros_like(acc_ref)

    @pl.when(n > 0)
    def _first():
        _chunk_copy(ys_hbm, buf_ref, sem_ref, cic_ref[base], 0).start()

    lane = lax.broadcasted_iota(I32, (tt, N_EXPERTS), 1)

    def step(k, carry):
        slot = k % 2
        e, chunk = cie_ref[base + k], cic_ref[base + k]
        _chunk_copy(ys_hbm, buf_ref, sem_ref, chunk, slot).wait()

        @pl.when(k + 1 < n)
        def _next():
            _chunk_copy(ys_hbm, buf_ref, sem_ref, cic_ref[base + k + 1], 1 - slot).start()

        rank = jnp.max(jnp.where(lane == e, rank_ref[...], -1.0), axis=-1, keepdims=True)
        gate = jnp.sum(jnp.where(lane == e, gate_ref[...], 0.0), axis=-1, keepdims=True)
        pos = jnp.where(rank >= 0.0, rank + off_ref[e].astype(F32), -1.0)
        col = (chunk * cr).astype(F32) + lax.broadcasted_iota(I32, (tt, cr), 1).astype(F32)
        acc_ref[...] += gate * _dot(jnp.where(pos == col, 1.0, 0.0).astype(BF16), buf_ref[slot])
        return carry

    lax.fori_loop(0, n, step, 0)
    out_ref[...] = _layer_norm(ALPHA * x_ref[...] + acc_ref[...], lng_ref[...], lnb_ref[...])


def _moe_layer(x, router_w, router_b, w13, w2, ln_g, ln_b):
    s = x.shape[0]
    ne, gt, gr, ck = N_EXPERTS, MOE_GROUP_TILE, MOE_GATHER_ROWS, MOE_GATHER_TOKENS
    tt, cr = MOE_COMBINE_TOKENS, MOE_COMBINE_ROWS
    sub = gt // gr
    nck = s // ck
    n_group_tiles = (TOP_K * s) // gt + ne
    n_sub = n_group_tiles * sub

    gate, rank, counts, xb = _router(x, router_w, router_b)

    cnt = counts[0].astype(I32)
    padded = ((cnt + gt - 1) // gt) * gt
    off = (jnp.cumsum(padded) - padded).astype(I32)
    n_used = (jnp.sum(padded) // gt).astype(I32)
    tile_expert = jnp.clip(
        jnp.searchsorted(jnp.cumsum(padded), jnp.arange(n_group_tiles, dtype=I32) * gt, side="right"), 0, ne - 1
    ).astype(I32)
    routed = (rank >= 0.0).reshape(nck, ck, ne)
    cum = jnp.cumsum(jnp.sum(routed, axis=1), axis=0).astype(I32).T

    u = jnp.arange(n_sub, dtype=I32)
    e_u = tile_expert[u // sub]
    r0 = u * gr - off[e_u]
    r1 = jnp.minimum(r0 + gr, cnt[e_u]) - 1
    live = (u // sub < n_used) & (r0 < cnt[e_u])
    cum_u = cum[e_u]
    c_lo = jnp.sum(cum_u <= r0[:, None], axis=1).astype(I32)
    c_hi = jnp.sum(cum_u <= r1[:, None], axis=1).astype(I32)
    n_items = jnp.where(live, c_hi - c_lo + 1, 0)
    item_start = jnp.concatenate([jnp.zeros((1,), I32), jnp.cumsum(n_items).astype(I32)])
    c_lo = jnp.where(live, c_lo, 0)

    fc = MOE_FF_CHUNK
    nf = D_FF_E // fc
    tile_of = lambda i, nu: jnp.minimum(i, nu[0] - 1)
    chunk_of = lambda i, f, nu: jnp.where(i < nu[0], f, nf - 1)
    ys = pl.pallas_call(
        _moe_gemm_kernel,
        out_shape=jax.ShapeDtypeStruct((n_group_tiles * gt, D_MODEL), BF16),
        grid_spec=pltpu.PrefetchScalarGridSpec(
            num_scalar_prefetch=5,
            grid=(n_group_tiles, nf),
            in_specs=[pl.BlockSpec(memory_space=pl.ANY),
                      pl.BlockSpec((ne, nck, ck), lambda i, f, te, nu, *_: (0, 0, 0)),
                      pl.BlockSpec((1, D_MODEL, fc),
                                   lambda i, f, te, nu, *_: (te[tile_of(i, nu)], 0, chunk_of(i, f, nu))),
                      pl.BlockSpec((1, D_MODEL, fc),
                                   lambda i, f, te, nu, *_: (te[tile_of(i, nu)], 0, nf + chunk_of(i, f, nu))),
                      pl.BlockSpec((1, fc, D_MODEL),
                                   lambda i, f, te, nu, *_: (te[tile_of(i, nu)], chunk_of(i, f, nu), 0))],
            out_specs=pl.BlockSpec((gt, D_MODEL), lambda i, f, *_: (i, 0)),
            scratch_shapes=[pltpu.VMEM((gt, D_MODEL), BF16), pltpu.VMEM((gt, D_MODEL), F32),
                            pltpu.VMEM((2, ck, D_MODEL), BF16), pltpu.SemaphoreType.DMA((2,))]),
        compiler_params=_params(("arbitrary", "arbitrary"), 48),
        name="moe_gemm",
    )(tile_expert, n_used.reshape(1), off, item_start, c_lo, xb, rank.T.reshape(ne, nck, ck), w13, w13, w2)

    njc = s // tt
    per = tt // ck
    incl = cum[:, per - 1::per]
    lo = off[:, None] + jnp.concatenate([jnp.zeros((ne, 1), I32), incl[:, :-1]], axis=1)
    hi = off[:, None] + incl
    first, last = lo // cr, (hi - 1) // cr
    max_chunks = tt // cr + 1
    kk = jnp.arange(max_chunks, dtype=I32)
    slot_chunk = (first[:, :, None] + kk).transpose(1, 0, 2).reshape(njc, -1)
    slot_live = ((hi > lo)[:, :, None] & (first[:, :, None] + kk <= last[:, :, None])).transpose(1, 0, 2)
    slot_live = slot_live.reshape(njc, -1)
    slots = ne * max_chunks
    slot_expert = jnp.broadcast_to(jnp.repeat(jnp.arange(ne, dtype=I32), max_chunks)[None, :], (njc, slots))
    dest = jnp.cumsum(slot_live, axis=1) - 1
    place = slot_live[:, :, None] & (dest[:, :, None] == jnp.arange(slots)[None, None, :])
    item_chunk = jnp.sum(jnp.where(place, slot_chunk[:, :, None], 0), axis=1).astype(I32).reshape(-1)
    item_expert = jnp.sum(jnp.where(place, slot_expert[:, :, None], 0), axis=1).astype(I32).reshape(-1)
    item_count = jnp.sum(slot_live, axis=1).astype(I32)

    small = pl.BlockSpec((1, D_MODEL), lambda j, *_: (0, 0))
    return pl.pallas_call(
        functools.partial(_combine_kernel, slots=slots),
        out_shape=jax.ShapeDtypeStruct((s, D_MODEL), F32),
        grid_spec=pltpu.PrefetchScalarGridSpec(
            num_scalar_prefetch=4,
            grid=(njc,),
            in_specs=[pl.BlockSpec((tt, D_MODEL), lambda j, *_: (j, 0)),
                      pl.BlockSpec(memory_space=pl.ANY),
                      pl.BlockSpec((tt, ne), lambda j, *_: (j, 0)),
                      pl.BlockSpec((tt, ne), lambda j, *_: (j, 0)),
                      small, small],
            out_specs=pl.BlockSpec((tt, D_MODEL), lambda j, *_: (j, 0)),
            scratch_shapes=[pltpu.VMEM((tt, D_MODEL), F32), pltpu.VMEM((2, cr, D_MODEL), BF16),
                            pltpu.SemaphoreType.DMA((2,))]),
        compiler_params=_params(("arbitrary",), 32),
        name="moe_combine",
    )(off, item_count, item_expert, item_chunk, x, ys, rank, gate, ln_g.reshape(1, -1), ln_b.reshape(1, -1))


def kernel(x, rel_bias, w_in, w_gate, b_gate, conv_w, conv_b, w_qk_m, w_v_m, w_if, b_if, m_norm_g, w_br_a, w_br_m,
           w_o, ln_g, ln_b, ffn_w13, ffn_w2, router_w, router_b, exp_w13, exp_w2):
    batch, seq, _ = x.shape
    assert batch == 1
    h = x.reshape(seq, D_MODEL)
    biases = [_attn_bias(rel_bias, window, dilation) for window, dilation in ATTN_PATTERNS]
    for l in range(DEPTH):
        h = _token_mixer_layer(h, biases, w_in, w_gate, b_gate[l], conv_w[l], conv_b[l], w_qk_m, w_v_m, w_if[l],
                               b_if[l], m_norm_g[l], w_br_a, w_br_m, w_o, ln_g[l, 0], ln_b[l, 0], l)
        j = l // 2
        if l % 2 == 0:
            h = _dense_ffn(h, ffn_w13, ffn_w2, ln_g[l, 1], ln_b[l, 1], j)
        else:
            h = _moe_layer(h, router_w[j], router_b[j], exp_w13[j], exp_w2[j], ln_g[l, 1], ln_b[l, 1])
    return h.reshape(batch, seq, D_MODEL)
```

```python
import functools
import math

import jax
import jax.numpy as jnp
from jax import lax
from jax.experimental import pallas as pl
from jax.experimental.pallas import tpu as pltpu

F32 = jnp.float32
BF16 = jnp.bfloat16
I32 = jnp.int32

D_MODEL = 1024
DEPTH = 2
N_HEADS_A = 8
HEAD_DIM_A = 64
D_A = N_HEADS_A * HEAD_DIM_A
ATTN_PATTERNS = ((128, 1), (512, 4), (2048, 16))
ATTN_BLOCK = 128
NUM_BUCKETS = 32
MAX_DISTANCE = 2048
N_HEADS_M = 4
HEAD_DIM_M = 128
D_M = N_HEADS_M * HEAD_DIM_M
CONV_K = 4
MLSTM_CHUNK = 128
N_PROJ = 5
P_IN = 3 * D_A + 2 * D_M
D_FF = 2816
N_EXPERTS = 8
TOP_K = 2
D_FF_E = 3584
ALPHA = (2.0 * DEPTH) ** 0.25
LN_EPS = 1e-5

NEG = -1e30
LANES = 128
HALO_ROWS = 16
MIB = 1024 * 1024

MOE_GROUP_TILE = 1024
MOE_FF_CHUNK = 512
MOE_GATHER_ROWS = 256
MOE_GATHER_TOKENS = 256
MOE_COMBINE_TOKENS = 512
MOE_COMBINE_ROWS = 256
MOE_DMA_DEPTH = 4
FFN_ROW_TILE = 1024
FFN_FF_CHUNK = 256


def _params(sem, vmem_mib):
    return pltpu.CompilerParams(dimension_semantics=sem, vmem_limit_bytes=vmem_mib * MIB)


def _sigmoid(x):
    return 1.0 / (1.0 + jnp.exp(-x))


def _layer_norm(r, g, b):
    mu = jnp.mean(r, axis=-1, keepdims=True)
    c = r - mu
    var = jnp.mean(c * c, axis=-1, keepdims=True)
    return c * lax.rsqrt(var + LN_EPS) * g + b


def _split3(a):
    hi = a.astype(BF16)
    r1 = a - hi.astype(F32)
    mid = r1.astype(BF16)
    lo = (r1 - mid.astype(F32)).astype(BF16)
    return hi, mid, lo


def _dot(a, b):
    return jnp.dot(a, b, preferred_element_type=F32)


def _dot_nt(a, b):
    return lax.dot_general(a, b, (((1,), (1,)), ((), ())), preferred_element_type=F32)


def _dot_tn(a, b):
    return lax.dot_general(a, b, (((0,), (0,)), ((), ())), preferred_element_type=F32)


IN_PROJ_ROWS = 1024
DILATIONS = tuple(d for _, d in ATTN_PATTERNS if d > 1)


def _in_proj_kernel(x_ref, w_ref, *refs):
    nat = refs[:N_PROJ]
    perm = refs[N_PROJ:N_PROJ + 3 * len(DILATIONS)]
    wb_ref, y_ref = refs[-2:]

    @pl.when(pl.program_id(0) == 0)
    def _cast_weights():
        wb_ref[...] = w_ref[...].astype(BF16)

    xb = x_ref[...].astype(BF16)
    for j in range(N_PROJ):
        y = _dot(xb, wb_ref[:, j * D_A:(j + 1) * D_A])
        if j == 0:
            y = y * (HEAD_DIM_A ** -0.5)
        nat[j][...] = y.astype(BF16)
        if j >= 3:
            continue
        for c in range(D_A // LANES):
            y_ref[c] = y[:, c * LANES:(c + 1) * LANES]
        for di, d in enumerate(DILATIONS):
            out = perm[di * 3 + j]
            tiles, _, rpc, _ = out.shape
            for t in range(tiles):
                for r in range(d):
                    for c in range(D_A // LANES):
                        out[t, r, :, c * LANES:(c + 1) * LANES] = (
                            y_ref[c, pl.ds(t * d * rpc + r, rpc, stride=d), :].astype(BF16))


def _in_proj(x, w_in, layer):
    s = x.shape[0]
    tm = IN_PROJ_ROWS
    blk = ATTN_BLOCK
    out_shape = [jax.ShapeDtypeStruct((s, D_A), BF16)] * N_PROJ
    out_specs = [pl.BlockSpec((tm, D_A), lambda i: (i, 0))] * N_PROJ
    for d in DILATIONS:
        tile = d * blk
        if tile <= tm:
            spec = pl.BlockSpec((tm // tile, d, blk, D_A), lambda i: (i, 0, 0, 0))
        else:
            parts = tile // tm
            spec = pl.BlockSpec((1, d, blk // parts, D_A), lambda i, parts=parts: (i // parts, 0, i % parts, 0))
        out_shape += [jax.ShapeDtypeStruct((s // tile, d, blk, D_A), BF16)] * 3
        out_specs += [spec] * 3
    outs = pl.pallas_call(
        _in_proj_kernel,
        out_shape=tuple(out_shape),
        grid=(s // tm,),
        in_specs=[
            pl.BlockSpec((tm, D_MODEL), lambda i: (i, 0)),
            pl.BlockSpec((None, D_MODEL, P_IN), lambda i: (layer, 0, 0), pipeline_mode=pl.Buffered(1)),
        ],
        out_specs=tuple(out_specs),
        scratch_shapes=[pltpu.VMEM((D_MODEL, P_IN), BF16), pltpu.VMEM((D_A // LANES, tm, LANES), F32)],
        compiler_params=_params(("arbitrary",), 56),
        name="in_proj",
    )(x, w_in)
    nat = outs[:N_PROJ]
    perm = [tuple(t.reshape(s, D_A) for t in outs[N_PROJ + 3 * i:N_PROJ + 3 * i + 3]) for i in range(len(DILATIONS))]
    return nat, perm


ATTN_STEP_BLOCKS = 16


def _rel_bucket(dist):
    exact = NUM_BUCKETS // 2
    d = jnp.maximum(dist, exact).astype(F32)
    log_b = exact + (jnp.log(d / exact) / math.log(MAX_DISTANCE / exact) * (NUM_BUCKETS - exact)).astype(I32)
    return jnp.where(dist < exact, dist, jnp.minimum(log_b, NUM_BUCKETS - 1))


def _attn_bias(rel_bias, window, dilation):
    blk = ATTN_BLOCK
    qi = jnp.arange(blk)[:, None]
    kj = jnp.arange(2 * blk)[None, :]
    rel = qi + blk - kj
    bucket = _rel_bucket(jnp.maximum(rel, 0) * dilation)
    onehot = (bucket[..., None] == jnp.arange(NUM_BUCKETS)).astype(F32)
    bias = jnp.einsum("qkb,bh->hqk", onehot, rel_bias.astype(F32), precision=lax.Precision.HIGHEST)
    mask = (rel >= 0) & (rel <= window // dilation)
    return jnp.where(mask[None], bias, NEG)


def _attn_kernel(q_ref, kp_ref, kc_ref, vp_ref, vc_ref, bias_ref, o_ref, st_ref, kf_ref, vf_ref, on_ref, sn_ref,
                 *, dilation):
    blk = ATTN_BLOCK
    nb = ATTN_STEP_BLOCKS
    rows = nb * blk
    step = pl.program_id(0)
    kf_ref[0:rows, :] = kp_ref[...]
    kf_ref[rows:, :] = kc_ref[...]
    vf_ref[0:rows, :] = vp_ref[...]
    vf_ref[rows:, :] = vc_ref[...]
    kj = lax.broadcasted_iota(I32, (blk, 2 * blk), 1)
    lane = lax.broadcasted_iota(I32, (blk, LANES), 1)
    lane_k = lax.broadcasted_iota(I32, (2 * blk, LANES), 1)
    ones = jnp.ones((2 * blk, LANES), BF16)

    def block(g, carry):
        cur = pl.multiple_of(g * blk, blk)
        prev = pl.multiple_of((nb + g - dilation) * blk, blk)
        pen = jnp.where(jnp.logical_and(step == 0, g < dilation), NEG, 0.0).astype(F32)
        prev_pen = jnp.where(kj < blk, pen, 0.0)
        stats = jnp.zeros((blk, LANES), F32)
        tile_i, cls = g // dilation, g % dilation
        dst = pl.ds(tile_i * (blk * dilation) + cls, blk, stride=dilation)
        for hp in range(N_HEADS_A // 2):
            ls = slice(hp * LANES, (hp + 1) * LANES)
            qb = q_ref[pl.ds(cur, blk), ls]
            kb = jnp.concatenate([kf_ref[pl.ds(prev, blk), ls], kf_ref[pl.ds(rows + cur, blk), ls]], axis=0)
            vb = jnp.concatenate([vf_ref[pl.ds(prev, blk), ls], vf_ref[pl.ds(rows + cur, blk), ls]], axis=0)
            acc = jnp.zeros((blk, LANES), F32)
            for par in range(2):
                h = 2 * hp + par
                own_q = (lane // HEAD_DIM_A) == par
                own_k = (lane_k // HEAD_DIM_A) == par
                qm = jnp.where(own_q, qb, jnp.zeros_like(qb))
                logits = _dot_nt(qm, kb) + bias_ref[h] + prev_pen
                m = jnp.max(logits, axis=-1, keepdims=True)
                p = jnp.exp(logits - m).astype(BF16)
                vm = jnp.where(own_k, vb, jnp.zeros_like(vb))
                ol = _dot(p, jnp.concatenate([vm, ones], axis=1))
                acc = acc + ol[:, :LANES]
                stats = jnp.where(lane == h, m, stats)
                stats = jnp.where(lane == N_HEADS_A + h, ol[:, LANES:], stats)
            on_ref[hp, dst, :] = acc
        sn_ref[dst, :] = stats
        return carry

    lax.fori_loop(0, nb, block, 0)
    for hp in range(N_HEADS_A // 2):
        o_ref[:, hp * LANES:(hp + 1) * LANES] = on_ref[hp].astype(o_ref.dtype)
    st_ref[...] = sn_ref[...]


def _attn_pattern(q, k, v, bias, dilation):
    s = q.shape[0]
    rows = ATTN_STEP_BLOCKS * ATTN_BLOCK
    cur = pl.BlockSpec((rows, D_A), lambda i: (i, 0))
    prev = pl.BlockSpec((rows, D_A), lambda i: (jnp.maximum(i - 1, 0), 0))
    return pl.pallas_call(
        functools.partial(_attn_kernel, dilation=dilation),
        out_shape=(jax.ShapeDtypeStruct((s, D_A), BF16), jax.ShapeDtypeStruct((s, LANES), F32)),
        grid=(s // rows,),
        in_specs=[cur, prev, cur, prev, cur,
                  pl.BlockSpec((N_HEADS_A, ATTN_BLOCK, 2 * ATTN_BLOCK), lambda i: (0, 0, 0))],
        out_specs=(pl.BlockSpec((rows, D_A), lambda i: (i, 0)), pl.BlockSpec((rows, LANES), lambda i: (i, 0))),
        scratch_shapes=[pltpu.VMEM((2 * rows, D_A), BF16), pltpu.VMEM((2 * rows, D_A), BF16),
                        pltpu.VMEM((D_A // LANES, rows, LANES), F32), pltpu.VMEM((rows, LANES), F32)],
        compiler_params=_params(("arbitrary",), 48),
        name=f"attn_d{dilation}",
    )(q, k, k, v, v, bias)


def _log_sigmoid(x):
    return jnp.minimum(x, 0.0) - jnp.log(1.0 + jnp.exp(-jnp.abs(x)))


def _mlstm_prep_kernel(xm_ref, halo_ref, cw_ref, cb_ref, wqk_ref, wv_ref, wif_ref, wift_ref, bif_ref, bift_ref,
                       q_ref, k_ref, v_ref, gc_ref, gr_ref):
    tm = xm_ref.shape[0]
    hm, hd, lc = N_HEADS_M, HEAD_DIM_M, MLSTM_CHUNK
    xmb = xm_ref[...]
    halo = jnp.where(pl.program_id(0) == 0, 0.0, halo_ref[...].astype(F32))
    xx = jnp.concatenate([halo, xmb.astype(F32)], axis=0)
    conv = jnp.zeros((tm, D_M), F32) + cb_ref[...]
    for j in range(CONV_K):
        start = HALO_ROWS - (CONV_K - 1) + j
        conv = conv + cw_ref[j:j + 1, :] * xx[start:start + tm, :]
    xcb = (conv * _sigmoid(conv)).astype(BF16)

    qs, ks, vs = [], [], []
    for h in range(hm):
        sl = slice(h * hd, (h + 1) * hd)
        qs.append(_dot(xcb[:, sl], wqk_ref[0, h].astype(BF16)))
        ks.append(_dot(xcb[:, sl], wqk_ref[1, h].astype(BF16)))
        vs.append(_dot(xmb[:, sl], wv_ref[h].astype(BF16)))
    q = jnp.concatenate(qs, axis=1)
    k = jnp.concatenate(ks, axis=1)
    v = jnp.concatenate(vs, axis=1)
    q_ref[...] = q.astype(BF16)
    k_ref[...] = (k * (hd ** -0.5)).astype(BF16)
    v_ref[...] = v.astype(BF16)

    qkv = jnp.concatenate([q, k, v], axis=1).astype(BF16)
    gates_c = _dot(qkv, wif_ref[...].astype(BF16)) + bif_ref[...]
    gates_r = _dot_nt(wift_ref[...].astype(BF16), qkv) + bift_ref[...]
    lane = lax.broadcasted_iota(I32, gates_c.shape, 1)
    row = lax.broadcasted_iota(I32, gates_r.shape, 0)
    gc_ref[...] = jnp.where(lane < hm, gates_c, _log_sigmoid(gates_c))
    gr_ref[...] = jnp.where(row < hm, gates_r, _log_sigmoid(gates_r))

    ri = lax.broadcasted_iota(I32, (lc, lc), 0)
    ci = lax.broadcasted_iota(I32, (lc, lc), 1)
    lower = (ri >= ci).astype(BF16)
    upper = (ri <= ci).astype(BF16)
    lane_c = lax.broadcasted_iota(I32, (lc, 2 * hm), 1)
    row_c = lax.broadcasted_iota(I32, (2 * hm, lc), 0)
    for c in range(tm // lc):
        rs = slice(c * lc, (c + 1) * lc)
        gcc = gc_ref[rs, :]
        grc = gr_ref[:, rs]
        cum_c = sum(_dot(lower, part) for part in _split3(gcc))
        cum_r = sum(_dot(part, upper) for part in _split3(grc))
        gc_ref[rs, :] = jnp.where(lane_c < hm, gcc, cum_c)
        gr_ref[:, rs] = jnp.where(row_c < hm, grc, cum_r)


def _mlstm_prep(xm, conv_w, conv_b, w_qk_m, w_v_m, w_if, b_if, layer):
    s = xm.shape[0]
    tm = 512
    hpt = tm // HALO_ROWS
    full = lambda shape: pl.BlockSpec(shape, lambda i: (0,) * len(shape))
    stacked = lambda shape: pl.BlockSpec((None,) + shape, lambda i: (layer,) + (0,) * len(shape))
    row_spec = pl.BlockSpec((tm, D_M), lambda i: (i, 0))
    return pl.pallas_call(
        _mlstm_prep_kernel,
        out_shape=(jax.ShapeDtypeStruct((s, D_M), BF16),) * 3
        + (jax.ShapeDtypeStruct((s, 2 * N_HEADS_M), F32), jax.ShapeDtypeStruct((2 * N_HEADS_M, s), F32)),
        grid=(s // tm,),
        in_specs=[
            row_spec,
            pl.BlockSpec((HALO_ROWS, D_M), lambda i: (jnp.maximum(i * hpt - 1, 0), 0)),
            full((CONV_K, D_M)), full((1, D_M)),
            stacked((2, N_HEADS_M, HEAD_DIM_M, HEAD_DIM_M)), stacked((N_HEADS_M, HEAD_DIM_M, HEAD_DIM_M)),
            full((3 * D_M, 2 * N_HEADS_M)), full((2 * N_HEADS_M, 3 * D_M)),
            full((1, 2 * N_HEADS_M)), full((2 * N_HEADS_M, 1)),
        ],
        out_specs=(row_spec, row_spec, row_spec,
                   pl.BlockSpec((tm, 2 * N_HEADS_M), lambda i: (i, 0)),
                   pl.BlockSpec((2 * N_HEADS_M, tm), lambda i: (0, i))),
        compiler_params=_params(("arbitrary",), 32),
        name="mlstm_prep",
    )(xm, xm, conv_w, conv_b.reshape(1, D_M), w_qk_m, w_v_m, w_if, w_if.T,
      b_if.reshape(1, -1), b_if.reshape(-1, 1))


def _mlstm_scan_kernel(q_ref, k_ref, v_ref, gc_ref, gr_ref, z_ref, g_ref, y_ref, c_ref, n_ref, m_ref, *, chunks):
    hm, hd, lc = N_HEADS_M, HEAD_DIM_M, MLSTM_CHUNK

    @pl.when(pl.program_id(0) == 0)
    def _init():
        c_ref[...] = jnp.zeros_like(c_ref)
        n_ref[...] = jnp.zeros_like(n_ref)
        m_ref[...] = jnp.zeros_like(m_ref)

    ri = lax.broadcasted_iota(I32, (lc, lc), 0)
    ci = lax.broadcasted_iota(I32, (lc, lc), 1)
    causal = ri >= ci
    for c in range(chunks):
        rs = slice(c * lc, (c + 1) * lc)
        gc = gc_ref[rs, :]
        gr = gr_ref[:, rs]
        for h in range(hm):
            hs = slice(h * hd, (h + 1) * hd)
            q = q_ref[rs, hs]
            k = k_ref[rs, hs]
            v = v_ref[rs, hs]
            i_col, b_col = gc[:, h:h + 1], gc[:, hm + h:hm + h + 1]
            i_row, b_row = gr[h:h + 1, :], gr[hm + h:hm + h + 1, :]
            m_prev = m_ref[h:h + 1, 0:1]
            n_prev = n_ref[h:h + 1, :]
            c_prev = c_ref[h]

            dm = jnp.where(causal, b_col - b_row + i_row, NEG)
            inter = b_col + m_prev
            m_loc = jnp.maximum(inter, jnp.max(dm, axis=-1, keepdims=True))
            dexp = jnp.exp(dm - m_loc)
            inter_w = jnp.exp(inter - m_loc)
            sc = _dot_nt(q, k) * dexp
            num = inter_w * _dot(q, c_prev.astype(BF16)) + _dot(sc.astype(BF16), v)
            nq = (inter_w * jnp.sum(q.astype(F32) * n_prev, axis=-1, keepdims=True)
                  + jnp.sum(sc, axis=-1, keepdims=True))
            hval = num / jnp.maximum(jnp.abs(nq), jnp.exp(-m_loc))

            b_last = b_col[lc - 1:lc, :]
            g_col = b_last - b_col + i_col
            m_new = jnp.maximum(b_last + m_prev, jnp.max(g_col, axis=0, keepdims=True))
            decay = jnp.exp(b_last + m_prev - m_new)
            kw = k.astype(F32) * jnp.exp(g_col - m_new)
            c_ref[h] = decay * c_prev + _dot_tn(kw.astype(BF16), v)
            n_ref[h:h + 1, :] = decay * n_prev + jnp.sum(kw, axis=0, keepdims=True)
            m_ref[h:h + 1, :] = jnp.broadcast_to(m_new, (1, LANES))

            mu = jnp.mean(hval, axis=-1, keepdims=True)
            cen = hval - mu
            var = jnp.mean(cen * cen, axis=-1, keepdims=True)
            hn = cen * lax.rsqrt(var + LN_EPS) * g_ref[:, hs]
            y_ref[rs, hs] = (_sigmoid(z_ref[rs, hs].astype(F32)) * hn).astype(y_ref.dtype)


def _mlstm_scan(q, k, v, gc, gr, z, m_norm_g):
    s = q.shape[0]
    chunks = 2
    tm = chunks * MLSTM_CHUNK
    row_spec = pl.BlockSpec((tm, D_M), lambda i: (i, 0))
    return pl.pallas_call(
        functools.partial(_mlstm_scan_kernel, chunks=chunks),
        out_shape=jax.ShapeDtypeStruct((s, D_M), BF16),
        grid=(s // tm,),
        in_specs=[row_spec, row_spec, row_spec,
                  pl.BlockSpec((tm, 2 * N_HEADS_M), lambda i: (i, 0)),
                  pl.BlockSpec((2 * N_HEADS_M, tm), lambda i: (0, i)),
                  row_spec,
                  pl.BlockSpec((1, D_M), lambda i: (0, 0))],
        out_specs=row_spec,
        scratch_shapes=[pltpu.VMEM((N_HEADS_M, HEAD_DIM_M, HEAD_DIM_M), F32),
                        pltpu.VMEM((8, HEAD_DIM_M), F32),
                        pltpu.VMEM((8, LANES), F32)],
        compiler_params=_params(("arbitrary",), 32),
        name="mlstm_scan",
    )(q, k, v, gc, gr, z, m_norm_g.reshape(1, D_M))


def _merge_kernel(x_ref, o1_ref, o2_ref, o3_ref, l1_ref, l2_ref, l3_ref, ym_ref,
                  wg_ref, bg_ref, wa_ref, wm_ref, wo_ref, lng_ref, lnb_ref, out_ref,
                  wgb_ref, wab_ref, wmb_ref, wob_ref):
    @pl.when(pl.program_id(0) == 0)
    def _cast_weights():
        wgb_ref[...] = wg_ref[...].astype(BF16)
        wab_ref[...] = wa_ref[...].astype(BF16)
        wmb_ref[...] = wm_ref[...].astype(BF16)
        wob_ref[...] = wo_ref[...].astype(BF16)

    x = x_ref[...]
    xb = x.astype(BF16)
    stats = (l1_ref[...], l2_ref[...], l3_ref[...])
    mx = jnp.maximum(jnp.maximum(stats[0], stats[1]), stats[2])
    es = [jnp.exp(st - mx) for st in stats]
    ls = [pltpu.roll(st, LANES - N_HEADS_A, 1) for st in stats]
    den = es[0] * ls[0] + es[1] * ls[1] + es[2] * ls[2]
    head_lane = lax.broadcasted_iota(I32, den.shape, 1) < N_HEADS_A
    inv = jnp.where(head_lane, 1.0 / den, 0.0)
    hrow = lax.broadcasted_iota(I32, (LANES, D_A), 0)
    hcol = lax.broadcasted_iota(I32, (LANES, D_A), 1) // HEAD_DIM_A
    expand = (hrow == hcol).astype(BF16)
    ya = jnp.zeros((x.shape[0], D_A), F32)
    for e, o_ref in zip(es, (o1_ref, o2_ref, o3_ref)):
        w = sum(_dot(part, expand) for part in _split3(e * inv))
        ya = ya + w * o_ref[...].astype(F32)

    gate = _sigmoid(_dot(xb, wgb_ref[...]) + bg_ref[...])
    merged = (gate[:, :D_MODEL] * _dot(ya.astype(BF16), wab_ref[...])
              + gate[:, D_MODEL:] * _dot(ym_ref[...], wmb_ref[...]))
    y = _dot(merged.astype(BF16), wob_ref[...])
    out_ref[...] = _layer_norm(ALPHA * x + y, lng_ref[...], lnb_ref[...])


def _merge(x, outs, lses, ym, w_gate, b_gate, w_br_a, w_br_m, w_o, ln_g, ln_b, layer):
    s = x.shape[0]
    tm = 512
    res = lambda shape: pl.BlockSpec((None,) + shape, lambda i: (layer, 0, 0), pipeline_mode=pl.Buffered(1))
    small = lambda n: pl.BlockSpec((1, n), lambda i: (0, 0))
    rows = lambda n: pl.BlockSpec((tm, n), lambda i: (i, 0))
    return pl.pallas_call(
        _merge_kernel,
        out_shape=jax.ShapeDtypeStruct((s, D_MODEL), F32),
        grid=(s // tm,),
        in_specs=[rows(D_MODEL), rows(D_A), rows(D_A), rows(D_A), rows(LANES), rows(LANES), rows(LANES), rows(D_M),
                  res((D_MODEL, 2 * D_MODEL)), small(2 * D_MODEL), res((D_A, D_MODEL)), res((D_M, D_MODEL)),
                  res((D_MODEL, D_MODEL)), small(D_MODEL), small(D_MODEL)],
        out_specs=rows(D_MODEL),
        scratch_shapes=[pltpu.VMEM((D_MODEL, 2 * D_MODEL), BF16), pltpu.VMEM((D_A, D_MODEL), BF16),
                        pltpu.VMEM((D_M, D_MODEL), BF16), pltpu.VMEM((D_MODEL, D_MODEL), BF16)],
        compiler_params=_params(("arbitrary",), 56),
        name="merge",
    )(x, *outs, *lses, ym, w_gate, b_gate.reshape(1, -1), w_br_a, w_br_m, w_o,
      ln_g.reshape(1, -1), ln_b.reshape(1, -1))


def _token_mixer_layer(x, biases, w_in, w_gate, b_gate, conv_w, conv_b, w_qk_m, w_v_m, w_if, b_if, m_norm_g,
                       w_br_a, w_br_m, w_o, ln_g, ln_b, layer):
    (q, k, v, xm, zm), perm = _in_proj(x, w_in, layer)
    outs, stats = [], []
    for (_, dilation), bias in zip(ATTN_PATTERNS, biases):
        qd, kd, vd = (q, k, v) if dilation == 1 else perm[DILATIONS.index(dilation)]
        o, st = _attn_pattern(qd, kd, vd, bias, dilation)
        outs.append(o)
        stats.append(st)
    qm, km, vm, gc, gr = _mlstm_prep(xm, conv_w, conv_b, w_qk_m, w_v_m, w_if, b_if, layer)
    ym = _mlstm_scan(qm, km, vm, gc, gr, zm, m_norm_g)
    return _merge(x, outs, stats, ym, w_gate, b_gate, w_br_a, w_br_m, w_o, ln_g, ln_b, layer)


def _ffn_kernel(x_ref, w1_ref, w3_ref, w2_ref, lng_ref, lnb_ref, out_ref, xb_ref, acc_ref):
    f = pl.program_id(1)

    @pl.when(f == 0)
    def _start():
        xb_ref[...] = x_ref[...].astype(BF16)
        acc_ref[...] = jnp.zeros_like(acc_ref)

    xb = xb_ref[...]
    a = _dot(xb, w1_ref[...].astype(BF16))
    g = _dot(xb, w3_ref[...].astype(BF16))
    hidden = (a * _sigmoid(a) * g).astype(BF16)
    acc_ref[...] += _dot(hidden, w2_ref[...].astype(BF16))

    @pl.when(f == pl.num_programs(1) - 1)
    def _finish():
        out_ref[...] = _layer_norm(ALPHA * x_ref[...] + acc_ref[...], lng_ref[...], lnb_ref[...])


def _dense_ffn(x, w13, w2, ln_g, ln_b, j):
    s = x.shape[0]
    tm = min(FFN_ROW_TILE, s)
    fc = FFN_FF_CHUNK
    nf = D_FF // fc
    small = pl.BlockSpec((1, D_MODEL), lambda i, f: (0, 0))
    return pl.pallas_call(
        _ffn_kernel,
        out_shape=jax.ShapeDtypeStruct((s, D_MODEL), F32),
        grid=(s // tm, nf),
        in_specs=[pl.BlockSpec((tm, D_MODEL), lambda i, f: (i, 0)),
                  pl.BlockSpec((None, D_MODEL, fc), lambda i, f: (j, 0, f)),
                  pl.BlockSpec((None, D_MODEL, fc), lambda i, f: (j, 0, nf + f)),
                  pl.BlockSpec((None, fc, D_MODEL), lambda i, f: (j, f, 0)),
                  small, small],
        out_specs=pl.BlockSpec((tm, D_MODEL), lambda i, f: (i, 0)),
        scratch_shapes=[pltpu.VMEM((tm, D_MODEL), BF16), pltpu.VMEM((tm, D_MODEL), F32)],
        compiler_params=_params(("arbitrary", "arbitrary"), 48),
        name="dense_ffn",
    )(x, w13, w13, w2, ln_g.reshape(1, -1), ln_b.reshape(1, -1))


def _router_kernel(x_ref, rw_ref, rb_ref, gate_ref, rank_ref, cnt_ref, xb_ref, carry_ref):
    tm = x_ref.shape[0]
    ne = N_EXPERTS

    @pl.when(pl.program_id(0) == 0)
    def _init():
        carry_ref[...] = jnp.zeros_like(carry_ref)

    x = x_ref[...]
    xb_ref[...] = x.astype(BF16)
    xs = _split3(x)
    ws = _split3(rw_ref[...])
    logits = rb_ref[...] + sum(_dot(xs[a], ws[b]) for a, b in ((2, 0), (0, 2), (1, 1), (1, 0), (0, 1), (0, 0)))
    lane = lax.broadcasted_iota(I32, (tm, ne), 1)
    v1 = jnp.max(logits, axis=-1, keepdims=True)
    i1 = jnp.min(jnp.where(logits == v1, lane, ne), axis=-1, keepdims=True)
    rest = jnp.where(lane == i1, -jnp.inf, logits)
    v2 = jnp.max(rest, axis=-1, keepdims=True)
    i2 = jnp.min(jnp.where(rest == v2, lane, ne), axis=-1, keepdims=True)
    e2 = jnp.exp(v2 - v1)
    den = 1.0 + e2
    sel1, sel2 = lane == i1, lane == i2
    gate_ref[...] = jnp.where(sel1, 1.0 / den, 0.0) + jnp.where(sel2, e2 / den, 0.0)
    sel = jnp.where(sel1 | sel2, 1.0, 0.0)
    ri = lax.broadcasted_iota(I32, (tm, tm), 0)
    ci = lax.broadcasted_iota(I32, (tm, tm), 1)
    before = (ri > ci).astype(BF16)
    carry = carry_ref[0:1, 0:ne]
    rank = _dot(before, sel.astype(BF16)) + carry
    rank_ref[...] = jnp.where(sel > 0.0, rank, -1.0)
    total = carry + jnp.sum(sel, axis=0, keepdims=True)
    carry_ref[0:1, 0:ne] = total
    cnt_ref[...] = total


def _router(x, router_w, router_b):
    s = x.shape[0]
    tm = 512
    ne = N_EXPERTS
    return pl.pallas_call(
        _router_kernel,
        out_shape=(jax.ShapeDtypeStruct((s, ne), F32), jax.ShapeDtypeStruct((s, ne), F32),
                   jax.ShapeDtypeStruct((1, ne), F32), jax.ShapeDtypeStruct((s, D_MODEL), BF16)),
        grid=(s // tm,),
        in_specs=[pl.BlockSpec((tm, D_MODEL), lambda i: (i, 0)),
                  pl.BlockSpec((D_MODEL, ne), lambda i: (0, 0)),
                  pl.BlockSpec((1, ne), lambda i: (0, 0))],
        out_specs=(pl.BlockSpec((tm, ne), lambda i: (i, 0)), pl.BlockSpec((tm, ne), lambda i: (i, 0)),
                   pl.BlockSpec((1, ne), lambda i: (0, 0)), pl.BlockSpec((tm, D_MODEL), lambda i: (i, 0))),
        scratch_shapes=[pltpu.VMEM((8, LANES), F32)],
        compiler_params=_params(("arbitrary",), 32),
        name="moe_router",
    )(x, router_w, router_b.reshape(1, ne))


def _chunk_copy(src_hbm, buf_ref, sem_ref, chunk, slot):
    rows = buf_ref.shape[1]
    start = pl.multiple_of(chunk * rows, rows)
    return pltpu.make_async_copy(src_hbm.at[pl.ds(start, rows), :], buf_ref.at[slot], sem_ref.at[slot])


def _moe_gemm_kernel(te_ref, nu_ref, off_ref, ist_ref, clo_ref, xb_hbm, rank_ref, w1_ref, w3_ref, w2_ref,
                     y_ref, xs_ref, acc_ref, buf_ref, sem_ref):
    gr, ck = MOE_GATHER_ROWS, MOE_GATHER_TOKENS
    sub = MOE_GROUP_TILE // gr
    i, f = pl.program_id(0), pl.program_id(1)
    last_f = pl.num_programs(1) - 1

    @pl.when(i < nu_ref[0])
    def _tile():
        @pl.when(f == 0)
        def _gather_rows():
            e = te_ref[i]
            base = i * sub
            k0, k1 = ist_ref[base], ist_ref[base + sub]

            def item(k):
                s = sum((k >= ist_ref[base + j]).astype(I32) for j in range(1, sub))
                return s, clo_ref[base + s] + (k - ist_ref[base + s])

            acc_ref[...] = jnp.zeros_like(acc_ref)
            nbuf = buf_ref.shape[0]

            def start(k):
                @pl.when(k < k1)
                def _():
                    _chunk_copy(xb_hbm, buf_ref, sem_ref, item(k)[1], (k - k0) % nbuf).start()

            for d in range(nbuf - 1):
                start(k0 + d)

            def step(k, carry):
                slot = (k - k0) % nbuf
                s, chunk = item(k)
                _chunk_copy(xb_hbm, buf_ref, sem_ref, chunk, slot).wait()
                start(k + nbuf - 1)

                rank0 = ((base + s) * gr - off_ref[e]).astype(F32)
                row_rank = rank0 + lax.broadcasted_iota(I32, (gr, ck), 0).astype(F32)
                hit = row_rank == rank_ref[e, pl.ds(chunk, 1), :]
                rows = pl.ds(pl.multiple_of(s * gr, gr), gr)
                acc_ref[rows, :] += _dot(jnp.where(hit, 1.0, 0.0).astype(BF16), buf_ref[slot])
                return carry

            lax.fori_loop(k0, k1, step, 0)
            xs_ref[...] = acc_ref[...].astype(BF16)
            acc_ref[...] = jnp.zeros_like(acc_ref)

        xb = xs_ref[...]
        a = _dot(xb, w1_ref[0].astype(BF16))
        g = _dot(xb, w3_ref[0].astype(BF16))
        hidden = (a * _sigmoid(a) * g).astype(BF16)
        acc_ref[...] += _dot(hidden, w2_ref[0].astype(BF16))

        @pl.when(f == last_f)
        def _finish():
            y_ref[...] = acc_ref[...].astype(y_ref.dtype)

    @pl.when(jnp.logical_and(i >= nu_ref[0], f == last_f))
    def _unused_tile():
        y_ref[...] = jnp.zeros_like(y_ref)


def _combine_kernel(off_ref, cn_ref, cie_ref, cic_ref, x_ref, ys_hbm, rank_ref, gate_ref, lng_ref, lnb_ref,
                    out_ref, acc_ref, buf_ref, sem_ref, *, slots):
    tt, cr = MOE_COMBINE_TOKENS, MOE_COMBINE_ROWS
    j = pl.program_id(0)
    n = cn_ref[j]
    base = j * slots
    acc_ref[...] = jnp.zeros_like(acc_ref)
    nbuf = buf_ref.shape[0]

    def start(k):
        @pl.when(k < n)
        def _():
            _chunk_copy(ys_hbm, buf_ref, sem_ref, cic_ref[base + jnp.minimum(k, slots - 1)], k % nbuf).start()

    for d in range(nbuf - 1):
        start(d)

    lane = lax.broadcasted_iota(I32, (tt, N_EXPERTS), 1)

    def step(k, carry):
        slot = k % nbuf
        e, chunk = cie_ref[base + k], cic_ref[base + k]
        _chunk_copy(ys_hbm, buf_ref, sem_ref, chunk, slot).wait()
        start(k + nbuf - 1)

        rank = jnp.max(jnp.where(lane == e, rank_ref[...], -1.0), axis=-1, keepdims=True)
        gate = jnp.sum(jnp.where(lane == e, gate_ref[...], 0.0), axis=-1, keepdims=True)
        pos = jnp.where(rank >= 0.0, rank + off_ref[e].astype(F32), -1.0)
        col = (chunk * cr).astype(F32) + lax.broadcasted_iota(I32, (tt, cr), 1).astype(F32)
        acc_ref[...] += gate * _dot(jnp.where(pos == col, 1.0, 0.0).astype(BF16), buf_ref[slot])
        return carry

    lax.fori_loop(0, n, step, 0)
    out_ref[...] = _layer_norm(ALPHA * x_ref[...] + acc_ref[...], lng_ref[...], lnb_ref[...])


def _moe_layer(x, router_w, router_b, w13, w2, ln_g, ln_b):
    s = x.shape[0]
    ne, gt, gr, ck = N_EXPERTS, MOE_GROUP_TILE, MOE_GATHER_ROWS, MOE_GATHER_TOKENS
    tt, cr = MOE_COMBINE_TOKENS, MOE_COMBINE_ROWS
    sub = gt // gr
    nck = s // ck
    n_group_tiles = (TOP_K * s) // gt + ne
    n_sub = n_group_tiles * sub

    gate, rank, counts, xb = _router(x, router_w, router_b)

    cnt = counts[0].astype(I32)
    padded = ((cnt + gt - 1) // gt) * gt
    off = (jnp.cumsum(padded) - padded).astype(I32)
    n_used = (jnp.sum(padded) // gt).astype(I32)
    tile_expert = jnp.clip(
        jnp.searchsorted(jnp.cumsum(padded), jnp.arange(n_group_tiles, dtype=I32) * gt, side="right"), 0, ne - 1
    ).astype(I32)
    routed = (rank >= 0.0).reshape(nck, ck, ne)
    cum = jnp.cumsum(jnp.sum(routed, axis=1), axis=0).astype(I32).T

    u = jnp.arange(n_sub, dtype=I32)
    e_u = tile_expert[u // sub]
    r0 = u * gr - off[e_u]
    r1 = jnp.minimum(r0 + gr, cnt[e_u]) - 1
    live = (u // sub < n_used) & (r0 < cnt[e_u])
    cum_u = cum[e_u]
    c_lo = jnp.sum(cum_u <= r0[:, None], axis=1).astype(I32)
    c_hi = jnp.sum(cum_u <= r1[:, None], axis=1).astype(I32)
    n_items = jnp.where(live, c_hi - c_lo + 1, 0)
    item_start = jnp.concatenate([jnp.zeros((1,), I32), jnp.cumsum(n_items).astype(I32)])
    c_lo = jnp.where(live, c_lo, 0)

    fc = MOE_FF_CHUNK
    nf = D_FF_E // fc
    tile_of = lambda i, nu: jnp.minimum(i, nu[0] - 1)
    chunk_of = lambda i, f, nu: jnp.where(i < nu[0], f, nf - 1)
    ys = pl.pallas_call(
        _moe_gemm_kernel,
        out_shape=jax.ShapeDtypeStruct((n_group_tiles * gt, D_MODEL), BF16),
        grid_spec=pltpu.PrefetchScalarGridSpec(
            num_scalar_prefetch=5,
            grid=(n_group_tiles, nf),
            in_specs=[pl.BlockSpec(memory_space=pl.ANY),
                      pl.BlockSpec((ne, nck, ck), lambda i, f, te, nu, *_: (0, 0, 0)),
                      pl.BlockSpec((1, D_MODEL, fc),
                                   lambda i, f, te, nu, *_: (te[tile_of(i, nu)], 0, chunk_of(i, f, nu))),
                      pl.BlockSpec((1, D_MODEL, fc),
                                   lambda i, f, te, nu, *_: (te[tile_of(i, nu)], 0, nf + chunk_of(i, f, nu))),
                      pl.BlockSpec((1, fc, D_MODEL),
                                   lambda i, f, te, nu, *_: (te[tile_of(i, nu)], chunk_of(i, f, nu), 0))],
            out_specs=pl.BlockSpec((gt, D_MODEL), lambda i, f, *_: (i, 0)),
            scratch_shapes=[pltpu.VMEM((gt, D_MODEL), BF16), pltpu.VMEM((gt, D_MODEL), F32),
                            pltpu.VMEM((MOE_DMA_DEPTH, ck, D_MODEL), BF16),
                            pltpu.SemaphoreType.DMA((MOE_DMA_DEPTH,))]),
        compiler_params=_params(("arbitrary", "arbitrary"), 48),
        name="moe_gemm",
    )(tile_expert, n_used.reshape(1), off, item_start, c_lo, xb, rank.T.reshape(ne, nck, ck), w13, w13, w2)

    njc = s // tt
    per = tt // ck
    incl = cum[:, per - 1::per]
    lo = off[:, None] + jnp.concatenate([jnp.zeros((ne, 1), I32), incl[:, :-1]], axis=1)
    hi = off[:, None] + incl
    first, last = lo // cr, (hi - 1) // cr
    max_chunks = tt // cr + 1
    kk = jnp.arange(max_chunks, dtype=I32)
    slot_chunk = (first[:, :, None] + kk).transpose(1, 0, 2).reshape(njc, -1)
    slot_live = ((hi > lo)[:, :, None] & (first[:, :, None] + kk <= last[:, :, None])).transpose(1, 0, 2)
    slot_live = slot_live.reshape(njc, -1)
    slots = ne * max_chunks
    slot_expert = jnp.broadcast_to(jnp.repeat(jnp.arange(ne, dtype=I32), max_chunks)[None, :], (njc, slots))
    dest = jnp.cumsum(slot_live, axis=1) - 1
    place = slot_live[:, :, None] & (dest[:, :, None] == jnp.arange(slots)[None, None, :])
    item_chunk = jnp.sum(jnp.where(place, slot_chunk[:, :, None], 0), axis=1).astype(I32).reshape(-1)
    item_expert = jnp.sum(jnp.where(place, slot_expert[:, :, None], 0), axis=1).astype(I32).reshape(-1)
    item_count = jnp.sum(slot_live, axis=1).astype(I32)

    small = pl.BlockSpec((1, D_MODEL), lambda j, *_: (0, 0))
    return pl.pallas_call(
        functools.partial(_combine_kernel, slots=slots),
        out_shape=jax.ShapeDtypeStruct((s, D_MODEL), F32),
        grid_spec=pltpu.PrefetchScalarGridSpec(
            num_scalar_prefetch=4,
            grid=(njc,),
            in_specs=[pl.BlockSpec((tt, D_MODEL), lambda j, *_: (j, 0)),
                      pl.BlockSpec(memory_space=pl.ANY),
                      pl.BlockSpec((tt, ne), lambda j, *_: (j, 0)),
                      pl.BlockSpec((tt, ne), lambda j, *_: (j, 0)),
                      small, small],
            out_specs=pl.BlockSpec((tt, D_MODEL), lambda j, *_: (j, 0)),
            scratch_shapes=[pltpu.VMEM((tt, D_MODEL), F32), pltpu.VMEM((MOE_DMA_DEPTH, cr, D_MODEL), BF16),
                            pltpu.SemaphoreType.DMA((MOE_DMA_DEPTH,))]),
        compiler_params=_params(("arbitrary",), 32),
        name="moe_combine",
    )(off, item_count, item_expert, item_chunk, x, ys, rank, gate, ln_g.reshape(1, -1), ln_b.reshape(1, -1))


def kernel(x, rel_bias, w_in, w_gate, b_gate, conv_w, conv_b, w_qk_m, w_v_m, w_if, b_if, m_norm_g, w_br_a, w_br_m,
           w_o, ln_g, ln_b, ffn_w13, ffn_w2, router_w, router_b, exp_w13, exp_w2):
    batch, seq, _ = x.shape
    assert batch == 1
    h = x.reshape(seq, D_MODEL)
    biases = [_attn_bias(rel_bias, window, dilation) for window, dilation in ATTN_PATTERNS]
    for l in range(DEPTH):
        h = _token_mixer_layer(h, biases, w_in, w_gate, b_gate[l], conv_w[l], conv_b[l], w_qk_m, w_v_m, w_if[l],
                               b_if[l], m_norm_g[l], w_br_a, w_br_m, w_o, ln_g[l, 0], ln_b[l, 0], l)
        j = l // 2
        if l % 2 == 0:
            h = _dense_ffn(h, ffn_w13, ffn_w2, ln_g[l, 1], ln_b[l, 1], j)
        else:
            h = _moe_layer(h, router_w[j], router_b[j], exp_w13[j], exp_w2[j], ln_g[l, 1], ln_b[l, 1])
    return h.reshape(batch, seq, D_MODEL)
```

```python
import functools
import math

import jax
import jax.numpy as jnp
from jax import lax
from jax.experimental import pallas as pl
from jax.experimental.pallas import tpu as pltpu

F32 = jnp.float32
BF16 = jnp.bfloat16
I32 = jnp.int32

D_MODEL = 1024
DEPTH = 2
N_HEADS_A = 8
HEAD_DIM_A = 64
D_A = N_HEADS_A * HEAD_DIM_A
ATTN_PATTERNS = ((128, 1), (512, 4), (2048, 16))
ATTN_BLOCK = 128
NUM_BUCKETS = 32
MAX_DISTANCE = 2048
N_HEADS_M = 4
HEAD_DIM_M = 128
D_M = N_HEADS_M * HEAD_DIM_M
CONV_K = 4
MLSTM_CHUNK = 128
N_PROJ = 5
P_IN = 3 * D_A + 2 * D_M
D_FF = 2816
N_EXPERTS = 8
TOP_K = 2
D_FF_E = 3584
ALPHA = (2.0 * DEPTH) ** 0.25
LN_EPS = 1e-5

NEG = -1e30
LANES = 128
HALO_ROWS = 16
MIB = 1024 * 1024

MOE_GROUP_TILE = 1024
MOE_FF_CHUNK = 512
MOE_GATHER_ROWS = 256
MOE_GATHER_TOKENS = 256
MOE_COMBINE_TOKENS = 512
MOE_COMBINE_ROWS = 256
MOE_DMA_DEPTH = 4
FFN_ROW_TILE = 1024
FFN_FF_CHUNK = 256


def _params(sem, vmem_mib):
    return pltpu.CompilerParams(dimension_semantics=sem, vmem_limit_bytes=vmem_mib * MIB)


def _sigmoid(x):
    return 1.0 / (1.0 + jnp.exp(-x))


def _layer_norm(r, g, b):
    mu = jnp.mean(r, axis=-1, keepdims=True)
    c = r - mu
    var = jnp.mean(c * c, axis=-1, keepdims=True)
    return c * lax.rsqrt(var + LN_EPS) * g + b


def _split3(a):
    hi = a.astype(BF16)
    r1 = a - hi.astype(F32)
    mid = r1.astype(BF16)
    lo = (r1 - mid.astype(F32)).astype(BF16)
    return hi, mid, lo


def _dot(a, b):
    return jnp.dot(a, b, preferred_element_type=F32)


def _dot_nt(a, b):
    return lax.dot_general(a, b, (((1,), (1,)), ((), ())), preferred_element_type=F32)


def _dot_tn(a, b):
    return lax.dot_general(a, b, (((0,), (0,)), ((), ())), preferred_element_type=F32)


IN_PROJ_ROWS = 1024
DILATIONS = tuple(d for _, d in ATTN_PATTERNS if d > 1)


def _in_proj_kernel(x_ref, w_ref, *refs):
    nat = refs[:N_PROJ]
    perm = refs[N_PROJ:N_PROJ + 3 * len(DILATIONS)]
    wb_ref, y_ref = refs[-2:]

    @pl.when(pl.program_id(0) == 0)
    def _cast_weights():
        wb_ref[...] = w_ref[...].astype(BF16)

    xb = x_ref[...].astype(BF16)
    for j in range(N_PROJ):
        y = _dot(xb, wb_ref[:, j * D_A:(j + 1) * D_A])
        if j == 0:
            y = y * (HEAD_DIM_A ** -0.5)
        nat[j][...] = y.astype(BF16)
        if j >= 3:
            continue
        for c in range(D_A // LANES):
            y_ref[c] = y[:, c * LANES:(c + 1) * LANES]
        for di, d in enumerate(DILATIONS):
            out = perm[di * 3 + j]
            tiles, _, rpc, _ = out.shape
            for t in range(tiles):
                for r in range(d):
                    for c in range(D_A // LANES):
                        out[t, r, :, c * LANES:(c + 1) * LANES] = (
                            y_ref[c, pl.ds(t * d * rpc + r, rpc, stride=d), :].astype(BF16))


def _in_proj(x, w_in, layer):
    s = x.shape[0]
    tm = IN_PROJ_ROWS
    blk = ATTN_BLOCK
    out_shape = [jax.ShapeDtypeStruct((s, D_A), BF16)] * N_PROJ
    out_specs = [pl.BlockSpec((tm, D_A), lambda i: (i, 0))] * N_PROJ
    for d in DILATIONS:
        tile = d * blk
        if tile <= tm:
            spec = pl.BlockSpec((tm // tile, d, blk, D_A), lambda i: (i, 0, 0, 0))
        else:
            parts = tile // tm
            spec = pl.BlockSpec((1, d, blk // parts, D_A), lambda i, parts=parts: (i // parts, 0, i % parts, 0))
        out_shape += [jax.ShapeDtypeStruct((s // tile, d, blk, D_A), BF16)] * 3
        out_specs += [spec] * 3
    outs = pl.pallas_call(
        _in_proj_kernel,
        out_shape=tuple(out_shape),
        grid=(s // tm,),
        in_specs=[
            pl.BlockSpec((tm, D_MODEL), lambda i: (i, 0)),
            pl.BlockSpec((None, D_MODEL, P_IN), lambda i: (layer, 0, 0), pipeline_mode=pl.Buffered(1)),
        ],
        out_specs=tuple(out_specs),
        scratch_shapes=[pltpu.VMEM((D_MODEL, P_IN), BF16), pltpu.VMEM((D_A // LANES, tm, LANES), F32)],
        compiler_params=_params(("arbitrary",), 56),
        name="in_proj",
    )(x, w_in)
    nat = outs[:N_PROJ]
    perm = [tuple(t.reshape(s, D_A) for t in outs[N_PROJ + 3 * i:N_PROJ + 3 * i + 3]) for i in range(len(DILATIONS))]
    return nat, perm


ATTN_STEP_BLOCKS = 16


def _rel_bucket(dist):
    exact = NUM_BUCKETS // 2
    d = jnp.maximum(dist, exact).astype(F32)
    log_b = exact + (jnp.log(d / exact) / math.log(MAX_DISTANCE / exact) * (NUM_BUCKETS - exact)).astype(I32)
    return jnp.where(dist < exact, dist, jnp.minimum(log_b, NUM_BUCKETS - 1))


def _attn_bias(rel_bias, window, dilation):
    blk = ATTN_BLOCK
    qi = jnp.arange(blk)[:, None]
    kj = jnp.arange(2 * blk)[None, :]
    rel = qi + blk - kj
    bucket = _rel_bucket(jnp.maximum(rel, 0) * dilation)
    onehot = (bucket[..., None] == jnp.arange(NUM_BUCKETS)).astype(F32)
    bias = jnp.einsum("qkb,bh->hqk", onehot, rel_bias.astype(F32), precision=lax.Precision.HIGHEST)
    mask = (rel >= 0) & (rel <= window // dilation)
    return jnp.where(mask[None], bias, NEG)


def _attn_kernel(q_ref, kp_ref, kc_ref, vp_ref, vc_ref, bias_ref, o_ref, st_ref, on_ref, sn_ref, *, dilation):
    blk = ATTN_BLOCK
    nb = ATTN_STEP_BLOCKS
    step = pl.program_id(0)
    kj = lax.broadcasted_iota(I32, (2 * blk, 2 * blk), 1)
    lane = lax.broadcasted_iota(I32, (blk, LANES), 1)
    ones = jnp.ones((2 * blk, LANES), BF16)
    pairs = range(N_HEADS_A // 2)

    def block(g, carry):
        cur = pl.multiple_of(g * blk, blk)
        in_tile = g >= dilation
        prev_c = pl.multiple_of(jnp.maximum(g - dilation, 0) * blk, blk)
        prev_p = pl.multiple_of(jnp.minimum(nb + g - dilation, nb - 1) * blk, blk)
        pen = jnp.where(jnp.logical_and(step == 0, g < dilation), NEG, 0.0).astype(F32)
        prev_pen = jnp.where(kj < blk, pen, 0.0)
        tile_i, cls = g // dilation, g % dilation
        dst = pl.ds(tile_i * (blk * dilation) + cls, blk, stride=dilation)

        def keys(cur_ref, prev_ref, ls):
            prev = jnp.where(in_tile, cur_ref[pl.ds(prev_c, blk), ls], prev_ref[pl.ds(prev_p, blk), ls])
            return jnp.concatenate([prev, cur_ref[pl.ds(cur, blk), ls]], axis=0)

        even = (lane // HEAD_DIM_A) == 0
        lss = [slice(hp * LANES, (hp + 1) * LANES) for hp in pairs]
        qb = [q_ref[pl.ds(cur, blk), lss[hp]] for hp in pairs]
        qcat = [jnp.concatenate([jnp.where(even, qb[hp], jnp.zeros_like(qb[hp])),
                                 jnp.where(even, jnp.zeros_like(qb[hp]), qb[hp])], axis=0) for hp in pairs]
        kb = [keys(kc_ref, kp_ref, lss[hp]) for hp in pairs]
        v1 = [jnp.concatenate([keys(vc_ref, vp_ref, lss[hp]), ones], axis=1) for hp in pairs]
        logits = [_dot_nt(qcat[hp], kb[hp]) + bias_ref[hp] + prev_pen for hp in pairs]
        m = [jnp.max(logits[hp], axis=-1, keepdims=True) for hp in pairs]
        p = [jnp.exp(logits[hp] - m[hp]).astype(BF16) for hp in pairs]
        ol = [_dot(p[hp], v1[hp]) for hp in pairs]
        stats = jnp.zeros((blk, LANES), F32)
        for hp in pairs:
            for par in range(2):
                h, rs = 2 * hp + par, slice(par * blk, (par + 1) * blk)
                stats = jnp.where(lane == h, m[hp][rs], stats)
                stats = jnp.where(lane == N_HEADS_A + h, ol[hp][rs, LANES:], stats)
            on_ref[hp, dst, :] = jnp.where(even, ol[hp][:blk, :LANES], ol[hp][blk:, :LANES])
        sn_ref[dst, :] = stats
        return carry

    lax.fori_loop(0, nb, block, 0, unroll=2)
    for hp in range(N_HEADS_A // 2):
        o_ref[:, hp * LANES:(hp + 1) * LANES] = on_ref[hp].astype(o_ref.dtype)
    st_ref[...] = sn_ref[...]


def _attn_pattern(q, k, v, bias, dilation):
    s = q.shape[0]
    rows = ATTN_STEP_BLOCKS * ATTN_BLOCK
    cur = pl.BlockSpec((rows, D_A), lambda i: (i, 0))
    prev = pl.BlockSpec((rows, D_A), lambda i: (jnp.maximum(i - 1, 0), 0))
    return pl.pallas_call(
        functools.partial(_attn_kernel, dilation=dilation),
        out_shape=(jax.ShapeDtypeStruct((s, D_A), BF16), jax.ShapeDtypeStruct((s, LANES), F32)),
        grid=(s // rows,),
        in_specs=[cur, prev, cur, prev, cur,
                  pl.BlockSpec((N_HEADS_A // 2, 2 * ATTN_BLOCK, 2 * ATTN_BLOCK), lambda i: (0, 0, 0))],
        out_specs=(pl.BlockSpec((rows, D_A), lambda i: (i, 0)), pl.BlockSpec((rows, LANES), lambda i: (i, 0))),
        scratch_shapes=[pltpu.VMEM((D_A // LANES, rows, LANES), F32), pltpu.VMEM((rows, LANES), F32)],
        compiler_params=_params(("arbitrary",), 48),
        name=f"attn_d{dilation}",
    )(q, k, k, v, v, bias.reshape(N_HEADS_A // 2, 2 * ATTN_BLOCK, 2 * ATTN_BLOCK))


def _log_sigmoid(x):
    return jnp.minimum(x, 0.0) - jnp.log(1.0 + jnp.exp(-jnp.abs(x)))


def _mlstm_prep_kernel(xm_ref, halo_ref, cw_ref, cb_ref, wqk_ref, wkt_ref, wv_ref, wif_ref, wift_ref, bif_ref,
                       bift_ref, q_ref, k_ref, kt_ref, v_ref, gc_ref, gr_ref):
    tm = xm_ref.shape[0]
    hm, hd, lc = N_HEADS_M, HEAD_DIM_M, MLSTM_CHUNK
    xmb = xm_ref[...]
    halo = jnp.where(pl.program_id(0) == 0, 0.0, halo_ref[...].astype(F32))
    xx = jnp.concatenate([halo, xmb.astype(F32)], axis=0)
    conv = jnp.zeros((tm, D_M), F32) + cb_ref[...]
    for j in range(CONV_K):
        start = HALO_ROWS - (CONV_K - 1) + j
        conv = conv + cw_ref[j:j + 1, :] * xx[start:start + tm, :]
    xcb = (conv * _sigmoid(conv)).astype(BF16)

    qs, ks, vs = [], [], []
    for h in range(hm):
        sl = slice(h * hd, (h + 1) * hd)
        qs.append(_dot(xcb[:, sl], wqk_ref[0, h].astype(BF16)))
        ks.append(_dot(xcb[:, sl], wqk_ref[1, h].astype(BF16)))
        vs.append(_dot(xmb[:, sl], wv_ref[h].astype(BF16)))
        kt = _dot_nt(wkt_ref[h].astype(BF16), xcb[:, sl])
        kt_ref[sl, :] = (kt * (hd ** -0.5)).astype(BF16)
    q = jnp.concatenate(qs, axis=1)
    k = jnp.concatenate(ks, axis=1)
    v = jnp.concatenate(vs, axis=1)
    q_ref[...] = q.astype(BF16)
    k_ref[...] = (k * (hd ** -0.5)).astype(BF16)
    v_ref[...] = v.astype(BF16)

    qkv = jnp.concatenate([q, k, v], axis=1).astype(BF16)
    gates_c = _dot(qkv, wif_ref[...].astype(BF16)) + bif_ref[...]
    gates_r = _dot_nt(wift_ref[...].astype(BF16), qkv) + bift_ref[...]
    lane = lax.broadcasted_iota(I32, gates_c.shape, 1)
    row = lax.broadcasted_iota(I32, gates_r.shape, 0)
    gc_ref[...] = jnp.where(lane < hm, gates_c, _log_sigmoid(gates_c))
    gr_ref[...] = jnp.where(row < hm, gates_r, _log_sigmoid(gates_r))

    ri = lax.broadcasted_iota(I32, (lc, lc), 0)
    ci = lax.broadcasted_iota(I32, (lc, lc), 1)
    lower = (ri >= ci).astype(BF16)
    upper = (ri <= ci).astype(BF16)
    lane_c = lax.broadcasted_iota(I32, (lc, 2 * hm), 1)
    row_c = lax.broadcasted_iota(I32, (2 * hm, lc), 0)
    for c in range(tm // lc):
        rs = slice(c * lc, (c + 1) * lc)
        gcc = gc_ref[rs, :]
        grc = gr_ref[:, rs]
        cum_c = sum(_dot(lower, part) for part in _split3(gcc))
        cum_r = sum(_dot(part, upper) for part in _split3(grc))
        gc_ref[rs, :] = jnp.where(lane_c < hm, gcc, cum_c)
        gr_ref[:, rs] = jnp.where(row_c < hm, grc, cum_r)


def _mlstm_prep(xm, conv_w, conv_b, w_qk_m, w_v_m, w_if, b_if, layer):
    s = xm.shape[0]
    tm = 512
    hpt = tm // HALO_ROWS
    full = lambda shape: pl.BlockSpec(shape, lambda i: (0,) * len(shape))
    stacked = lambda shape: pl.BlockSpec((None,) + shape, lambda i: (layer,) + (0,) * len(shape))
    row_spec = pl.BlockSpec((tm, D_M), lambda i: (i, 0))
    head_sq = (N_HEADS_M, HEAD_DIM_M, HEAD_DIM_M)
    wk_t = jnp.swapaxes(w_qk_m[layer, 1], -1, -2)
    return pl.pallas_call(
        _mlstm_prep_kernel,
        out_shape=(jax.ShapeDtypeStruct((s, D_M), BF16), jax.ShapeDtypeStruct((s, D_M), BF16),
                   jax.ShapeDtypeStruct((D_M, s), BF16), jax.ShapeDtypeStruct((s, D_M), BF16),
                   jax.ShapeDtypeStruct((s, 2 * N_HEADS_M), F32), jax.ShapeDtypeStruct((2 * N_HEADS_M, s), F32)),
        grid=(s // tm,),
        in_specs=[
            row_spec,
            pl.BlockSpec((HALO_ROWS, D_M), lambda i: (jnp.maximum(i * hpt - 1, 0), 0)),
            full((CONV_K, D_M)), full((1, D_M)),
            stacked((2,) + head_sq), full(head_sq), stacked(head_sq),
            full((3 * D_M, 2 * N_HEADS_M)), full((2 * N_HEADS_M, 3 * D_M)),
            full((1, 2 * N_HEADS_M)), full((2 * N_HEADS_M, 1)),
        ],
        out_specs=(row_spec, row_spec, pl.BlockSpec((D_M, tm), lambda i: (0, i)), row_spec,
                   pl.BlockSpec((tm, 2 * N_HEADS_M), lambda i: (i, 0)),
                   pl.BlockSpec((2 * N_HEADS_M, tm), lambda i: (0, i))),
        compiler_params=_params(("arbitrary",), 32),
        name="mlstm_prep",
    )(xm, xm, conv_w, conv_b.reshape(1, D_M), w_qk_m, wk_t, w_v_m, w_if, w_if.T,
      b_if.reshape(1, -1), b_if.reshape(-1, 1))


MLSTM_STEP_CHUNKS = 4


def _mlstm_scan_kernel(q_ref, k_ref, kt_ref, v_ref, gc_ref, gr_ref, z_ref, g_ref, y_ref, c_ref, m_ref, *, chunks):
    hm, hd, lc = N_HEADS_M, HEAD_DIM_M, MLSTM_CHUNK

    @pl.when(pl.program_id(0) == 0)
    def _init():
        c_ref[...] = jnp.zeros_like(c_ref)
        m_ref[...] = jnp.zeros_like(m_ref)

    ri = lax.broadcasted_iota(I32, (lc, lc), 0)
    ci = lax.broadcasted_iota(I32, (lc, lc), 1)
    causal = ri >= ci
    ones = jnp.ones((lc, hd), BF16)
    heads = range(hm)
    hsl = [slice(h * hd, (h + 1) * hd) for h in heads]
    m_state = [m_ref[h:h + 1, 0:1] for h in heads]
    for c in range(chunks):
        rs = slice(c * lc, (c + 1) * lc)
        gc = gc_ref[rs, :]
        gr = gr_ref[:, rs]
        qs = [q_ref[rs, hsl[h]] for h in heads]
        v1 = [jnp.concatenate([v_ref[rs, hsl[h]], ones], axis=1) for h in heads]
        i_row = [gr[h:h + 1, :] for h in heads]
        b_row = [gr[hm + h:hm + h + 1, :] for h in heads]
        b_col = [gc[:, hm + h:hm + h + 1] for h in heads]
        qk = [_dot_nt(qs[h], k_ref[rs, hsl[h]]) for h in heads]
        c_prev = [c_ref[h] for h in heads]
        qc = [_dot(qs[h], c_prev[h].astype(BF16)) for h in heads]
        dm = [jnp.where(causal, b_col[h] - b_row[h] + i_row[h], NEG) for h in heads]
        inter = [b_col[h] + m_state[h] for h in heads]
        m_loc = [jnp.maximum(inter[h], jnp.max(dm[h], axis=-1, keepdims=True)) for h in heads]
        sc = [(qk[h] * jnp.exp(dm[h] - m_loc[h])).astype(BF16) for h in heads]
        both = [jnp.exp(inter[h] - m_loc[h]) * qc[h] + _dot(sc[h], v1[h]) for h in heads]
        hval = [both[h][:, :hd] / jnp.maximum(jnp.abs(both[h][:, hd:]), jnp.exp(-m_loc[h])) for h in heads]

        b_last = [b_row[h][:, lc - 1:lc] for h in heads]
        g_row = [b_last[h] - b_row[h] + i_row[h] for h in heads]
        m_new = [jnp.maximum(b_last[h] + m_state[h], jnp.max(g_row[h], axis=-1, keepdims=True)) for h in heads]
        ktw = [(kt_ref[hsl[h], rs].astype(F32) * jnp.exp(g_row[h] - m_new[h])).astype(BF16) for h in heads]
        for h in heads:
            c_ref[h] = jnp.exp(b_last[h] + m_state[h] - m_new[h]) * c_prev[h] + _dot(ktw[h], v1[h])
        m_state = m_new

        mu = [jnp.mean(hval[h], axis=-1, keepdims=True) for h in heads]
        cen = [hval[h] - mu[h] for h in heads]
        var = [jnp.mean(cen[h] * cen[h], axis=-1, keepdims=True) for h in heads]
        for h in heads:
            hn = cen[h] * lax.rsqrt(var[h] + LN_EPS) * g_ref[:, hsl[h]]
            y_ref[rs, hsl[h]] = (_sigmoid(z_ref[rs, hsl[h]].astype(F32)) * hn).astype(y_ref.dtype)
    for h in heads:
        m_ref[h:h + 1, :] = jnp.broadcast_to(m_state[h], (1, LANES))


def _mlstm_scan(q, k, kt, v, gc, gr, z, m_norm_g):
    s = q.shape[0]
    chunks = MLSTM_STEP_CHUNKS
    tm = chunks * MLSTM_CHUNK
    row_spec = pl.BlockSpec((tm, D_M), lambda i: (i, 0))
    return pl.pallas_call(
        functools.partial(_mlstm_scan_kernel, chunks=chunks),
        out_shape=jax.ShapeDtypeStruct((s, D_M), BF16),
        grid=(s // tm,),
        in_specs=[row_spec, row_spec, pl.BlockSpec((D_M, tm), lambda i: (0, i)), row_spec,
                  pl.BlockSpec((tm, 2 * N_HEADS_M), lambda i: (i, 0)),
                  pl.BlockSpec((2 * N_HEADS_M, tm), lambda i: (0, i)),
                  row_spec,
                  pl.BlockSpec((1, D_M), lambda i: (0, 0))],
        out_specs=row_spec,
        scratch_shapes=[pltpu.VMEM((N_HEADS_M, HEAD_DIM_M, 2 * HEAD_DIM_M), F32),
                        pltpu.VMEM((8, LANES), F32)],
        compiler_params=_params(("arbitrary",), 32),
        name="mlstm_scan",
    )(q, k, kt, v, gc, gr, z, m_norm_g.reshape(1, D_M))


def _merge_kernel(x_ref, o1_ref, o2_ref, o3_ref, l1_ref, l2_ref, l3_ref, ym_ref,
                  wg_ref, bg_ref, wa_ref, wm_ref, wo_ref, lng_ref, lnb_ref, out_ref,
                  wgb_ref, wab_ref, wmb_ref, wob_ref):
    @pl.when(pl.program_id(0) == 0)
    def _cast_weights():
        wgb_ref[...] = wg_ref[...].astype(BF16)
        wab_ref[...] = wa_ref[...].astype(BF16)
        wmb_ref[...] = wm_ref[...].astype(BF16)
        wob_ref[...] = wo_ref[...].astype(BF16)

    x = x_ref[...]
    xb = x.astype(BF16)
    stats = (l1_ref[...], l2_ref[...], l3_ref[...])
    mx = jnp.maximum(jnp.maximum(stats[0], stats[1]), stats[2])
    es = [jnp.exp(st - mx) for st in stats]
    ls = [pltpu.roll(st, LANES - N_HEADS_A, 1) for st in stats]
    den = es[0] * ls[0] + es[1] * ls[1] + es[2] * ls[2]
    head_lane = lax.broadcasted_iota(I32, den.shape, 1) < N_HEADS_A
    inv = jnp.where(head_lane, 1.0 / den, 0.0)
    hrow = lax.broadcasted_iota(I32, (LANES, D_A), 0)
    hcol = lax.broadcasted_iota(I32, (LANES, D_A), 1) // HEAD_DIM_A
    expand = (hrow == hcol).astype(BF16)
    ya = jnp.zeros((x.shape[0], D_A), F32)
    for e, o_ref in zip(es, (o1_ref, o2_ref, o3_ref)):
        w = sum(_dot(part, expand) for part in _split3(e * inv))
        ya = ya + w * o_ref[...].astype(F32)

    gate = _sigmoid(_dot(xb, wgb_ref[...]) + bg_ref[...])
    merged = (gate[:, :D_MODEL] * _dot(ya.astype(BF16), wab_ref[...])
              + gate[:, D_MODEL:] * _dot(ym_ref[...], wmb_ref[...]))
    y = _dot(merged.astype(BF16), wob_ref[...])
    out_ref[...] = _layer_norm(ALPHA * x + y, lng_ref[...], lnb_ref[...])


def _merge(x, outs, lses, ym, w_gate, b_gate, w_br_a, w_br_m, w_o, ln_g, ln_b, layer):
    s = x.shape[0]
    tm = 512
    res = lambda shape: pl.BlockSpec((None,) + shape, lambda i: (layer, 0, 0), pipeline_mode=pl.Buffered(1))
    small = lambda n: pl.BlockSpec((1, n), lambda i: (0, 0))
    rows = lambda n: pl.BlockSpec((tm, n), lambda i: (i, 0))
    return pl.pallas_call(
        _merge_kernel,
        out_shape=jax.ShapeDtypeStruct((s, D_MODEL), F32),
        grid=(s // tm,),
        in_specs=[rows(D_MODEL), rows(D_A), rows(D_A), rows(D_A), rows(LANES), rows(LANES), rows(LANES), rows(D_M),
                  res((D_MODEL, 2 * D_MODEL)), small(2 * D_MODEL), res((D_A, D_MODEL)), res((D_M, D_MODEL)),
                  res((D_MODEL, D_MODEL)), small(D_MODEL), small(D_MODEL)],
        out_specs=rows(D_MODEL),
        scratch_shapes=[pltpu.VMEM((D_MODEL, 2 * D_MODEL), BF16), pltpu.VMEM((D_A, D_MODEL), BF16),
                        pltpu.VMEM((D_M, D_MODEL), BF16), pltpu.VMEM((D_MODEL, D_MODEL), BF16)],
        compiler_params=_params(("arbitrary",), 56),
        name="merge",
    )(x, *outs, *lses, ym, w_gate, b_gate.reshape(1, -1), w_br_a, w_br_m, w_o,
      ln_g.reshape(1, -1), ln_b.reshape(1, -1))


def _token_mixer_layer(x, biases, w_in, w_gate, b_gate, conv_w, conv_b, w_qk_m, w_v_m, w_if, b_if, m_norm_g,
                       w_br_a, w_br_m, w_o, ln_g, ln_b, layer):
    (q, k, v, xm, zm), perm = _in_proj(x, w_in, layer)
    outs, stats = [], []
    for (_, dilation), bias in zip(ATTN_PATTERNS, biases):
        qd, kd, vd = (q, k, v) if dilation == 1 else perm[DILATIONS.index(dilation)]
        o, st = _attn_pattern(qd, kd, vd, bias, dilation)
        outs.append(o)
        stats.append(st)
    qm, km, ktm, vm, gc, gr = _mlstm_prep(xm, conv_w, conv_b, w_qk_m, w_v_m, w_if, b_if, layer)
    ym = _mlstm_scan(qm, km, ktm, vm, gc, gr, zm, m_norm_g)
    return _merge(x, outs, stats, ym, w_gate, b_gate, w_br_a, w_br_m, w_o, ln_g, ln_b, layer)


def _ffn_kernel(x_ref, w1_ref, w3_ref, w2_ref, lng_ref, lnb_ref, out_ref, xb_ref, acc_ref):
    f = pl.program_id(1)

    @pl.when(f == 0)
    def _start():
        xb_ref[...] = x_ref[...].astype(BF16)
        acc_ref[...] = jnp.zeros_like(acc_ref)

    xb = xb_ref[...]
    a = _dot(xb, w1_ref[...].astype(BF16))
    g = _dot(xb, w3_ref[...].astype(BF16))
    hidden = (a * _sigmoid(a) * g).astype(BF16)
    acc_ref[...] += _dot(hidden, w2_ref[...].astype(BF16))

    @pl.when(f == pl.num_programs(1) - 1)
    def _finish():
        out_ref[...] = _layer_norm(ALPHA * x_ref[...] + acc_ref[...], lng_ref[...], lnb_ref[...])


def _dense_ffn(x, w13, w2, ln_g, ln_b, j):
    s = x.shape[0]
    tm = min(FFN_ROW_TILE, s)
    fc = FFN_FF_CHUNK
    nf = D_FF // fc
    small = pl.BlockSpec((1, D_MODEL), lambda i, f: (0, 0))
    return pl.pallas_call(
        _ffn_kernel,
        out_shape=jax.ShapeDtypeStruct((s, D_MODEL), F32),
        grid=(s // tm, nf),
        in_specs=[pl.BlockSpec((tm, D_MODEL), lambda i, f: (i, 0)),
                  pl.BlockSpec((None, D_MODEL, fc), lambda i, f: (j, 0, f)),
                  pl.BlockSpec((None, D_MODEL, fc), lambda i, f: (j, 0, nf + f)),
                  pl.BlockSpec((None, fc, D_MODEL), lambda i, f: (j, f, 0)),
                  small, small],
        out_specs=pl.BlockSpec((tm, D_MODEL), lambda i, f: (i, 0)),
        scratch_shapes=[pltpu.VMEM((tm, D_MODEL), BF16), pltpu.VMEM((tm, D_MODEL), F32)],
        compiler_params=_params(("arbitrary", "arbitrary"), 48),
        name="dense_ffn",
    )(x, w13, w13, w2, ln_g.reshape(1, -1), ln_b.reshape(1, -1))


def _router_kernel(x_ref, rw_ref, rb_ref, gate_ref, rank_ref, cnt_ref, xb_ref, carry_ref):
    tm = x_ref.shape[0]
    ne = N_EXPERTS

    @pl.when(pl.program_id(0) == 0)
    def _init():
        carry_ref[...] = jnp.zeros_like(carry_ref)

    x = x_ref[...]
    xb_ref[...] = x.astype(BF16)
    xs = _split3(x)
    ws = _split3(rw_ref[...])
    logits = rb_ref[...] + sum(_dot(xs[a], ws[b]) for a, b in ((2, 0), (0, 2), (1, 1), (1, 0), (0, 1), (0, 0)))
    lane = lax.broadcasted_iota(I32, (tm, ne), 1)
    v1 = jnp.max(logits, axis=-1, keepdims=True)
    i1 = jnp.min(jnp.where(logits == v1, lane, ne), axis=-1, keepdims=True)
    rest = jnp.where(lane == i1, -jnp.inf, logits)
    v2 = jnp.max(rest, axis=-1, keepdims=True)
    i2 = jnp.min(jnp.where(rest == v2, lane, ne), axis=-1, keepdims=True)
    e2 = jnp.exp(v2 - v1)
    den = 1.0 + e2
    sel1, sel2 = lane == i1, lane == i2
    gate_ref[...] = jnp.where(sel1, 1.0 / den, 0.0) + jnp.where(sel2, e2 / den, 0.0)
    sel = jnp.where(sel1 | sel2, 1.0, 0.0)
    ri = lax.broadcasted_iota(I32, (tm, tm), 0)
    ci = lax.broadcasted_iota(I32, (tm, tm), 1)
    before = (ri > ci).astype(BF16)
    carry = carry_ref[0:1, 0:ne]
    rank = _dot(before, sel.astype(BF16)) + carry
    rank_ref[...] = jnp.where(sel > 0.0, rank, -1.0)
    total = carry + jnp.sum(sel, axis=0, keepdims=True)
    carry_ref[0:1, 0:ne] = total
    cnt_ref[...] = total


def _router(x, router_w, router_b):
    s = x.shape[0]
    tm = 512
    ne = N_EXPERTS
    return pl.pallas_call(
        _router_kernel,
        out_shape=(jax.ShapeDtypeStruct((s, ne), F32), jax.ShapeDtypeStruct((s, ne), F32),
                   jax.ShapeDtypeStruct((1, ne), F32), jax.ShapeDtypeStruct((s, D_MODEL), BF16)),
        grid=(s // tm,),
        in_specs=[pl.BlockSpec((tm, D_MODEL), lambda i: (i, 0)),
                  pl.BlockSpec((D_MODEL, ne), lambda i: (0, 0)),
                  pl.BlockSpec((1, ne), lambda i: (0, 0))],
        out_specs=(pl.BlockSpec((tm, ne), lambda i: (i, 0)), pl.BlockSpec((tm, ne), lambda i: (i, 0)),
                   pl.BlockSpec((1, ne), lambda i: (0, 0)), pl.BlockSpec((tm, D_MODEL), lambda i: (i, 0))),
        scratch_shapes=[pltpu.VMEM((8, LANES), F32)],
        compiler_params=_params(("arbitrary",), 32),
        name="moe_router",
    )(x, router_w, router_b.reshape(1, ne))


def _chunk_copy(src_hbm, buf_ref, sem_ref, chunk, slot):
    rows = buf_ref.shape[1]
    start = pl.multiple_of(chunk * rows, rows)
    return pltpu.make_async_copy(src_hbm.at[pl.ds(start, rows), :], buf_ref.at[slot], sem_ref.at[slot])


def _moe_gemm_kernel(te_ref, nu_ref, off_ref, ist_ref, clo_ref, xb_hbm, rank_ref, w1_ref, w3_ref, w2_ref,
                     y_ref, xs_ref, acc_ref, buf_ref, sem_ref):
    gr, ck = MOE_GATHER_ROWS, MOE_GATHER_TOKENS
    sub = MOE_GROUP_TILE // gr
    i, f = pl.program_id(0), pl.program_id(1)
    last_f = pl.num_programs(1) - 1

    @pl.when(i < nu_ref[0])
    def _tile():
        @pl.when(f == 0)
        def _gather_rows():
            e = te_ref[i]
            base = i * sub
            k0, k1 = ist_ref[base], ist_ref[base + sub]

            def item(k):
                s = sum((k >= ist_ref[base + j]).astype(I32) for j in range(1, sub))
                return s, clo_ref[base + s] + (k - ist_ref[base + s])

            acc_ref[...] = jnp.zeros_like(acc_ref)
            nbuf = buf_ref.shape[0]

            def start(k):
                @pl.when(k < k1)
                def _():
                    _chunk_copy(xb_hbm, buf_ref, sem_ref, item(k)[1], (k - k0) % nbuf).start()

            for d in range(nbuf - 1):
                start(k0 + d)

            def step(k, carry):
                slot = (k - k0) % nbuf
                s, chunk = item(k)
                _chunk_copy(xb_hbm, buf_ref, sem_ref, chunk, slot).wait()
                start(k + nbuf - 1)

                rank0 = ((base + s) * gr - off_ref[e]).astype(F32)
                row_rank = rank0 + lax.broadcasted_iota(I32, (gr, ck), 0).astype(F32)
                hit = row_rank == rank_ref[e, pl.ds(chunk, 1), :]
                rows = pl.ds(pl.multiple_of(s * gr, gr), gr)
                acc_ref[rows, :] += _dot(jnp.where(hit, 1.0, 0.0).astype(BF16), buf_ref[slot])
                return carry

            lax.fori_loop(k0, k1, step, 0)
            xs_ref[...] = acc_ref[...].astype(BF16)
            acc_ref[...] = jnp.zeros_like(acc_ref)

        xb = xs_ref[...]
        a = _dot(xb, w1_ref[0].astype(BF16))
        g = _dot(xb, w3_ref[0].astype(BF16))
        hidden = (a * _sigmoid(a) * g).astype(BF16)
        acc_ref[...] += _dot(hidden, w2_ref[0].astype(BF16))

        @pl.when(f == last_f)
        def _finish():
            y_ref[...] = acc_ref[...].astype(y_ref.dtype)

    @pl.when(jnp.logical_and(i >= nu_ref[0], f == last_f))
    def _unused_tile():
        y_ref[...] = jnp.zeros_like(y_ref)


def _combine_kernel(off_ref, cn_ref, cie_ref, cic_ref, x_ref, ys_hbm, rank_ref, gate_ref, lng_ref, lnb_ref,
                    out_ref, acc_ref, buf_ref, sem_ref, *, slots):
    tt, cr = MOE_COMBINE_TOKENS, MOE_COMBINE_ROWS
    j = pl.program_id(0)
    n = cn_ref[j]
    base = j * slots
    acc_ref[...] = jnp.zeros_like(acc_ref)
    nbuf = buf_ref.shape[0]

    def start(k):
        @pl.when(k < n)
        def _():
            _chunk_copy(ys_hbm, buf_ref, sem_ref, cic_ref[base + jnp.minimum(k, slots - 1)], k % nbuf).start()

    for d in range(nbuf - 1):
        start(d)

    lane = lax.broadcasted_iota(I32, (tt, N_EXPERTS), 1)

    def step(k, carry):
        slot = k % nbuf
        e, chunk = cie_ref[base + k], cic_ref[base + k]
        _chunk_copy(ys_hbm, buf_ref, sem_ref, chunk, slot).wait()
        start(k + nbuf - 1)

        rank = jnp.max(jnp.where(lane == e, rank_ref[...], -1.0), axis=-1, keepdims=True)
        gate = jnp.sum(jnp.where(lane == e, gate_ref[...], 0.0), axis=-1, keepdims=True)
        pos = jnp.where(rank >= 0.0, rank + off_ref[e].astype(F32), -1.0)
        col = (chunk * cr).astype(F32) + lax.broadcasted_iota(I32, (tt, cr), 1).astype(F32)
        acc_ref[...] += gate * _dot(jnp.where(pos == col, 1.0, 0.0).astype(BF16), buf_ref[slot])
        return carry

    lax.fori_loop(0, n, step, 0)
    out_ref[...] = _layer_norm(ALPHA * x_ref[...] + acc_ref[...], lng_ref[...], lnb_ref[...])


def _moe_layer(x, router_w, router_b, w13, w2, ln_g, ln_b):
    s = x.shape[0]
    ne, gt, gr, ck = N_EXPERTS, MOE_GROUP_TILE, MOE_GATHER_ROWS, MOE_GATHER_TOKENS
    tt, cr = MOE_COMBINE_TOKENS, MOE_COMBINE_ROWS
    sub = gt // gr
    nck = s // ck
    n_group_tiles = (TOP_K * s) // gt + ne
    n_sub = n_group_tiles * sub

    gate, rank, counts, xb = _router(x, router_w, router_b)

    cnt = counts[0].astype(I32)
    padded = ((cnt + gt - 1) // gt) * gt
    off = (jnp.cumsum(padded) - padded).astype(I32)
    n_used = (jnp.sum(padded) // gt).astype(I32)
    tile_expert = jnp.clip(
        jnp.searchsorted(jnp.cumsum(padded), jnp.arange(n_group_tiles, dtype=I32) * gt, side="right"), 0, ne - 1
    ).astype(I32)
    routed = (rank >= 0.0).reshape(nck, ck, ne)
    cum = jnp.cumsum(jnp.sum(routed, axis=1), axis=0).astype(I32).T

    u = jnp.arange(n_sub, dtype=I32)
    e_u = tile_expert[u // sub]
    r0 = u * gr - off[e_u]
    r1 = jnp.minimum(r0 + gr, cnt[e_u]) - 1
    live = (u // sub < n_used) & (r0 < cnt[e_u])
    cum_u = cum[e_u]
    c_lo = jnp.sum(cum_u <= r0[:, None], axis=1).astype(I32)
    c_hi = jnp.sum(cum_u <= r1[:, None], axis=1).astype(I32)
    n_items = jnp.where(live, c_hi - c_lo + 1, 0)
    item_start = jnp.concatenate([jnp.zeros((1,), I32), jnp.cumsum(n_items).astype(I32)])
    c_lo = jnp.where(live, c_lo, 0)

    fc = MOE_FF_CHUNK
    nf = D_FF_E // fc
    tile_of = lambda i, nu: jnp.minimum(i, nu[0] - 1)
    chunk_of = lambda i, f, nu: jnp.where(i < nu[0], f, nf - 1)
    ys = pl.pallas_call(
        _moe_gemm_kernel,
        out_shape=jax.ShapeDtypeStruct((n_group_tiles * gt, D_MODEL), BF16),
        grid_spec=pltpu.PrefetchScalarGridSpec(
            num_scalar_prefetch=5,
            grid=(n_group_tiles, nf),
            in_specs=[pl.BlockSpec(memory_space=pl.ANY),
                      pl.BlockSpec((ne, nck, ck), lambda i, f, te, nu, *_: (0, 0, 0)),
                      pl.BlockSpec((1, D_MODEL, fc),
                                   lambda i, f, te, nu, *_: (te[tile_of(i, nu)], 0, chunk_of(i, f, nu))),
                      pl.BlockSpec((1, D_MODEL, fc),
                                   lambda i, f, te, nu, *_: (te[tile_of(i, nu)], 0, nf + chunk_of(i, f, nu))),
                      pl.BlockSpec((1, fc, D_MODEL),
                                   lambda i, f, te, nu, *_: (te[tile_of(i, nu)], chunk_of(i, f, nu), 0))],
            out_specs=pl.BlockSpec((gt, D_MODEL), lambda i, f, *_: (i, 0)),
            scratch_shapes=[pltpu.VMEM((gt, D_MODEL), BF16), pltpu.VMEM((gt, D_MODEL), F32),
                            pltpu.VMEM((MOE_DMA_DEPTH, ck, D_MODEL), BF16),
                            pltpu.SemaphoreType.DMA((MOE_DMA_DEPTH,))]),
        compiler_params=_params(("arbitrary", "arbitrary"), 48),
        name="moe_gemm",
    )(tile_expert, n_used.reshape(1), off, item_start, c_lo, xb, rank.T.reshape(ne, nck, ck), w13, w13, w2)

    njc = s // tt
    per = tt // ck
    incl = cum[:, per - 1::per]
    lo = off[:, None] + jnp.concatenate([jnp.zeros((ne, 1), I32), incl[:, :-1]], axis=1)
    hi = off[:, None] + incl
    first, last = lo // cr, (hi - 1) // cr
    max_chunks = tt // cr + 1
    kk = jnp.arange(max_chunks, dtype=I32)
    slot_chunk = (first[:, :, None] + kk).transpose(1, 0, 2).reshape(njc, -1)
    slot_live = ((hi > lo)[:, :, None] & (first[:, :, None] + kk <= last[:, :, None])).transpose(1, 0, 2)
    slot_live = slot_live.reshape(njc, -1)
    slots = ne * max_chunks
    slot_expert = jnp.broadcast_to(jnp.repeat(jnp.arange(ne, dtype=I32), max_chunks)[None, :], (njc, slots))
    dest = jnp.cumsum(slot_live, axis=1) - 1
    place = slot_live[:, :, None] & (dest[:, :, None] == jnp.arange(slots)[None, None, :])
    item_chunk = jnp.sum(jnp.where(place, slot_chunk[:, :, None], 0), axis=1).astype(I32).reshape(-1)
    item_expert = jnp.sum(jnp.where(place, slot_expert[:, :, None], 0), axis=1).astype(I32).reshape(-1)
    item_count = jnp.sum(slot_live, axis=1).astype(I32)

    small = pl.BlockSpec((1, D_MODEL), lambda j, *_: (0, 0))
    return pl.pallas_call(
        functools.partial(_combine_kernel, slots=slots),
        out_shape=jax.ShapeDtypeStruct((s, D_MODEL), F32),
        grid_spec=pltpu.PrefetchScalarGridSpec(
            num_scalar_prefetch=4,
            grid=(njc,),
            in_specs=[pl.BlockSpec((tt, D_MODEL), lambda j, *_: (j, 0)),
                      pl.BlockSpec(memory_space=pl.ANY),
                      pl.BlockSpec((tt, ne), lambda j, *_: (j, 0)),
                      pl.BlockSpec((tt, ne), lambda j, *_: (j, 0)),
                      small, small],
            out_specs=pl.BlockSpec((tt, D_MODEL), lambda j, *_: (j, 0)),
            scratch_shapes=[pltpu.VMEM((tt, D_MODEL), F32), pltpu.VMEM((MOE_DMA_DEPTH, cr, D_MODEL), BF16),
                            pltpu.SemaphoreType.DMA((MOE_DMA_DEPTH,))]),
        compiler_params=_params(("arbitrary",), 32),
        name="moe_combine",
    )(off, item_count, item_expert, item_chunk, x, ys, rank, gate, ln_g.reshape(1, -1), ln_b.reshape(1, -1))


def kernel(x, rel_bias, w_in, w_gate, b_gate, conv_w, conv_b, w_qk_m, w_v_m, w_if, b_if, m_norm_g, w_br_a, w_br_m,
           w_o, ln_g, ln_b, ffn_w13, ffn_w2, router_w, router_b, exp_w13, exp_w2):
    batch, seq, _ = x.shape
    assert batch == 1
    h = x.reshape(seq, D_MODEL)
    biases = [_attn_bias(rel_bias, window, dilation) for window, dilation in ATTN_PATTERNS]
    for l in range(DEPTH):
        h = _token_mixer_layer(h, biases, w_in, w_gate, b_gate[l], conv_w[l], conv_b[l], w_qk_m, w_v_m, w_if[l],
                               b_if[l], m_norm_g[l], w_br_a, w_br_m, w_o, ln_g[l, 0], ln_b[l, 0], l)
        j = l // 2
        if l % 2 == 0:
            h = _dense_ffn(h, ffn_w13, ffn_w2, ln_g[l, 1], ln_b[l, 1], j)
        else:
            h = _moe_layer(h, router_w[j], router_b[j], exp_w13[j], exp_w2[j], ln_g[l, 1], ln_b[l, 1])
    return h.reshape(batch, seq, D_MODEL)
```

```python
import functools
import math

import jax
import jax.numpy as jnp
from jax import lax
from jax.experimental import pallas as pl
from jax.experimental.pallas import tpu as pltpu

F32 = jnp.float32
BF16 = jnp.bfloat16
I32 = jnp.int32

D_MODEL = 1024
DEPTH = 2
N_HEADS_A = 8
HEAD_DIM_A = 64
D_A = N_HEADS_A * HEAD_DIM_A
ATTN_PATTERNS = ((128, 1), (512, 4), (2048, 16))
ATTN_BLOCK = 128
NUM_BUCKETS = 32
MAX_DISTANCE = 2048
N_HEADS_M = 4
HEAD_DIM_M = 128
D_M = N_HEADS_M * HEAD_DIM_M
CONV_K = 4
MLSTM_CHUNK = 128
N_PROJ = 5
P_IN = 3 * D_A + 2 * D_M
D_FF = 2816
N_EXPERTS = 8
TOP_K = 2
D_FF_E = 3584
ALPHA = (2.0 * DEPTH) ** 0.25
LN_EPS = 1e-5

NEG = -1e30
LOG2E = math.log2(math.e)
LANES = 128
HALO_ROWS = 16
MIB = 1024 * 1024

MOE_GROUP_TILE = 1024
MOE_FF_CHUNK = 896
MOE_GATHER_ROWS = 256
MOE_GATHER_TOKENS = 256
MOE_COMBINE_TOKENS = 512
MOE_COMBINE_ROWS = 256
MOE_DMA_DEPTH = 4
FFN_ROW_TILE = 1024
FFN_FF_CHUNK = 256


def _params(sem, vmem_mib):
    return pltpu.CompilerParams(dimension_semantics=sem, vmem_limit_bytes=vmem_mib * MIB)


def _sigmoid(x):
    return 1.0 / (1.0 + jnp.exp(-x))


def _layer_norm(r, g, b):
    mu = jnp.mean(r, axis=-1, keepdims=True)
    c = r - mu
    var = jnp.mean(c * c, axis=-1, keepdims=True)
    return c * lax.rsqrt(var + LN_EPS) * g + b


def _split3(a):
    hi = a.astype(BF16)
    r1 = a - hi.astype(F32)
    mid = r1.astype(BF16)
    lo = (r1 - mid.astype(F32)).astype(BF16)
    return hi, mid, lo


def _dot(a, b):
    return jnp.dot(a, b, preferred_element_type=F32)


def _dot_nt(a, b):
    return lax.dot_general(a, b, (((1,), (1,)), ((), ())), preferred_element_type=F32)


def _dot_tn(a, b):
    return lax.dot_general(a, b, (((0,), (0,)), ((), ())), preferred_element_type=F32)


IN_PROJ_ROWS = 1024
DILATIONS = tuple(d for _, d in ATTN_PATTERNS if d > 1)


def _in_proj_kernel(x_ref, w_ref, *refs):
    nat = refs[:N_PROJ]
    perm = refs[N_PROJ:N_PROJ + 3 * len(DILATIONS)]
    wb_ref, y_ref = refs[-2:]

    @pl.when(pl.program_id(0) == 0)
    def _cast_weights():
        wb_ref[...] = w_ref[...].astype(BF16)

    xb = x_ref[...].astype(BF16)
    for j in range(N_PROJ):
        y = _dot(xb, wb_ref[:, j * D_A:(j + 1) * D_A])
        if j == 0:
            y = y * (HEAD_DIM_A ** -0.5 * LOG2E)
        nat[j][...] = y.astype(BF16)
        if j >= 3:
            continue
        for c in range(D_A // LANES):
            y_ref[c] = y[:, c * LANES:(c + 1) * LANES]
        for di, d in enumerate(DILATIONS):
            out = perm[di * 3 + j]
            tiles, _, rpc, _ = out.shape
            for t in range(tiles):
                for r in range(d):
                    for c in range(D_A // LANES):
                        out[t, r, :, c * LANES:(c + 1) * LANES] = (
                            y_ref[c, pl.ds(t * d * rpc + r, rpc, stride=d), :].astype(BF16))


def _in_proj(x, w_in, layer):
    s = x.shape[0]
    tm = IN_PROJ_ROWS
    blk = ATTN_BLOCK
    out_shape = [jax.ShapeDtypeStruct((s, D_A), BF16)] * N_PROJ
    out_specs = [pl.BlockSpec((tm, D_A), lambda i: (i, 0))] * N_PROJ
    for d in DILATIONS:
        tile = d * blk
        if tile <= tm:
            spec = pl.BlockSpec((tm // tile, d, blk, D_A), lambda i: (i, 0, 0, 0))
        else:
            parts = tile // tm
            spec = pl.BlockSpec((1, d, blk // parts, D_A), lambda i, parts=parts: (i // parts, 0, i % parts, 0))
        out_shape += [jax.ShapeDtypeStruct((s // tile, d, blk, D_A), BF16)] * 3
        out_specs += [spec] * 3
    outs = pl.pallas_call(
        _in_proj_kernel,
        out_shape=tuple(out_shape),
        grid=(s // tm,),
        in_specs=[
            pl.BlockSpec((tm, D_MODEL), lambda i: (i, 0)),
            pl.BlockSpec((None, D_MODEL, P_IN), lambda i: (layer, 0, 0), pipeline_mode=pl.Buffered(1)),
        ],
        out_specs=tuple(out_specs),
        scratch_shapes=[pltpu.VMEM((D_MODEL, P_IN), BF16), pltpu.VMEM((D_A // LANES, tm, LANES), F32)],
        compiler_params=_params(("arbitrary",), 56),
        name="in_proj",
    )(x, w_in)
    nat = outs[:N_PROJ]
    perm = [tuple(t.reshape(s, D_A) for t in outs[N_PROJ + 3 * i:N_PROJ + 3 * i + 3]) for i in range(len(DILATIONS))]
    return nat, perm


ATTN_STEP_BLOCKS = 16


def _rel_bucket(dist):
    exact = NUM_BUCKETS // 2
    d = jnp.maximum(dist, exact).astype(F32)
    log_b = exact + (jnp.log(d / exact) / math.log(MAX_DISTANCE / exact) * (NUM_BUCKETS - exact)).astype(I32)
    return jnp.where(dist < exact, dist, jnp.minimum(log_b, NUM_BUCKETS - 1))


def _attn_bias(rel_bias, window, dilation):
    blk = ATTN_BLOCK
    qi = jnp.arange(blk)[:, None]
    kj = jnp.arange(2 * blk)[None, :]
    rel = qi + blk - kj
    bucket = _rel_bucket(jnp.maximum(rel, 0) * dilation)
    onehot = (bucket[..., None] == jnp.arange(NUM_BUCKETS)).astype(F32)
    bias = jnp.einsum("qkb,bh->hqk", onehot, rel_bias.astype(F32), precision=lax.Precision.HIGHEST)
    mask = (rel >= 0) & (rel <= window // dilation)
    return jnp.where(mask[None], bias * LOG2E, NEG)


def _attn_kernel(q_ref, kp_ref, kc_ref, vp_ref, vc_ref, bias_ref, o_ref, st_ref, on_ref, sn_ref, *, dilation):
    blk = ATTN_BLOCK
    nb = ATTN_STEP_BLOCKS
    step = pl.program_id(0)
    kj = lax.broadcasted_iota(I32, (2 * blk, 2 * blk), 1)
    lane = lax.broadcasted_iota(I32, (blk, LANES), 1)
    ones = jnp.ones((2 * blk, LANES), BF16)
    pairs = range(N_HEADS_A // 2)

    def block(g, carry):
        cur = pl.multiple_of(g * blk, blk)
        in_tile = g >= dilation
        prev_c = pl.multiple_of(jnp.maximum(g - dilation, 0) * blk, blk)
        prev_p = pl.multiple_of(jnp.minimum(nb + g - dilation, nb - 1) * blk, blk)
        pen = jnp.where(jnp.logical_and(step == 0, g < dilation), NEG, 0.0).astype(F32)
        prev_pen = jnp.where(kj < blk, pen, 0.0)
        tile_i, cls = g // dilation, g % dilation
        dst = pl.ds(tile_i * (blk * dilation) + cls, blk, stride=dilation)

        def keys(cur_ref, prev_ref, ls):
            prev = jnp.where(in_tile, cur_ref[pl.ds(prev_c, blk), ls], prev_ref[pl.ds(prev_p, blk), ls])
            return jnp.concatenate([prev, cur_ref[pl.ds(cur, blk), ls]], axis=0)

        even = (lane // HEAD_DIM_A) == 0
        lss = [slice(hp * LANES, (hp + 1) * LANES) for hp in pairs]
        qb = [q_ref[pl.ds(cur, blk), lss[hp]] for hp in pairs]
        qcat = [jnp.concatenate([jnp.where(even, qb[hp], jnp.zeros_like(qb[hp])),
                                 jnp.where(even, jnp.zeros_like(qb[hp]), qb[hp])], axis=0) for hp in pairs]
        kb = [keys(kc_ref, kp_ref, lss[hp]) for hp in pairs]
        v1 = [jnp.concatenate([keys(vc_ref, vp_ref, lss[hp]), ones], axis=1) for hp in pairs]
        logits = [_dot_nt(qcat[hp], kb[hp]) + bias_ref[hp] + prev_pen for hp in pairs]
        m = [jnp.max(logits[hp], axis=-1, keepdims=True) for hp in pairs]
        p = [jnp.exp2(logits[hp] - m[hp]).astype(BF16) for hp in pairs]
        ol = [_dot(p[hp], v1[hp]) for hp in pairs]
        stats = jnp.zeros((blk, LANES), F32)
        for hp in pairs:
            for par in range(2):
                h, rs = 2 * hp + par, slice(par * blk, (par + 1) * blk)
                stats = jnp.where(lane == h, m[hp][rs], stats)
                stats = jnp.where(lane == N_HEADS_A + h, ol[hp][rs, LANES:], stats)
            on_ref[hp, dst, :] = jnp.where(even, ol[hp][:blk, :LANES], ol[hp][blk:, :LANES])
        sn_ref[dst, :] = stats
        return carry

    lax.fori_loop(0, nb, block, 0, unroll=2)
    for hp in range(N_HEADS_A // 2):
        o_ref[:, hp * LANES:(hp + 1) * LANES] = on_ref[hp].astype(o_ref.dtype)
    st_ref[...] = sn_ref[...]


def _attn_pattern(q, k, v, bias, dilation):
    s = q.shape[0]
    rows = ATTN_STEP_BLOCKS * ATTN_BLOCK
    cur = pl.BlockSpec((rows, D_A), lambda i: (i, 0))
    prev = pl.BlockSpec((rows, D_A), lambda i: (jnp.maximum(i - 1, 0), 0))
    return pl.pallas_call(
        functools.partial(_attn_kernel, dilation=dilation),
        out_shape=(jax.ShapeDtypeStruct((s, D_A), BF16), jax.ShapeDtypeStruct((s, LANES), F32)),
        grid=(s // rows,),
        in_specs=[cur, prev, cur, prev, cur,
                  pl.BlockSpec((N_HEADS_A // 2, 2 * ATTN_BLOCK, 2 * ATTN_BLOCK), lambda i: (0, 0, 0))],
        out_specs=(pl.BlockSpec((rows, D_A), lambda i: (i, 0)), pl.BlockSpec((rows, LANES), lambda i: (i, 0))),
        scratch_shapes=[pltpu.VMEM((D_A // LANES, rows, LANES), F32), pltpu.VMEM((rows, LANES), F32)],
        compiler_params=_params(("arbitrary",), 48),
        name=f"attn_d{dilation}",
    )(q, k, k, v, v, bias.reshape(N_HEADS_A // 2, 2 * ATTN_BLOCK, 2 * ATTN_BLOCK))


def _log_sigmoid(x):
    return jnp.minimum(x, 0.0) - jnp.log(1.0 + jnp.exp(-jnp.abs(x)))


def _mlstm_prep_kernel(xm_ref, halo_ref, cw_ref, cb_ref, wqk_ref, wkt_ref, wv_ref, wif_ref, wift_ref, bif_ref,
                       bift_ref, q_ref, k_ref, kt_ref, v_ref, gc_ref, gr_ref):
    tm = xm_ref.shape[0]
    hm, hd, lc = N_HEADS_M, HEAD_DIM_M, MLSTM_CHUNK
    xmb = xm_ref[...]
    halo = jnp.where(pl.program_id(0) == 0, 0.0, halo_ref[...].astype(F32))
    xx = jnp.concatenate([halo, xmb.astype(F32)], axis=0)
    conv = jnp.zeros((tm, D_M), F32) + cb_ref[...]
    for j in range(CONV_K):
        start = HALO_ROWS - (CONV_K - 1) + j
        conv = conv + cw_ref[j:j + 1, :] * xx[start:start + tm, :]
    xcb = (conv * _sigmoid(conv)).astype(BF16)

    qs, ks, vs = [], [], []
    for h in range(hm):
        sl = slice(h * hd, (h + 1) * hd)
        qs.append(_dot(xcb[:, sl], wqk_ref[0, h].astype(BF16)))
        ks.append(_dot(xcb[:, sl], wqk_ref[1, h].astype(BF16)))
        vs.append(_dot(xmb[:, sl], wv_ref[h].astype(BF16)))
        kt = _dot_nt(wkt_ref[h].astype(BF16), xcb[:, sl])
        kt_ref[sl, :] = (kt * (hd ** -0.5)).astype(BF16)
    q = jnp.concatenate(qs, axis=1)
    k = jnp.concatenate(ks, axis=1)
    v = jnp.concatenate(vs, axis=1)
    q_ref[...] = q.astype(BF16)
    k_ref[...] = (k * (hd ** -0.5)).astype(BF16)
    v_ref[...] = v.astype(BF16)

    qkv = jnp.concatenate([q, k, v], axis=1).astype(BF16)
    gates_c = _dot(qkv, wif_ref[...].astype(BF16)) + bif_ref[...]
    gates_r = _dot_nt(wift_ref[...].astype(BF16), qkv) + bift_ref[...]
    lane = lax.broadcasted_iota(I32, gates_c.shape, 1)
    row = lax.broadcasted_iota(I32, gates_r.shape, 0)
    gc_ref[...] = jnp.where(lane < hm, gates_c, _log_sigmoid(gates_c))
    gr_ref[...] = jnp.where(row < hm, gates_r, _log_sigmoid(gates_r))

    ri = lax.broadcasted_iota(I32, (lc, lc), 0)
    ci = lax.broadcasted_iota(I32, (lc, lc), 1)
    lower = (ri >= ci).astype(BF16)
    upper = (ri <= ci).astype(BF16)
    lane_c = lax.broadcasted_iota(I32, (lc, 2 * hm), 1)
    row_c = lax.broadcasted_iota(I32, (2 * hm, lc), 0)
    for c in range(tm // lc):
        rs = slice(c * lc, (c + 1) * lc)
        gcc = gc_ref[rs, :]
        grc = gr_ref[:, rs]
        cum_c = sum(_dot(lower, part) for part in _split3(gcc))
        cum_r = sum(_dot(part, upper) for part in _split3(grc))
        gc_ref[rs, :] = jnp.where(lane_c < hm, gcc, cum_c)
        gr_ref[:, rs] = jnp.where(row_c < hm, grc, cum_r)


def _mlstm_prep(xm, conv_w, conv_b, w_qk_m, w_v_m, w_if, b_if, layer):
    s = xm.shape[0]
    tm = 512
    hpt = tm // HALO_ROWS
    full = lambda shape: pl.BlockSpec(shape, lambda i: (0,) * len(shape))
    stacked = lambda shape: pl.BlockSpec((None,) + shape, lambda i: (layer,) + (0,) * len(shape))
    row_spec = pl.BlockSpec((tm, D_M), lambda i: (i, 0))
    head_sq = (N_HEADS_M, HEAD_DIM_M, HEAD_DIM_M)
    wk_t = jnp.swapaxes(w_qk_m[layer, 1], -1, -2)
    return pl.pallas_call(
        _mlstm_prep_kernel,
        out_shape=(jax.ShapeDtypeStruct((s, D_M), BF16), jax.ShapeDtypeStruct((s, D_M), BF16),
                   jax.ShapeDtypeStruct((D_M, s), BF16), jax.ShapeDtypeStruct((s, D_M), BF16),
                   jax.ShapeDtypeStruct((s, 2 * N_HEADS_M), F32), jax.ShapeDtypeStruct((2 * N_HEADS_M, s), F32)),
        grid=(s // tm,),
        in_specs=[
            row_spec,
            pl.BlockSpec((HALO_ROWS, D_M), lambda i: (jnp.maximum(i * hpt - 1, 0), 0)),
            full((CONV_K, D_M)), full((1, D_M)),
            stacked((2,) + head_sq), full(head_sq), stacked(head_sq),
            full((3 * D_M, 2 * N_HEADS_M)), full((2 * N_HEADS_M, 3 * D_M)),
            full((1, 2 * N_HEADS_M)), full((2 * N_HEADS_M, 1)),
        ],
        out_specs=(row_spec, row_spec, pl.BlockSpec((D_M, tm), lambda i: (0, i)), row_spec,
                   pl.BlockSpec((tm, 2 * N_HEADS_M), lambda i: (i, 0)),
                   pl.BlockSpec((2 * N_HEADS_M, tm), lambda i: (0, i))),
        compiler_params=_params(("arbitrary",), 32),
        name="mlstm_prep",
    )(xm, xm, conv_w, conv_b.reshape(1, D_M), w_qk_m, wk_t, w_v_m, w_if, w_if.T,
      b_if.reshape(1, -1), b_if.reshape(-1, 1))


MLSTM_STEP_CHUNKS = 4


def _mlstm_scan_kernel(q_ref, k_ref, kt_ref, v_ref, gc_ref, gr_ref, z_ref, g_ref, y_ref, c_ref, m_ref, *, chunks):
    hm, hd, lc = N_HEADS_M, HEAD_DIM_M, MLSTM_CHUNK

    @pl.when(pl.program_id(0) == 0)
    def _init():
        c_ref[...] = jnp.zeros_like(c_ref)
        m_ref[...] = jnp.zeros_like(m_ref)

    ri = lax.broadcasted_iota(I32, (lc, lc), 0)
    ci = lax.broadcasted_iota(I32, (lc, lc), 1)
    causal = ri >= ci
    ones = jnp.ones((lc, hd), BF16)
    heads = range(hm)
    hsl = [slice(h * hd, (h + 1) * hd) for h in heads]
    m_state = [m_ref[h:h + 1, 0:1] for h in heads]
    for c in range(chunks):
        rs = slice(c * lc, (c + 1) * lc)
        gc = gc_ref[rs, :]
        gr = gr_ref[:, rs]
        qs = [q_ref[rs, hsl[h]] for h in heads]
        v1 = [jnp.concatenate([v_ref[rs, hsl[h]], ones], axis=1) for h in heads]
        i_row = [gr[h:h + 1, :] for h in heads]
        b_row = [gr[hm + h:hm + h + 1, :] for h in heads]
        b_col = [gc[:, hm + h:hm + h + 1] for h in heads]
        qk = [_dot_nt(qs[h], k_ref[rs, hsl[h]]) for h in heads]
        c_prev = [c_ref[h] for h in heads]
        qc = [_dot(qs[h], c_prev[h].astype(BF16)) for h in heads]
        dm = [jnp.where(causal, b_col[h] - b_row[h] + i_row[h], NEG) for h in heads]
        inter = [b_col[h] + m_state[h] for h in heads]
        m_loc = [jnp.maximum(inter[h], jnp.max(dm[h], axis=-1, keepdims=True)) for h in heads]
        sc = [(qk[h] * jnp.exp(dm[h] - m_loc[h])).astype(BF16) for h in heads]
        both = [jnp.exp(inter[h] - m_loc[h]) * qc[h] + _dot(sc[h], v1[h]) for h in heads]
        hval = [both[h][:, :hd] / jnp.maximum(jnp.abs(both[h][:, hd:]), jnp.exp(-m_loc[h])) for h in heads]

        b_last = [b_row[h][:, lc - 1:lc] for h in heads]
        g_row = [b_last[h] - b_row[h] + i_row[h] for h in heads]
        m_new = [jnp.maximum(b_last[h] + m_state[h], jnp.max(g_row[h], axis=-1, keepdims=True)) for h in heads]
        ktw = [(kt_ref[hsl[h], rs].astype(F32) * jnp.exp(g_row[h] - m_new[h])).astype(BF16) for h in heads]
        for h in heads:
            c_ref[h] = jnp.exp(b_last[h] + m_state[h] - m_new[h]) * c_prev[h] + _dot(ktw[h], v1[h])
        m_state = m_new

        mu = [jnp.mean(hval[h], axis=-1, keepdims=True) for h in heads]
        cen = [hval[h] - mu[h] for h in heads]
        var = [jnp.mean(cen[h] * cen[h], axis=-1, keepdims=True) for h in heads]
        for h in heads:
            hn = cen[h] * lax.rsqrt(var[h] + LN_EPS) * g_ref[:, hsl[h]]
            y_ref[rs, hsl[h]] = (_sigmoid(z_ref[rs, hsl[h]].astype(F32)) * hn).astype(y_ref.dtype)
    for h in heads:
        m_ref[h:h + 1, :] = jnp.broadcast_to(m_state[h], (1, LANES))


def _mlstm_scan(q, k, kt, v, gc, gr, z, m_norm_g):
    s = q.shape[0]
    chunks = MLSTM_STEP_CHUNKS
    tm = chunks * MLSTM_CHUNK
    row_spec = pl.BlockSpec((tm, D_M), lambda i: (i, 0))
    return pl.pallas_call(
        functools.partial(_mlstm_scan_kernel, chunks=chunks),
        out_shape=jax.ShapeDtypeStruct((s, D_M), BF16),
        grid=(s // tm,),
        in_specs=[row_spec, row_spec, pl.BlockSpec((D_M, tm), lambda i: (0, i)), row_spec,
                  pl.BlockSpec((tm, 2 * N_HEADS_M), lambda i: (i, 0)),
                  pl.BlockSpec((2 * N_HEADS_M, tm), lambda i: (0, i)),
                  row_spec,
                  pl.BlockSpec((1, D_M), lambda i: (0, 0))],
        out_specs=row_spec,
        scratch_shapes=[pltpu.VMEM((N_HEADS_M, HEAD_DIM_M, 2 * HEAD_DIM_M), F32),
                        pltpu.VMEM((8, LANES), F32)],
        compiler_params=_params(("arbitrary",), 32),
        name="mlstm_scan",
    )(q, k, kt, v, gc, gr, z, m_norm_g.reshape(1, D_M))


def _merge_kernel(x_ref, o1_ref, o2_ref, o3_ref, l1_ref, l2_ref, l3_ref, ym_ref,
                  wg_ref, bg_ref, wa_ref, wm_ref, wo_ref, lng_ref, lnb_ref, out_ref,
                  wgb_ref, wab_ref, wmb_ref, wob_ref):
    @pl.when(pl.program_id(0) == 0)
    def _cast_weights():
        wgb_ref[...] = wg_ref[...].astype(BF16)
        wab_ref[...] = wa_ref[...].astype(BF16)
        wmb_ref[...] = wm_ref[...].astype(BF16)
        wob_ref[...] = wo_ref[...].astype(BF16)

    x = x_ref[...]
    xb = x.astype(BF16)
    stats = (l1_ref[...], l2_ref[...], l3_ref[...])
    mx = jnp.maximum(jnp.maximum(stats[0], stats[1]), stats[2])
    es = [jnp.exp2(st - mx) for st in stats]
    ls = [pltpu.roll(st, LANES - N_HEADS_A, 1) for st in stats]
    den = es[0] * ls[0] + es[1] * ls[1] + es[2] * ls[2]
    head_lane = lax.broadcasted_iota(I32, den.shape, 1) < N_HEADS_A
    inv = jnp.where(head_lane, 1.0 / den, 0.0)
    hrow = lax.broadcasted_iota(I32, (LANES, D_A), 0)
    hcol = lax.broadcasted_iota(I32, (LANES, D_A), 1) // HEAD_DIM_A
    expand = (hrow == hcol).astype(BF16)
    ya = jnp.zeros((x.shape[0], D_A), F32)
    for e, o_ref in zip(es, (o1_ref, o2_ref, o3_ref)):
        w = sum(_dot(part, expand) for part in _split3(e * inv)[:2])
        ya = ya + w * o_ref[...].astype(F32)

    gate = _sigmoid(_dot(xb, wgb_ref[...]) + bg_ref[...])
    merged = (gate[:, :D_MODEL] * _dot(ya.astype(BF16), wab_ref[...])
              + gate[:, D_MODEL:] * _dot(ym_ref[...], wmb_ref[...]))
    y = _dot(merged.astype(BF16), wob_ref[...])
    out_ref[...] = _layer_norm(ALPHA * x + y, lng_ref[...], lnb_ref[...])


def _merge(x, outs, lses, ym, w_gate, b_gate, w_br_a, w_br_m, w_o, ln_g, ln_b, layer):
    s = x.shape[0]
    tm = 512
    res = lambda shape: pl.BlockSpec((None,) + shape, lambda i: (layer, 0, 0), pipeline_mode=pl.Buffered(1))
    small = lambda n: pl.BlockSpec((1, n), lambda i: (0, 0))
    rows = lambda n: pl.BlockSpec((tm, n), lambda i: (i, 0))
    return pl.pallas_call(
        _merge_kernel,
        out_shape=jax.ShapeDtypeStruct((s, D_MODEL), F32),
        grid=(s // tm,),
        in_specs=[rows(D_MODEL), rows(D_A), rows(D_A), rows(D_A), rows(LANES), rows(LANES), rows(LANES), rows(D_M),
                  res((D_MODEL, 2 * D_MODEL)), small(2 * D_MODEL), res((D_A, D_MODEL)), res((D_M, D_MODEL)),
                  res((D_MODEL, D_MODEL)), small(D_MODEL), small(D_MODEL)],
        out_specs=rows(D_MODEL),
        scratch_shapes=[pltpu.VMEM((D_MODEL, 2 * D_MODEL), BF16), pltpu.VMEM((D_A, D_MODEL), BF16),
                        pltpu.VMEM((D_M, D_MODEL), BF16), pltpu.VMEM((D_MODEL, D_MODEL), BF16)],
        compiler_params=_params(("arbitrary",), 56),
        name="merge",
    )(x, *outs, *lses, ym, w_gate, b_gate.reshape(1, -1), w_br_a, w_br_m, w_o,
      ln_g.reshape(1, -1), ln_b.reshape(1, -1))


def _token_mixer_layer(x, biases, w_in, w_gate, b_gate, conv_w, conv_b, w_qk_m, w_v_m, w_if, b_if, m_norm_g,
                       w_br_a, w_br_m, w_o, ln_g, ln_b, layer):
    (q, k, v, xm, zm), perm = _in_proj(x, w_in, layer)
    outs, stats = [], []
    for (_, dilation), bias in zip(ATTN_PATTERNS, biases):
        qd, kd, vd = (q, k, v) if dilation == 1 else perm[DILATIONS.index(dilation)]
        o, st = _attn_pattern(qd, kd, vd, bias, dilation)
        outs.append(o)
        stats.append(st)
    qm, km, ktm, vm, gc, gr = _mlstm_prep(xm, conv_w, conv_b, w_qk_m, w_v_m, w_if, b_if, layer)
    ym = _mlstm_scan(qm, km, ktm, vm, gc, gr, zm, m_norm_g)
    return _merge(x, outs, stats, ym, w_gate, b_gate, w_br_a, w_br_m, w_o, ln_g, ln_b, layer)


def _load_cast(chunks, stage_ref, sem_ref):
    copies = [pltpu.make_async_copy(src, stage_ref.at[k % 2], sem_ref.at[k % 2]) for k, (src, _) in enumerate(chunks)]
    copies[0].start()
    for k, (_, dst) in enumerate(chunks):
        if k + 1 < len(chunks):
            copies[k + 1].start()
        copies[k].wait()
        dst[...] = stage_ref[k % 2].astype(BF16)


def _ffn_kernel(x_ref, w13_hbm, w2_hbm, lng_ref, lnb_ref, out_ref, w13b_ref, w2b_ref, acc_ref, st13_ref, st2_ref,
                sem_ref, *, j):
    fc = FFN_FF_CHUNK

    @pl.when(pl.program_id(0) == 0)
    def _load_weights():
        cols = lambda c: pl.ds(c * fc, fc)
        _load_cast([(w13_hbm.at[j, :, cols(c)], w13b_ref.at[:, cols(c)]) for c in range(2 * D_FF // fc)],
                   st13_ref, sem_ref)
        _load_cast([(w2_hbm.at[j, cols(c), :], w2b_ref.at[cols(c), :]) for c in range(D_FF // fc)],
                   st2_ref, sem_ref)

    xb = x_ref[...].astype(BF16)
    for c in range(D_FF // fc):
        a = _dot(xb, w13b_ref[:, c * fc:(c + 1) * fc])
        g = _dot(xb, w13b_ref[:, D_FF + c * fc:D_FF + (c + 1) * fc])
        y = _dot((a * _sigmoid(a) * g).astype(BF16), w2b_ref[c * fc:(c + 1) * fc, :])
        if c == 0:
            acc_ref[...] = y
        else:
            acc_ref[...] += y
    out_ref[...] = _layer_norm(ALPHA * x_ref[...] + acc_ref[...], lng_ref[...], lnb_ref[...])


def _dense_ffn(x, w13, w2, ln_g, ln_b, j):
    s = x.shape[0]
    tm = min(FFN_ROW_TILE, s)
    fc = FFN_FF_CHUNK
    small = pl.BlockSpec((1, D_MODEL), lambda i: (0, 0))
    return pl.pallas_call(
        functools.partial(_ffn_kernel, j=j),
        out_shape=jax.ShapeDtypeStruct((s, D_MODEL), F32),
        grid=(s // tm,),
        in_specs=[pl.BlockSpec((tm, D_MODEL), lambda i: (i, 0)),
                  pl.BlockSpec(memory_space=pl.ANY), pl.BlockSpec(memory_space=pl.ANY),
                  small, small],
        out_specs=pl.BlockSpec((tm, D_MODEL), lambda i: (i, 0)),
        scratch_shapes=[pltpu.VMEM((D_MODEL, 2 * D_FF), BF16), pltpu.VMEM((D_FF, D_MODEL), BF16),
                        pltpu.VMEM((tm, D_MODEL), F32),
                        pltpu.VMEM((2, D_MODEL, fc), F32), pltpu.VMEM((2, fc, D_MODEL), F32),
                        pltpu.SemaphoreType.DMA((2,))],
        compiler_params=_params(("arbitrary",), 56),
        name="dense_ffn",
    )(x, w13, w2, ln_g.reshape(1, -1), ln_b.reshape(1, -1))


def _router_kernel(x_ref, rw_ref, rb_ref, gate_ref, rank_ref, cnt_ref, xb_ref, carry_ref):
    tm = x_ref.shape[0]
    ne = N_EXPERTS

    @pl.when(pl.program_id(0) == 0)
    def _init():
        carry_ref[...] = jnp.zeros_like(carry_ref)

    x = x_ref[...]
    xb_ref[...] = x.astype(BF16)
    xs = _split3(x)
    ws = _split3(rw_ref[...])
    logits = rb_ref[...] + sum(_dot(xs[a], ws[b]) for a, b in ((1, 0), (0, 1), (0, 0)))
    lane = lax.broadcasted_iota(I32, (tm, ne), 1)
    v1 = jnp.max(logits, axis=-1, keepdims=True)
    i1 = jnp.min(jnp.where(logits == v1, lane, ne), axis=-1, keepdims=True)
    rest = jnp.where(lane == i1, -jnp.inf, logits)
    v2 = jnp.max(rest, axis=-1, keepdims=True)
    i2 = jnp.min(jnp.where(rest == v2, lane, ne), axis=-1, keepdims=True)
    e2 = jnp.exp(v2 - v1)
    den = 1.0 + e2
    sel1, sel2 = lane == i1, lane == i2
    gate_ref[...] = jnp.where(sel1, 1.0 / den, 0.0) + jnp.where(sel2, e2 / den, 0.0)
    sel = jnp.where(sel1 | sel2, 1.0, 0.0)
    ri = lax.broadcasted_iota(I32, (tm, tm), 0)
    ci = lax.broadcasted_iota(I32, (tm, tm), 1)
    before = (ri > ci).astype(BF16)
    carry = carry_ref[0:1, 0:ne]
    rank = _dot(before, sel.astype(BF16)) + carry
    rank_ref[...] = jnp.where(sel > 0.0, rank, -1.0)
    total = carry + jnp.sum(sel, axis=0, keepdims=True)
    carry_ref[0:1, 0:ne] = total
    cnt_ref[...] = total


def _router(x, router_w, router_b):
    s = x.shape[0]
    tm = 512
    ne = N_EXPERTS
    return pl.pallas_call(
        _router_kernel,
        out_shape=(jax.ShapeDtypeStruct((s, ne), F32), jax.ShapeDtypeStruct((s, ne), F32),
                   jax.ShapeDtypeStruct((1, ne), F32), jax.ShapeDtypeStruct((s, D_MODEL), BF16)),
        grid=(s // tm,),
        in_specs=[pl.BlockSpec((tm, D_MODEL), lambda i: (i, 0)),
                  pl.BlockSpec((D_MODEL, ne), lambda i: (0, 0)),
                  pl.BlockSpec((1, ne), lambda i: (0, 0))],
        out_specs=(pl.BlockSpec((tm, ne), lambda i: (i, 0)), pl.BlockSpec((tm, ne), lambda i: (i, 0)),
                   pl.BlockSpec((1, ne), lambda i: (0, 0)), pl.BlockSpec((tm, D_MODEL), lambda i: (i, 0))),
        scratch_shapes=[pltpu.VMEM((8, LANES), F32)],
        compiler_params=_params(("arbitrary",), 32),
        name="moe_router",
    )(x, router_w, router_b.reshape(1, ne))


def _chunk_copy(src_hbm, buf_ref, sem_ref, chunk, slot):
    rows = buf_ref.shape[1]
    start = pl.multiple_of(chunk * rows, rows)
    return pltpu.make_async_copy(src_hbm.at[pl.ds(start, rows), :], buf_ref.at[slot], sem_ref.at[slot])


def _moe_gemm_kernel(te_ref, nu_ref, off_ref, ist_ref, clo_ref, xb_hbm, rank_ref, w1_ref, w3_ref, w2_ref,
                     y_ref, xs_ref, acc_ref, buf_ref, sem_ref):
    gr, ck = MOE_GATHER_ROWS, MOE_GATHER_TOKENS
    sub = MOE_GROUP_TILE // gr
    nbuf = buf_ref.shape[0]
    i, f = pl.program_id(0), pl.program_id(1)
    last_f = pl.num_programs(1) - 1

    def item(t, k):
        base = t * sub
        s = sum((k >= ist_ref[base + j]).astype(I32) for j in range(1, sub))
        return s, clo_ref[base + s] + (k - ist_ref[base + s])

    def start(t, k):
        @pl.when(k < ist_ref[t * sub + sub])
        def _():
            _chunk_copy(xb_hbm, buf_ref, sem_ref, item(t, k)[1], (k - ist_ref[t * sub]) % nbuf).start()

    @pl.when(i < nu_ref[0])
    def _tile():
        @pl.when(f == 0)
        def _gather_rows():
            e = te_ref[i]
            base = i * sub
            k0, k1 = ist_ref[base], ist_ref[base + sub]
            acc_ref[...] = jnp.zeros_like(acc_ref)

            @pl.when(i == 0)
            def _first_tile():
                for d in range(nbuf - 1):
                    start(i, k0 + d)

            def step(k, carry):
                slot = (k - k0) % nbuf
                s, chunk = item(i, k)
                _chunk_copy(xb_hbm, buf_ref, sem_ref, chunk, slot).wait()
                start(i, k + nbuf - 1)

                rank0 = ((base + s) * gr - off_ref[e]).astype(F32)
                row_rank = rank0 + lax.broadcasted_iota(I32, (gr, ck), 0).astype(F32)
                hit = row_rank == rank_ref[e, pl.ds(chunk, 1), :]
                rows = pl.ds(pl.multiple_of(s * gr, gr), gr)
                acc_ref[rows, :] += _dot(jnp.where(hit, 1.0, 0.0).astype(BF16), buf_ref[slot])
                return carry

            lax.fori_loop(k0, k1, step, 0)

            @pl.when(i + 1 < nu_ref[0])
            def _prefetch_next_tile():
                for d in range(nbuf - 1):
                    start(i + 1, k1 + d)

            xs_ref[...] = acc_ref[...].astype(BF16)
            acc_ref[...] = jnp.zeros_like(acc_ref)

        xb = xs_ref[...]
        a = _dot(xb, w1_ref[0].astype(BF16))
        g = _dot(xb, w3_ref[0].astype(BF16))
        hidden = (a * _sigmoid(a) * g).astype(BF16)
        acc_ref[...] += _dot(hidden, w2_ref[0].astype(BF16))

        @pl.when(f == last_f)
        def _finish():
            y_ref[...] = acc_ref[...].astype(y_ref.dtype)

    @pl.when(jnp.logical_and(i >= nu_ref[0], f == last_f))
    def _unused_tile():
        y_ref[...] = jnp.zeros_like(y_ref)


def _combine_kernel(off_ref, cn_ref, cie_ref, cic_ref, x_ref, ys_hbm, rank_ref, gate_ref, lng_ref, lnb_ref,
                    out_ref, acc_ref, buf_ref, sem_ref, *, slots):
    tt, cr = MOE_COMBINE_TOKENS, MOE_COMBINE_ROWS
    j = pl.program_id(0)
    n = cn_ref[j]
    base = j * slots
    acc_ref[...] = jnp.zeros_like(acc_ref)
    nbuf = buf_ref.shape[0]

    def start(k):
        @pl.when(k < n)
        def _():
            _chunk_copy(ys_hbm, buf_ref, sem_ref, cic_ref[base + jnp.minimum(k, slots - 1)], k % nbuf).start()

    for d in range(nbuf - 1):
        start(d)

    lane = lax.broadcasted_iota(I32, (tt, N_EXPERTS), 1)

    def step(k, carry):
        slot = k % nbuf
        e, chunk = cie_ref[base + k], cic_ref[base + k]
        _chunk_copy(ys_hbm, buf_ref, sem_ref, chunk, slot).wait()
        start(k + nbuf - 1)

        rank = jnp.max(jnp.where(lane == e, rank_ref[...], -1.0), axis=-1, keepdims=True)
        gate = jnp.sum(jnp.where(lane == e, gate_ref[...], 0.0), axis=-1, keepdims=True)
        pos = jnp.where(rank >= 0.0, rank + off_ref[e].astype(F32), -1.0)
        col = (chunk * cr).astype(F32) + lax.broadcasted_iota(I32, (tt, cr), 1).astype(F32)
        acc_ref[...] += gate * _dot(jnp.where(pos == col, 1.0, 0.0).astype(BF16), buf_ref[slot])
        return carry

    lax.fori_loop(0, n, step, 0)
    out_ref[...] = _layer_norm(ALPHA * x_ref[...] + acc_ref[...], lng_ref[...], lnb_ref[...])


def _moe_layer(x, router_w, router_b, w13, w2, ln_g, ln_b):
    s = x.shape[0]
    ne, gt, gr, ck = N_EXPERTS, MOE_GROUP_TILE, MOE_GATHER_ROWS, MOE_GATHER_TOKENS
    tt, cr = MOE_COMBINE_TOKENS, MOE_COMBINE_ROWS
    sub = gt // gr
    nck = s // ck
    n_group_tiles = (TOP_K * s) // gt + ne
    n_sub = n_group_tiles * sub

    gate, rank, counts, xb = _router(x, router_w, router_b)

    cnt = counts[0].astype(I32)
    padded = ((cnt + gt - 1) // gt) * gt
    off = (jnp.cumsum(padded) - padded).astype(I32)
    n_used = (jnp.sum(padded) // gt).astype(I32)
    tile_expert = jnp.clip(
        jnp.searchsorted(jnp.cumsum(padded), jnp.arange(n_group_tiles, dtype=I32) * gt, side="right"), 0, ne - 1
    ).astype(I32)
    routed = (rank >= 0.0).reshape(nck, ck, ne)
    cum = jnp.cumsum(jnp.sum(routed, axis=1), axis=0).astype(I32).T

    u = jnp.arange(n_sub, dtype=I32)
    e_u = tile_expert[u // sub]
    r0 = u * gr - off[e_u]
    r1 = jnp.minimum(r0 + gr, cnt[e_u]) - 1
    live = (u // sub < n_used) & (r0 < cnt[e_u])
    cum_u = cum[e_u]
    c_lo = jnp.sum(cum_u <= r0[:, None], axis=1).astype(I32)
    c_hi = jnp.sum(cum_u <= r1[:, None], axis=1).astype(I32)
    n_items = jnp.where(live, c_hi - c_lo + 1, 0)
    item_start = jnp.concatenate([jnp.zeros((1,), I32), jnp.cumsum(n_items).astype(I32)])
    c_lo = jnp.where(live, c_lo, 0)

    fc = MOE_FF_CHUNK
    nf = D_FF_E // fc
    tile_of = lambda i, nu: jnp.minimum(i, nu[0] - 1)
    chunk_of = lambda i, f, nu: jnp.where(i < nu[0], f, nf - 1)
    ys = pl.pallas_call(
        _moe_gemm_kernel,
        out_shape=jax.ShapeDtypeStruct((n_group_tiles * gt, D_MODEL), BF16),
        grid_spec=pltpu.PrefetchScalarGridSpec(
            num_scalar_prefetch=5,
            grid=(n_group_tiles, nf),
            in_specs=[pl.BlockSpec(memory_space=pl.ANY),
                      pl.BlockSpec((ne, nck, ck), lambda i, f, te, nu, *_: (0, 0, 0)),
                      pl.BlockSpec((1, D_MODEL, fc),
                                   lambda i, f, te, nu, *_: (te[tile_of(i, nu)], 0, chunk_of(i, f, nu))),
                      pl.BlockSpec((1, D_MODEL, fc),
                                   lambda i, f, te, nu, *_: (te[tile_of(i, nu)], 0, nf + chunk_of(i, f, nu))),
                      pl.BlockSpec((1, fc, D_MODEL),
                                   lambda i, f, te, nu, *_: (te[tile_of(i, nu)], chunk_of(i, f, nu), 0))],
            out_specs=pl.BlockSpec((gt, D_MODEL), lambda i, f, *_: (i, 0)),
            scratch_shapes=[pltpu.VMEM((gt, D_MODEL), BF16), pltpu.VMEM((gt, D_MODEL), F32),
                            pltpu.VMEM((MOE_DMA_DEPTH, ck, D_MODEL), BF16),
                            pltpu.SemaphoreType.DMA((MOE_DMA_DEPTH,))]),
        compiler_params=_params(("arbitrary", "arbitrary"), 58),
        name="moe_gemm",
    )(tile_expert, n_used.reshape(1), off, item_start, c_lo, xb, rank.T.reshape(ne, nck, ck), w13, w13, w2)

    njc = s // tt
    per = tt // ck
    incl = cum[:, per - 1::per]
    lo = off[:, None] + jnp.concatenate([jnp.zeros((ne, 1), I32), incl[:, :-1]], axis=1)
    hi = off[:, None] + incl
    first, last = lo // cr, (hi - 1) // cr
    max_chunks = tt // cr + 1
    kk = jnp.arange(max_chunks, dtype=I32)
    slot_chunk = (first[:, :, None] + kk).transpose(1, 0, 2).reshape(njc, -1)
    slot_live = ((hi > lo)[:, :, None] & (first[:, :, None] + kk <= last[:, :, None])).transpose(1, 0, 2)
    slot_live = slot_live.reshape(njc, -1)
    slots = ne * max_chunks
    slot_expert = jnp.broadcast_to(jnp.repeat(jnp.arange(ne, dtype=I32), max_chunks)[None, :], (njc, slots))
    dest = jnp.cumsum(slot_live, axis=1) - 1
    place = slot_live[:, :, None] & (dest[:, :, None] == jnp.arange(slots)[None, None, :])
    item_chunk = jnp.sum(jnp.where(place, slot_chunk[:, :, None], 0), axis=1).astype(I32).reshape(-1)
    item_expert = jnp.sum(jnp.where(place, slot_expert[:, :, None], 0), axis=1).astype(I32).reshape(-1)
    item_count = jnp.sum(slot_live, axis=1).astype(I32)

    small = pl.BlockSpec((1, D_MODEL), lambda j, *_: (0, 0))
    return pl.pallas_call(
        functools.partial(_combine_kernel, slots=slots),
        out_shape=jax.ShapeDtypeStruct((s, D_MODEL), F32),
        grid_spec=pltpu.PrefetchScalarGridSpec(
            num_scalar_prefetch=4,
            grid=(njc,),
            in_specs=[pl.BlockSpec((tt, D_MODEL), lambda j, *_: (j, 0)),
                      pl.BlockSpec(memory_space=pl.ANY),
                      pl.BlockSpec((tt, ne), lambda j, *_: (j, 0)),
                      pl.BlockSpec((tt, ne), lambda j, *_: (j, 0)),
                      small, small],
            out_specs=pl.BlockSpec((tt, D_MODEL), lambda j, *_: (j, 0)),
            scratch_shapes=[pltpu.VMEM((tt, D_MODEL), F32), pltpu.VMEM((MOE_DMA_DEPTH, cr, D_MODEL), BF16),
                            pltpu.SemaphoreType.DMA((MOE_DMA_DEPTH,))]),
        compiler_params=_params(("arbitrary",), 32),
        name="moe_combine",
    )(off, item_count, item_expert, item_chunk, x, ys, rank, gate, ln_g.reshape(1, -1), ln_b.reshape(1, -1))


def kernel(x, rel_bias, w_in, w_gate, b_gate, conv_w, conv_b, w_qk_m, w_v_m, w_if, b_if, m_norm_g, w_br_a, w_br_m,
           w_o, ln_g, ln_b, ffn_w13, ffn_w2, router_w, router_b, exp_w13, exp_w2):
    batch, seq, _ = x.shape
    assert batch == 1
    h = x.reshape(seq, D_MODEL)
    biases = [_attn_bias(rel_bias, window, dilation) for window, dilation in ATTN_PATTERNS]
    for l in range(DEPTH):
        h = _token_mixer_layer(h, biases, w_in, w_gate, b_gate[l], conv_w[l], conv_b[l], w_qk_m, w_v_m, w_if[l],
                               b_if[l], m_norm_g[l], w_br_a, w_br_m, w_o, ln_g[l, 0], ln_b[l, 0], l)
        j = l // 2
        if l % 2 == 0:
            h = _dense_ffn(h, ffn_w13, ffn_w2, ln_g[l, 1], ln_b[l, 1], j)
        else:
            h = _moe_layer(h, router_w[j], router_b[j], exp_w13[j], exp_w2[j], ln_g[l, 1], ln_b[l, 1])
    return h.reshape(batch, seq, D_MODEL)
```

```python
import functools
import math

import jax
import jax.numpy as jnp
from jax import lax
from jax.experimental import pallas as pl
from jax.experimental.pallas import tpu as pltpu

F32 = jnp.float32
BF16 = jnp.bfloat16
I32 = jnp.int32

D_MODEL = 1024
DEPTH = 2
N_HEADS_A = 8
HEAD_DIM_A = 64
D_A = N_HEADS_A * HEAD_DIM_A
ATTN_PATTERNS = ((128, 1), (512, 4), (2048, 16))
ATTN_BLOCK = 128
NUM_BUCKETS = 32
MAX_DISTANCE = 2048
N_HEADS_M = 4
HEAD_DIM_M = 128
D_M = N_HEADS_M * HEAD_DIM_M
CONV_K = 4
MLSTM_CHUNK = 128
N_PROJ = 5
P_IN = 3 * D_A + 2 * D_M
D_FF = 2816
N_EXPERTS = 8
TOP_K = 2
D_FF_E = 3584
ALPHA = (2.0 * DEPTH) ** 0.25
LN_EPS = 1e-5

NEG = -1e30
LOG2E = math.log2(math.e)
LANES = 128
HALO_ROWS = 16
MIB = 1024 * 1024

MOE_GROUP_TILE = 1024
MOE_FF_CHUNK = 512
MOE_GATHER_ROWS = 256
MOE_GATHER_TOKENS = 256
MOE_COMBINE_TOKENS = 512
MOE_COMBINE_ROWS = 256
MOE_DMA_DEPTH = 6
FFN_ROW_TILE = 1024
FFN_FF_CHUNK = 256


def _params(sem, vmem_mib):
    return pltpu.CompilerParams(dimension_semantics=sem, vmem_limit_bytes=vmem_mib * MIB)


def _sigmoid(x):
    return 1.0 / (1.0 + jnp.exp(-x))


def _layer_norm(r, g, b):
    mu = jnp.mean(r, axis=-1, keepdims=True)
    c = r - mu
    var = jnp.mean(c * c, axis=-1, keepdims=True)
    return c * lax.rsqrt(var + LN_EPS) * g + b


def _split3(a):
    hi = a.astype(BF16)
    r1 = a - hi.astype(F32)
    mid = r1.astype(BF16)
    lo = (r1 - mid.astype(F32)).astype(BF16)
    return hi, mid, lo


def _dot(a, b):
    return jnp.dot(a, b, preferred_element_type=F32)


def _dot_nt(a, b):
    return lax.dot_general(a, b, (((1,), (1,)), ((), ())), preferred_element_type=F32)


def _dot_tn(a, b):
    return lax.dot_general(a, b, (((0,), (0,)), ((), ())), preferred_element_type=F32)


IN_PROJ_ROWS = 1024
DILATIONS = tuple(d for _, d in ATTN_PATTERNS if d > 1)


def _in_proj_kernel(x_ref, w_ref, *refs):
    nat = refs[:N_PROJ]
    perm = refs[N_PROJ:N_PROJ + 3 * len(DILATIONS)]
    wb_ref, y_ref = refs[-2:]

    @pl.when(pl.program_id(0) == 0)
    def _cast_weights():
        wb_ref[...] = w_ref[...].astype(BF16)

    xb = x_ref[...].astype(BF16)
    for j in range(N_PROJ):
        y = _dot(xb, wb_ref[:, j * D_A:(j + 1) * D_A])
        if j == 0:
            y = y * (HEAD_DIM_A ** -0.5 * LOG2E)
        nat[j][...] = y.astype(BF16)
        if j >= 3:
            continue
        for c in range(D_A // LANES):
            y_ref[c] = y[:, c * LANES:(c + 1) * LANES]
        for di, d in enumerate(DILATIONS):
            out = perm[di * 3 + j]
            tiles, _, rpc, _ = out.shape
            for t in range(tiles):
                for r in range(d):
                    for c in range(D_A // LANES):
                        out[t, r, :, c * LANES:(c + 1) * LANES] = (
                            y_ref[c, pl.ds(t * d * rpc + r, rpc, stride=d), :].astype(BF16))


def _in_proj(x, w_in, layer):
    s = x.shape[0]
    tm = IN_PROJ_ROWS
    blk = ATTN_BLOCK
    out_shape = [jax.ShapeDtypeStruct((s, D_A), BF16)] * N_PROJ
    out_specs = [pl.BlockSpec((tm, D_A), lambda i: (i, 0))] * N_PROJ
    for d in DILATIONS:
        tile = d * blk
        if tile <= tm:
            spec = pl.BlockSpec((tm // tile, d, blk, D_A), lambda i: (i, 0, 0, 0))
        else:
            parts = tile // tm
            spec = pl.BlockSpec((1, d, blk // parts, D_A), lambda i, parts=parts: (i // parts, 0, i % parts, 0))
        out_shape += [jax.ShapeDtypeStruct((s // tile, d, blk, D_A), BF16)] * 3
        out_specs += [spec] * 3
    outs = pl.pallas_call(
        _in_proj_kernel,
        out_shape=tuple(out_shape),
        grid=(s // tm,),
        in_specs=[
            pl.BlockSpec((tm, D_MODEL), lambda i: (i, 0)),
            pl.BlockSpec((None, D_MODEL, P_IN), lambda i: (layer, 0, 0), pipeline_mode=pl.Buffered(1)),
        ],
        out_specs=tuple(out_specs),
        scratch_shapes=[pltpu.VMEM((D_MODEL, P_IN), BF16), pltpu.VMEM((D_A // LANES, tm, LANES), F32)],
        compiler_params=_params(("arbitrary",), 56),
        name="in_proj",
    )(x, w_in)
    nat = outs[:N_PROJ]
    perm = [tuple(t.reshape(s, D_A) for t in outs[N_PROJ + 3 * i:N_PROJ + 3 * i + 3]) for i in range(len(DILATIONS))]
    return nat, perm


ATTN_STEP_BLOCKS = 16


def _rel_bucket(dist):
    exact = NUM_BUCKETS // 2
    d = jnp.maximum(dist, exact).astype(F32)
    log_b = exact + (jnp.log(d / exact) / math.log(MAX_DISTANCE / exact) * (NUM_BUCKETS - exact)).astype(I32)
    return jnp.where(dist < exact, dist, jnp.minimum(log_b, NUM_BUCKETS - 1))


def _attn_bias(rel_bias, window, dilation):
    blk = ATTN_BLOCK
    qi = jnp.arange(blk)[:, None]
    kj = jnp.arange(2 * blk)[None, :]
    rel = qi + blk - kj
    bucket = _rel_bucket(jnp.maximum(rel, 0) * dilation)
    onehot = (bucket[..., None] == jnp.arange(NUM_BUCKETS)).astype(F32)
    bias = jnp.einsum("qkb,bh->hqk", onehot, rel_bias.astype(F32), precision=lax.Precision.HIGHEST)
    mask = (rel >= 0) & (rel <= window // dilation)
    return jnp.where(mask[None], bias * LOG2E, NEG)


def _attn_kernel(q_ref, kp_ref, kc_ref, vp_ref, vc_ref, bias_ref, o_ref, st_ref, on_ref, sn_ref, *, dilation):
    blk = ATTN_BLOCK
    nb = ATTN_STEP_BLOCKS
    step = pl.program_id(0)
    kj = lax.broadcasted_iota(I32, (2 * blk, 2 * blk), 1)
    lane = lax.broadcasted_iota(I32, (blk, LANES), 1)
    ones = jnp.ones((2 * blk, LANES), BF16)
    pairs = range(N_HEADS_A // 2)

    def block(g, carry):
        cur = pl.multiple_of(g * blk, blk)
        in_tile = g >= dilation
        prev_c = pl.multiple_of(jnp.maximum(g - dilation, 0) * blk, blk)
        prev_p = pl.multiple_of(jnp.minimum(nb + g - dilation, nb - 1) * blk, blk)
        pen = jnp.where(jnp.logical_and(step == 0, g < dilation), NEG, 0.0).astype(F32)
        prev_pen = jnp.where(kj < blk, pen, 0.0)
        tile_i, cls = g // dilation, g % dilation
        dst = pl.ds(tile_i * (blk * dilation) + cls, blk, stride=dilation)

        def keys(cur_ref, prev_ref, ls):
            prev = jnp.where(in_tile, cur_ref[pl.ds(prev_c, blk), ls], prev_ref[pl.ds(prev_p, blk), ls])
            return jnp.concatenate([prev, cur_ref[pl.ds(cur, blk), ls]], axis=0)

        even = (lane // HEAD_DIM_A) == 0
        lss = [slice(hp * LANES, (hp + 1) * LANES) for hp in pairs]
        qb = [q_ref[pl.ds(cur, blk), lss[hp]] for hp in pairs]
        qcat = [jnp.concatenate([jnp.where(even, qb[hp], jnp.zeros_like(qb[hp])),
                                 jnp.where(even, jnp.zeros_like(qb[hp]), qb[hp])], axis=0) for hp in pairs]
        kb = [keys(kc_ref, kp_ref, lss[hp]) for hp in pairs]
        v1 = [jnp.concatenate([keys(vc_ref, vp_ref, lss[hp]), ones], axis=1) for hp in pairs]
        logits = [_dot_nt(qcat[hp], kb[hp]) + bias_ref[hp] + prev_pen for hp in pairs]
        m = [jnp.max(logits[hp], axis=-1, keepdims=True) for hp in pairs]
        p = [jnp.exp2(logits[hp] - m[hp]).astype(BF16) for hp in pairs]
        ol = [_dot(p[hp], v1[hp]) for hp in pairs]
        stats = jnp.zeros((blk, LANES), F32)
        for hp in pairs:
            for par in range(2):
                h, rs = 2 * hp + par, slice(par * blk, (par + 1) * blk)
                stats = jnp.where(lane == h, m[hp][rs], stats)
                stats = jnp.where(lane == N_HEADS_A + h, ol[hp][rs, LANES:], stats)
            on_ref[hp, dst, :] = jnp.where(even, ol[hp][:blk, :LANES], ol[hp][blk:, :LANES])
        sn_ref[dst, :] = stats
        return carry

    lax.fori_loop(0, nb, block, 0, unroll=2)
    for hp in range(N_HEADS_A // 2):
        o_ref[:, hp * LANES:(hp + 1) * LANES] = on_ref[hp].astype(o_ref.dtype)
    st_ref[...] = sn_ref[...]


def _attn_pattern(q, k, v, bias, dilation):
    s = q.shape[0]
    rows = ATTN_STEP_BLOCKS * ATTN_BLOCK
    cur = pl.BlockSpec((rows, D_A), lambda i: (i, 0))
    prev = pl.BlockSpec((rows, D_A), lambda i: (jnp.maximum(i - 1, 0), 0))
    return pl.pallas_call(
        functools.partial(_attn_kernel, dilation=dilation),
        out_shape=(jax.ShapeDtypeStruct((s, D_A), BF16), jax.ShapeDtypeStruct((s, LANES), F32)),
        grid=(s // rows,),
        in_specs=[cur, prev, cur, prev, cur,
                  pl.BlockSpec((N_HEADS_A // 2, 2 * ATTN_BLOCK, 2 * ATTN_BLOCK), lambda i: (0, 0, 0))],
        out_specs=(pl.BlockSpec((rows, D_A), lambda i: (i, 0)), pl.BlockSpec((rows, LANES), lambda i: (i, 0))),
        scratch_shapes=[pltpu.VMEM((D_A // LANES, rows, LANES), F32), pltpu.VMEM((rows, LANES), F32)],
        compiler_params=_params(("arbitrary",), 48),
        name=f"attn_d{dilation}",
    )(q, k, k, v, v, bias.reshape(N_HEADS_A // 2, 2 * ATTN_BLOCK, 2 * ATTN_BLOCK))


def _log_sigmoid(x):
    return jnp.minimum(x, 0.0) - jnp.log(1.0 + jnp.exp(-jnp.abs(x)))


def _mlstm_prep_kernel(xm_ref, halo_ref, cw_ref, cb_ref, wqk_ref, wkt_ref, wv_ref, wif_ref, wift_ref, bif_ref,
                       bift_ref, q_ref, k_ref, kt_ref, v_ref, gc_ref, gr_ref):
    tm = xm_ref.shape[0]
    hm, hd, lc = N_HEADS_M, HEAD_DIM_M, MLSTM_CHUNK
    xmb = xm_ref[...]
    halo = jnp.where(pl.program_id(0) == 0, 0.0, halo_ref[...].astype(F32))
    xx = jnp.concatenate([halo, xmb.astype(F32)], axis=0)
    conv = jnp.zeros((tm, D_M), F32) + cb_ref[...]
    for j in range(CONV_K):
        start = HALO_ROWS - (CONV_K - 1) + j
        conv = conv + cw_ref[j:j + 1, :] * xx[start:start + tm, :]
    xcb = (conv * _sigmoid(conv)).astype(BF16)

    qs, ks, vs = [], [], []
    for h in range(hm):
        sl = slice(h * hd, (h + 1) * hd)
        qs.append(_dot(xcb[:, sl], wqk_ref[0, h].astype(BF16)))
        ks.append(_dot(xcb[:, sl], wqk_ref[1, h].astype(BF16)))
        vs.append(_dot(xmb[:, sl], wv_ref[h].astype(BF16)))
        kt = _dot_nt(wkt_ref[h].astype(BF16), xcb[:, sl])
        kt_ref[sl, :] = (kt * (hd ** -0.5)).astype(BF16)
    q = jnp.concatenate(qs, axis=1)
    k = jnp.concatenate(ks, axis=1)
    v = jnp.concatenate(vs, axis=1)
    q_ref[...] = q.astype(BF16)
    k_ref[...] = (k * (hd ** -0.5)).astype(BF16)
    v_ref[...] = v.astype(BF16)

    qkv = jnp.concatenate([q, k, v], axis=1).astype(BF16)
    gates_c = _dot(qkv, wif_ref[...].astype(BF16)) + bif_ref[...]
    gates_r = _dot_nt(wift_ref[...].astype(BF16), qkv) + bift_ref[...]
    lane = lax.broadcasted_iota(I32, gates_c.shape, 1)
    row = lax.broadcasted_iota(I32, gates_r.shape, 0)
    gc_ref[...] = jnp.where(lane < hm, gates_c, _log_sigmoid(gates_c))
    gr_ref[...] = jnp.where(row < hm, gates_r, _log_sigmoid(gates_r))

    ri = lax.broadcasted_iota(I32, (lc, lc), 0)
    ci = lax.broadcasted_iota(I32, (lc, lc), 1)
    lower = (ri >= ci).astype(BF16)
    upper = (ri <= ci).astype(BF16)
    lane_c = lax.broadcasted_iota(I32, (lc, 2 * hm), 1)
    row_c = lax.broadcasted_iota(I32, (2 * hm, lc), 0)
    for c in range(tm // lc):
        rs = slice(c * lc, (c + 1) * lc)
        gcc = gc_ref[rs, :]
        grc = gr_ref[:, rs]
        cum_c = sum(_dot(lower, part) for part in _split3(gcc))
        cum_r = sum(_dot(part, upper) for part in _split3(grc))
        gc_ref[rs, :] = jnp.where(lane_c < hm, gcc, cum_c)
        gr_ref[:, rs] = jnp.where(row_c < hm, grc, cum_r)


def _mlstm_prep(xm, conv_w, conv_b, w_qk_m, w_v_m, w_if, b_if, layer):
    s = xm.shape[0]
    tm = 512
    hpt = tm // HALO_ROWS
    full = lambda shape: pl.BlockSpec(shape, lambda i: (0,) * len(shape))
    stacked = lambda shape: pl.BlockSpec((None,) + shape, lambda i: (layer,) + (0,) * len(shape))
    row_spec = pl.BlockSpec((tm, D_M), lambda i: (i, 0))
    head_sq = (N_HEADS_M, HEAD_DIM_M, HEAD_DIM_M)
    wk_t = jnp.swapaxes(w_qk_m[layer, 1], -1, -2)
    return pl.pallas_call(
        _mlstm_prep_kernel,
        out_shape=(jax.ShapeDtypeStruct((s, D_M), BF16), jax.ShapeDtypeStruct((s, D_M), BF16),
                   jax.ShapeDtypeStruct((D_M, s), BF16), jax.ShapeDtypeStruct((s, D_M), BF16),
                   jax.ShapeDtypeStruct((s, 2 * N_HEADS_M), F32), jax.ShapeDtypeStruct((2 * N_HEADS_M, s), F32)),
        grid=(s // tm,),
        in_specs=[
            row_spec,
            pl.BlockSpec((HALO_ROWS, D_M), lambda i: (jnp.maximum(i * hpt - 1, 0), 0)),
            full((CONV_K, D_M)), full((1, D_M)),
            stacked((2,) + head_sq), full(head_sq), stacked(head_sq),
            full((3 * D_M, 2 * N_HEADS_M)), full((2 * N_HEADS_M, 3 * D_M)),
            full((1, 2 * N_HEADS_M)), full((2 * N_HEADS_M, 1)),
        ],
        out_specs=(row_spec, row_spec, pl.BlockSpec((D_M, tm), lambda i: (0, i)), row_spec,
                   pl.BlockSpec((tm, 2 * N_HEADS_M), lambda i: (i, 0)),
                   pl.BlockSpec((2 * N_HEADS_M, tm), lambda i: (0, i))),
        compiler_params=_params(("arbitrary",), 32),
        name="mlstm_prep",
    )(xm, xm, conv_w, conv_b.reshape(1, D_M), w_qk_m, wk_t, w_v_m, w_if, w_if.T,
      b_if.reshape(1, -1), b_if.reshape(-1, 1))


MLSTM_STEP_CHUNKS = 4


def _mlstm_scan_kernel(q_ref, k_ref, kt_ref, v_ref, gc_ref, gr_ref, z_ref, g_ref, y_ref, c_ref, m_ref, *, chunks):
    hm, hd, lc = N_HEADS_M, HEAD_DIM_M, MLSTM_CHUNK

    @pl.when(pl.program_id(0) == 0)
    def _init():
        c_ref[...] = jnp.zeros_like(c_ref)
        m_ref[...] = jnp.zeros_like(m_ref)

    ri = lax.broadcasted_iota(I32, (lc, lc), 0)
    ci = lax.broadcasted_iota(I32, (lc, lc), 1)
    causal = ri >= ci
    ones = jnp.ones((lc, hd), BF16)
    heads = range(hm)
    hsl = [slice(h * hd, (h + 1) * hd) for h in heads]
    m_state = [m_ref[h:h + 1, 0:1] for h in heads]
    for c in range(chunks):
        rs = slice(c * lc, (c + 1) * lc)
        gc = gc_ref[rs, :]
        gr = gr_ref[:, rs]
        qs = [q_ref[rs, hsl[h]] for h in heads]
        v1 = [jnp.concatenate([v_ref[rs, hsl[h]], ones], axis=1) for h in heads]
        i_row = [gr[h:h + 1, :] for h in heads]
        b_row = [gr[hm + h:hm + h + 1, :] for h in heads]
        b_col = [gc[:, hm + h:hm + h + 1] for h in heads]
        qk = [_dot_nt(qs[h], k_ref[rs, hsl[h]]) for h in heads]
        c_prev = [c_ref[h] for h in heads]
        qc = [_dot(qs[h], c_prev[h].astype(BF16)) for h in heads]
        dm = [jnp.where(causal, b_col[h] - b_row[h] + i_row[h], NEG) for h in heads]
        inter = [b_col[h] + m_state[h] for h in heads]
        m_loc = [jnp.maximum(inter[h], jnp.max(dm[h], axis=-1, keepdims=True)) for h in heads]
        sc = [(qk[h] * jnp.exp(dm[h] - m_loc[h])).astype(BF16) for h in heads]
        both = [jnp.exp(inter[h] - m_loc[h]) * qc[h] + _dot(sc[h], v1[h]) for h in heads]
        hval = [both[h][:, :hd] / jnp.maximum(jnp.abs(both[h][:, hd:]), jnp.exp(-m_loc[h])) for h in heads]

        b_last = [b_row[h][:, lc - 1:lc] for h in heads]
        g_row = [b_last[h] - b_row[h] + i_row[h] for h in heads]
        m_new = [jnp.maximum(b_last[h] + m_state[h], jnp.max(g_row[h], axis=-1, keepdims=True)) for h in heads]
        ktw = [(kt_ref[hsl[h], rs].astype(F32) * jnp.exp(g_row[h] - m_new[h])).astype(BF16) for h in heads]
        for h in heads:
            c_ref[h] = jnp.exp(b_last[h] + m_state[h] - m_new[h]) * c_prev[h] + _dot(ktw[h], v1[h])
        m_state = m_new

        mu = [jnp.mean(hval[h], axis=-1, keepdims=True) for h in heads]
        cen = [hval[h] - mu[h] for h in heads]
        var = [jnp.mean(cen[h] * cen[h], axis=-1, keepdims=True) for h in heads]
        for h in heads:
            hn = cen[h] * lax.rsqrt(var[h] + LN_EPS) * g_ref[:, hsl[h]]
            y_ref[rs, hsl[h]] = (_sigmoid(z_ref[rs, hsl[h]].astype(F32)) * hn).astype(y_ref.dtype)
    for h in heads:
        m_ref[h:h + 1, :] = jnp.broadcast_to(m_state[h], (1, LANES))


def _mlstm_scan(q, k, kt, v, gc, gr, z, m_norm_g):
    s = q.shape[0]
    chunks = MLSTM_STEP_CHUNKS
    tm = chunks * MLSTM_CHUNK
    row_spec = pl.BlockSpec((tm, D_M), lambda i: (i, 0))
    return pl.pallas_call(
        functools.partial(_mlstm_scan_kernel, chunks=chunks),
        out_shape=jax.ShapeDtypeStruct((s, D_M), BF16),
        grid=(s // tm,),
        in_specs=[row_spec, row_spec, pl.BlockSpec((D_M, tm), lambda i: (0, i)), row_spec,
                  pl.BlockSpec((tm, 2 * N_HEADS_M), lambda i: (i, 0)),
                  pl.BlockSpec((2 * N_HEADS_M, tm), lambda i: (0, i)),
                  row_spec,
                  pl.BlockSpec((1, D_M), lambda i: (0, 0))],
        out_specs=row_spec,
        scratch_shapes=[pltpu.VMEM((N_HEADS_M, HEAD_DIM_M, 2 * HEAD_DIM_M), F32),
                        pltpu.VMEM((8, LANES), F32)],
        compiler_params=_params(("arbitrary",), 32),
        name="mlstm_scan",
    )(q, k, kt, v, gc, gr, z, m_norm_g.reshape(1, D_M))


def _merge_kernel(x_ref, o1_ref, o2_ref, o3_ref, l1_ref, l2_ref, l3_ref, ym_ref,
                  wg_ref, bg_ref, wa_ref, wm_ref, wo_ref, lng_ref, lnb_ref, out_ref,
                  wgb_ref, wab_ref, wmb_ref, wob_ref):
    @pl.when(pl.program_id(0) == 0)
    def _cast_weights():
        wgb_ref[...] = wg_ref[...].astype(BF16)
        wab_ref[...] = wa_ref[...].astype(BF16)
        wmb_ref[...] = wm_ref[...].astype(BF16)
        wob_ref[...] = wo_ref[...].astype(BF16)

    x = x_ref[...]
    xb = x.astype(BF16)
    stats = (l1_ref[...], l2_ref[...], l3_ref[...])
    mx = jnp.maximum(jnp.maximum(stats[0], stats[1]), stats[2])
    es = [jnp.exp2(st - mx) for st in stats]
    ls = [pltpu.roll(st, LANES - N_HEADS_A, 1) for st in stats]
    den = es[0] * ls[0] + es[1] * ls[1] + es[2] * ls[2]
    head_lane = lax.broadcasted_iota(I32, den.shape, 1) < N_HEADS_A
    inv = jnp.where(head_lane, 1.0 / den, 0.0)
    hrow = lax.broadcasted_iota(I32, (LANES, D_A), 0)
    hcol = lax.broadcasted_iota(I32, (LANES, D_A), 1) // HEAD_DIM_A
    expand = (hrow == hcol).astype(BF16)
    ya = jnp.zeros((x.shape[0], D_A), F32)
    for e, o_ref in zip(es, (o1_ref, o2_ref, o3_ref)):
        w = sum(_dot(part, expand) for part in _split3(e * inv)[:2])
        ya = ya + w * o_ref[...].astype(F32)

    gate = _sigmoid(_dot(xb, wgb_ref[...]) + bg_ref[...])
    merged = (gate[:, :D_MODEL] * _dot(ya.astype(BF16), wab_ref[...])
              + gate[:, D_MODEL:] * _dot(ym_ref[...], wmb_ref[...]))
    y = _dot(merged.astype(BF16), wob_ref[...])
    out_ref[...] = _layer_norm(ALPHA * x + y, lng_ref[...], lnb_ref[...])


def _merge(x, outs, lses, ym, w_gate, b_gate, w_br_a, w_br_m, w_o, ln_g, ln_b, layer):
    s = x.shape[0]
    tm = 512
    res = lambda shape: pl.BlockSpec((None,) + shape, lambda i: (layer, 0, 0), pipeline_mode=pl.Buffered(1))
    small = lambda n: pl.BlockSpec((1, n), lambda i: (0, 0))
    rows = lambda n: pl.BlockSpec((tm, n), lambda i: (i, 0))
    return pl.pallas_call(
        _merge_kernel,
        out_shape=jax.ShapeDtypeStruct((s, D_MODEL), F32),
        grid=(s // tm,),
        in_specs=[rows(D_MODEL), rows(D_A), rows(D_A), rows(D_A), rows(LANES), rows(LANES), rows(LANES), rows(D_M),
                  res((D_MODEL, 2 * D_MODEL)), small(2 * D_MODEL), res((D_A, D_MODEL)), res((D_M, D_MODEL)),
                  res((D_MODEL, D_MODEL)), small(D_MODEL), small(D_MODEL)],
        out_specs=rows(D_MODEL),
        scratch_shapes=[pltpu.VMEM((D_MODEL, 2 * D_MODEL), BF16), pltpu.VMEM((D_A, D_MODEL), BF16),
                        pltpu.VMEM((D_M, D_MODEL), BF16), pltpu.VMEM((D_MODEL, D_MODEL), BF16)],
        compiler_params=_params(("arbitrary",), 56),
        name="merge",
    )(x, *outs, *lses, ym, w_gate, b_gate.reshape(1, -1), w_br_a, w_br_m, w_o,
      ln_g.reshape(1, -1), ln_b.reshape(1, -1))


def _token_mixer_layer(x, biases, w_in, w_gate, b_gate, conv_w, conv_b, w_qk_m, w_v_m, w_if, b_if, m_norm_g,
                       w_br_a, w_br_m, w_o, ln_g, ln_b, layer):
    (q, k, v, xm, zm), perm = _in_proj(x, w_in, layer)
    outs, stats = [], []
    for (_, dilation), bias in zip(ATTN_PATTERNS, biases):
        qd, kd, vd = (q, k, v) if dilation == 1 else perm[DILATIONS.index(dilation)]
        o, st = _attn_pattern(qd, kd, vd, bias, dilation)
        outs.append(o)
        stats.append(st)
    qm, km, ktm, vm, gc, gr = _mlstm_prep(xm, conv_w, conv_b, w_qk_m, w_v_m, w_if, b_if, layer)
    ym = _mlstm_scan(qm, km, ktm, vm, gc, gr, zm, m_norm_g)
    return _merge(x, outs, stats, ym, w_gate, b_gate, w_br_a, w_br_m, w_o, ln_g, ln_b, layer)


def _load_cast(chunks, stage_ref, sem_ref):
    copies = [pltpu.make_async_copy(src, stage_ref.at[k % 2], sem_ref.at[k % 2]) for k, (src, _) in enumerate(chunks)]
    copies[0].start()
    for k, (_, dst) in enumerate(chunks):
        if k + 1 < len(chunks):
            copies[k + 1].start()
        copies[k].wait()
        dst[...] = stage_ref[k % 2].astype(BF16)


def _ffn_kernel(x_ref, w13_hbm, w2_hbm, lng_ref, lnb_ref, out_ref, w13b_ref, w2b_ref, acc_ref, st13_ref, st2_ref,
                sem_ref, *, j):
    fc = FFN_FF_CHUNK

    @pl.when(pl.program_id(0) == 0)
    def _load_weights():
        cols = lambda c: pl.ds(c * fc, fc)
        _load_cast([(w13_hbm.at[j, :, cols(c)], w13b_ref.at[:, cols(c)]) for c in range(2 * D_FF // fc)],
                   st13_ref, sem_ref)
        _load_cast([(w2_hbm.at[j, cols(c), :], w2b_ref.at[cols(c), :]) for c in range(D_FF // fc)],
                   st2_ref, sem_ref)

    xb = x_ref[...].astype(BF16)
    for c in range(D_FF // fc):
        a = _dot(xb, w13b_ref[:, c * fc:(c + 1) * fc])
        g = _dot(xb, w13b_ref[:, D_FF + c * fc:D_FF + (c + 1) * fc])
        y = _dot((a * _sigmoid(a) * g).astype(BF16), w2b_ref[c * fc:(c + 1) * fc, :])
        if c == 0:
            acc_ref[...] = y
        else:
            acc_ref[...] += y
    out_ref[...] = _layer_norm(ALPHA * x_ref[...] + acc_ref[...], lng_ref[...], lnb_ref[...])


def _dense_ffn(x, w13, w2, ln_g, ln_b, j):
    s = x.shape[0]
    tm = min(FFN_ROW_TILE, s)
    fc = FFN_FF_CHUNK
    small = pl.BlockSpec((1, D_MODEL), lambda i: (0, 0))
    return pl.pallas_call(
        functools.partial(_ffn_kernel, j=j),
        out_shape=jax.ShapeDtypeStruct((s, D_MODEL), F32),
        grid=(s // tm,),
        in_specs=[pl.BlockSpec((tm, D_MODEL), lambda i: (i, 0)),
                  pl.BlockSpec(memory_space=pl.ANY), pl.BlockSpec(memory_space=pl.ANY),
                  small, small],
        out_specs=pl.BlockSpec((tm, D_MODEL), lambda i: (i, 0)),
        scratch_shapes=[pltpu.VMEM((D_MODEL, 2 * D_FF), BF16), pltpu.VMEM((D_FF, D_MODEL), BF16),
                        pltpu.VMEM((tm, D_MODEL), F32),
                        pltpu.VMEM((2, D_MODEL, fc), F32), pltpu.VMEM((2, fc, D_MODEL), F32),
                        pltpu.SemaphoreType.DMA((2,))],
        compiler_params=_params(("arbitrary",), 56),
        name="dense_ffn",
    )(x, w13, w2, ln_g.reshape(1, -1), ln_b.reshape(1, -1))


def _router_kernel(x_ref, rw_ref, rb_ref, gate_ref, rank_ref, cnt_ref, xb_ref, carry_ref):
    tm = x_ref.shape[0]
    ne = N_EXPERTS

    @pl.when(pl.program_id(0) == 0)
    def _init():
        carry_ref[...] = jnp.zeros_like(carry_ref)

    x = x_ref[...]
    xb_ref[...] = x.astype(BF16)
    xs = _split3(x)
    ws = _split3(rw_ref[...])
    logits = rb_ref[...] + sum(_dot(xs[a], ws[b]) for a, b in ((1, 0), (0, 1), (0, 0)))
    lane = lax.broadcasted_iota(I32, (tm, ne), 1)
    v1 = jnp.max(logits, axis=-1, keepdims=True)
    i1 = jnp.min(jnp.where(logits == v1, lane, ne), axis=-1, keepdims=True)
    rest = jnp.where(lane == i1, -jnp.inf, logits)
    v2 = jnp.max(rest, axis=-1, keepdims=True)
    i2 = jnp.min(jnp.where(rest == v2, lane, ne), axis=-1, keepdims=True)
    e2 = jnp.exp(v2 - v1)
    den = 1.0 + e2
    sel1, sel2 = lane == i1, lane == i2
    gate_ref[...] = jnp.where(sel1, 1.0 / den, 0.0) + jnp.where(sel2, e2 / den, 0.0)
    sel = jnp.where(sel1 | sel2, 1.0, 0.0)
    ri = lax.broadcasted_iota(I32, (tm, tm), 0)
    ci = lax.broadcasted_iota(I32, (tm, tm), 1)
    before = (ri > ci).astype(BF16)
    carry = carry_ref[0:1, 0:ne]
    rank = _dot(before, sel.astype(BF16)) + carry
    rank_ref[...] = jnp.where(sel > 0.0, rank, -1.0)
    total = carry + jnp.sum(sel, axis=0, keepdims=True)
    carry_ref[0:1, 0:ne] = total
    cnt_ref[...] = total


def _router(x, router_w, router_b):
    s = x.shape[0]
    tm = 512
    ne = N_EXPERTS
    return pl.pallas_call(
        _router_kernel,
        out_shape=(jax.ShapeDtypeStruct((s, ne), F32), jax.ShapeDtypeStruct((s, ne), F32),
                   jax.ShapeDtypeStruct((1, ne), F32), jax.ShapeDtypeStruct((s, D_MODEL), BF16)),
        grid=(s // tm,),
        in_specs=[pl.BlockSpec((tm, D_MODEL), lambda i: (i, 0)),
                  pl.BlockSpec((D_MODEL, ne), lambda i: (0, 0)),
                  pl.BlockSpec((1, ne), lambda i: (0, 0))],
        out_specs=(pl.BlockSpec((tm, ne), lambda i: (i, 0)), pl.BlockSpec((tm, ne), lambda i: (i, 0)),
                   pl.BlockSpec((1, ne), lambda i: (0, 0)), pl.BlockSpec((tm, D_MODEL), lambda i: (i, 0))),
        scratch_shapes=[pltpu.VMEM((8, LANES), F32)],
        compiler_params=_params(("arbitrary",), 32),
        name="moe_router",
    )(x, router_w, router_b.reshape(1, ne))


def _chunk_copy(src_hbm, buf_ref, sem_ref, chunk, slot):
    rows = buf_ref.shape[1]
    start = pl.multiple_of(chunk * rows, rows)
    return pltpu.make_async_copy(src_hbm.at[pl.ds(start, rows), :], buf_ref.at[slot], sem_ref.at[slot])


def _moe_gemm_kernel(te_ref, nu_ref, off_ref, ist_ref, clo_ref, pad_ref, xb_hbm, rank_ref, w1_ref, w3_ref, w2_ref,
                     y_ref, xs_ref, acc_ref, buf_ref, sem_ref):
    gr, ck = MOE_GATHER_ROWS, MOE_GATHER_TOKENS
    sub = MOE_GROUP_TILE // gr
    nbuf = buf_ref.shape[0]
    last_chunk = rank_ref.shape[1] - 1
    i, f = pl.program_id(0), pl.program_id(1)
    last_f = pl.num_programs(1) - 1

    def item(t, k):
        base = t * sub
        s = sum((k >= ist_ref[base + j]).astype(I32) for j in range(1, sub))
        return s, jnp.minimum(clo_ref[base + s] + (k - ist_ref[base + s]), last_chunk)

    def start(t, k):
        @pl.when(k < ist_ref[t * sub + sub])
        def _():
            _chunk_copy(xb_hbm, buf_ref, sem_ref, item(t, k)[1], (k - ist_ref[t * sub]) % nbuf).start()

    @pl.when(i < nu_ref[0])
    def _tile():
        @pl.when(f == 0)
        def _gather_rows():
            e = te_ref[i]
            base = i * sub
            k0, k1 = ist_ref[base], ist_ref[base + sub]
            k_real = k1 - pad_ref[i]
            acc_ref[...] = jnp.zeros_like(acc_ref)

            @pl.when(i == 0)
            def _first_tile():
                for d in range(nbuf):
                    start(i, k0 + d)

            def pair(t, carry):
                ks = (k0 + 2 * t, k0 + 2 * t + 1)
                its = [item(i, k) for k in ks]
                slots = [(k - k0) % nbuf for k in ks]
                for (_, chunk), slot in zip(its, slots):
                    _chunk_copy(xb_hbm, buf_ref, sem_ref, chunk, slot).wait()
                iota = lax.broadcasted_iota(I32, (gr, ck), 0).astype(F32)
                rank0 = [jnp.where(k < k_real, (base + s) * gr - off_ref[e], -(2 ** 30)).astype(F32)
                         for k, (s, _) in zip(ks, its)]
                hit = [(r0 + iota) == rank_ref[e, pl.ds(chunk, 1), :] for r0, (_, chunk) in zip(rank0, its)]
                ys = [_dot(jnp.where(h, 1.0, 0.0).astype(BF16), buf_ref[slot]) for h, slot in zip(hit, slots)]
                for (s, _), y in zip(its, ys):
                    acc_ref[pl.ds(pl.multiple_of(s * gr, gr), gr), :] += y
                for k in ks:
                    start(i, k + nbuf)
                return carry

            lax.fori_loop(0, (k1 - k0) // 2, pair, 0)

            @pl.when(i + 1 < nu_ref[0])
            def _prefetch_next_tile():
                for d in range(nbuf):
                    start(i + 1, k1 + d)

            xs_ref[...] = acc_ref[...].astype(BF16)
            acc_ref[...] = jnp.zeros_like(acc_ref)

        xb = xs_ref[...]
        a = _dot(xb, w1_ref[0].astype(BF16))
        g = _dot(xb, w3_ref[0].astype(BF16))
        hidden = (a * _sigmoid(a) * g).astype(BF16)
        acc_ref[...] += _dot(hidden, w2_ref[0].astype(BF16))

        @pl.when(f == last_f)
        def _finish():
            y_ref[...] = acc_ref[...].astype(y_ref.dtype)

    @pl.when(jnp.logical_and(i >= nu_ref[0], f == last_f))
    def _unused_tile():
        y_ref[...] = jnp.zeros_like(y_ref)


def _combine_kernel(off_ref, cn_ref, cie_ref, cic_ref, x_ref, ys_hbm, rank_ref, gate_ref, lng_ref, lnb_ref,
                    out_ref, acc_ref, buf_ref, sem_ref, *, slots):
    tt, cr = MOE_COMBINE_TOKENS, MOE_COMBINE_ROWS
    j = pl.program_id(0)
    n = cn_ref[j]
    base = j * slots
    acc_ref[...] = jnp.zeros_like(acc_ref)
    nbuf = buf_ref.shape[0]

    def start(k):
        @pl.when(k < n)
        def _():
            _chunk_copy(ys_hbm, buf_ref, sem_ref, cic_ref[base + jnp.minimum(k, slots - 1)], k % nbuf).start()

    for d in range(nbuf):
        start(d)

    lane = lax.broadcasted_iota(I32, (tt, N_EXPERTS), 1)
    iota = lax.broadcasted_iota(I32, (tt, cr), 1).astype(F32)

    def pair(t, carry):
        ks = (2 * t, 2 * t + 1)
        es = [cie_ref[base + k] for k in ks]
        chunks = [cic_ref[base + k] for k in ks]
        for k, chunk in zip(ks, chunks):
            _chunk_copy(ys_hbm, buf_ref, sem_ref, chunk, k % nbuf).wait()
        rank = [jnp.max(jnp.where(lane == e, rank_ref[...], -1.0), axis=-1, keepdims=True) for e in es]
        gate = [jnp.sum(jnp.where(lane == e, gate_ref[...], 0.0), axis=-1, keepdims=True) for e in es]
        pos = [jnp.where(r >= 0.0, r + off_ref[jnp.maximum(e, 0)].astype(F32), -1.0)
               for r, e in zip(rank, es)]
        ys = [_dot(jnp.where(p == (chunk * cr).astype(F32) + iota, 1.0, 0.0).astype(BF16), buf_ref[k % nbuf])
              for p, chunk, k in zip(pos, chunks, ks)]
        acc_ref[...] += gate[0] * ys[0] + gate[1] * ys[1]
        for k in ks:
            start(k + nbuf)
        return carry

    lax.fori_loop(0, n // 2, pair, 0)
    out_ref[...] = _layer_norm(ALPHA * x_ref[...] + acc_ref[...], lng_ref[...], lnb_ref[...])


def _moe_layer(x, router_w, router_b, w13, w2, ln_g, ln_b):
    s = x.shape[0]
    ne, gt, gr, ck = N_EXPERTS, MOE_GROUP_TILE, MOE_GATHER_ROWS, MOE_GATHER_TOKENS
    tt, cr = MOE_COMBINE_TOKENS, MOE_COMBINE_ROWS
    sub = gt // gr
    nck = s // ck
    n_group_tiles = (TOP_K * s) // gt + ne
    n_sub = n_group_tiles * sub

    gate, rank, counts, xb = _router(x, router_w, router_b)

    cnt = counts[0].astype(I32)
    padded = ((cnt + gt - 1) // gt) * gt
    off = (jnp.cumsum(padded) - padded).astype(I32)
    n_used = (jnp.sum(padded) // gt).astype(I32)
    tile_expert = jnp.clip(
        jnp.searchsorted(jnp.cumsum(padded), jnp.arange(n_group_tiles, dtype=I32) * gt, side="right"), 0, ne - 1
    ).astype(I32)
    routed = (rank >= 0.0).reshape(nck, ck, ne)
    cum = jnp.cumsum(jnp.sum(routed, axis=1), axis=0).astype(I32).T

    u = jnp.arange(n_sub, dtype=I32)
    e_u = tile_expert[u // sub]
    r0 = u * gr - off[e_u]
    r1 = jnp.minimum(r0 + gr, cnt[e_u]) - 1
    live = (u // sub < n_used) & (r0 < cnt[e_u])
    cum_u = cum[e_u]
    c_lo = jnp.sum(cum_u <= r0[:, None], axis=1).astype(I32)
    c_hi = jnp.sum(cum_u <= r1[:, None], axis=1).astype(I32)
    n_items = jnp.where(live, c_hi - c_lo + 1, 0)
    item_pad = jnp.sum(n_items.reshape(n_group_tiles, sub), axis=1).astype(I32) % 2
    n_items = n_items + jnp.where(u % sub == sub - 1, item_pad[u // sub], 0)
    item_start = jnp.concatenate([jnp.zeros((1,), I32), jnp.cumsum(n_items).astype(I32)])
    c_lo = jnp.where(live, c_lo, 0)

    fc = MOE_FF_CHUNK
    nf = D_FF_E // fc
    tile_of = lambda i, nu: jnp.minimum(i, nu[0] - 1)
    chunk_of = lambda i, f, nu: jnp.where(i < nu[0], f, nf - 1)
    ys = pl.pallas_call(
        _moe_gemm_kernel,
        out_shape=jax.ShapeDtypeStruct((n_group_tiles * gt, D_MODEL), BF16),
        grid_spec=pltpu.PrefetchScalarGridSpec(
            num_scalar_prefetch=6,
            grid=(n_group_tiles, nf),
            in_specs=[pl.BlockSpec(memory_space=pl.ANY),
                      pl.BlockSpec((ne, nck, ck), lambda i, f, te, nu, *_: (0, 0, 0)),
                      pl.BlockSpec((1, D_MODEL, fc),
                                   lambda i, f, te, nu, *_: (te[tile_of(i, nu)], 0, chunk_of(i, f, nu))),
                      pl.BlockSpec((1, D_MODEL, fc),
                                   lambda i, f, te, nu, *_: (te[tile_of(i, nu)], 0, nf + chunk_of(i, f, nu))),
                      pl.BlockSpec((1, fc, D_MODEL),
                                   lambda i, f, te, nu, *_: (te[tile_of(i, nu)], chunk_of(i, f, nu), 0))],
            out_specs=pl.BlockSpec((gt, D_MODEL), lambda i, f, *_: (i, 0)),
            scratch_shapes=[pltpu.VMEM((gt, D_MODEL), BF16), pltpu.VMEM((gt, D_MODEL), F32),
                            pltpu.VMEM((MOE_DMA_DEPTH, ck, D_MODEL), BF16),
                            pltpu.SemaphoreType.DMA((MOE_DMA_DEPTH,))]),
        compiler_params=_params(("arbitrary", "arbitrary"), 58),
        name="moe_gemm",
    )(tile_expert, n_used.reshape(1), off, item_start, c_lo, item_pad, xb, rank.T.reshape(ne, nck, ck), w13, w13, w2)

    njc = s // tt
    per = tt // ck
    incl = cum[:, per - 1::per]
    lo = off[:, None] + jnp.concatenate([jnp.zeros((ne, 1), I32), incl[:, :-1]], axis=1)
    hi = off[:, None] + incl
    first, last = lo // cr, (hi - 1) // cr
    max_chunks = tt // cr + 1
    kk = jnp.arange(max_chunks, dtype=I32)
    slot_chunk = (first[:, :, None] + kk).transpose(1, 0, 2).reshape(njc, -1)
    slot_live = ((hi > lo)[:, :, None] & (first[:, :, None] + kk <= last[:, :, None])).transpose(1, 0, 2)
    slot_live = slot_live.reshape(njc, -1)
    slots = ne * max_chunks
    slot_expert = jnp.broadcast_to(jnp.repeat(jnp.arange(ne, dtype=I32), max_chunks)[None, :], (njc, slots))
    dest = jnp.cumsum(slot_live, axis=1) - 1
    place = slot_live[:, :, None] & (dest[:, :, None] == jnp.arange(slots)[None, None, :])
    item_chunk = jnp.sum(jnp.where(place, slot_chunk[:, :, None], 0), axis=1).astype(I32).reshape(-1)
    item_expert = jnp.sum(jnp.where(place, slot_expert[:, :, None], 0), axis=1).astype(I32)
    item_count = jnp.sum(slot_live, axis=1).astype(I32)
    item_expert = jnp.where(jnp.arange(slots)[None, :] < item_count[:, None], item_expert, -1).reshape(-1)
    item_count = item_count + item_count % 2

    small = pl.BlockSpec((1, D_MODEL), lambda j, *_: (0, 0))
    return pl.pallas_call(
        functools.partial(_combine_kernel, slots=slots),
        out_shape=jax.ShapeDtypeStruct((s, D_MODEL), F32),
        grid_spec=pltpu.PrefetchScalarGridSpec(
            num_scalar_prefetch=4,
            grid=(njc,),
            in_specs=[pl.BlockSpec((tt, D_MODEL), lambda j, *_: (j, 0)),
                      pl.BlockSpec(memory_space=pl.ANY),
                      pl.BlockSpec((tt, ne), lambda j, *_: (j, 0)),
                      pl.BlockSpec((tt, ne), lambda j, *_: (j, 0)),
                      small, small],
            out_specs=pl.BlockSpec((tt, D_MODEL), lambda j, *_: (j, 0)),
            scratch_shapes=[pltpu.VMEM((tt, D_MODEL), F32), pltpu.VMEM((MOE_DMA_DEPTH, cr, D_MODEL), BF16),
                            pltpu.SemaphoreType.DMA((MOE_DMA_DEPTH,))]),
        compiler_params=_params(("arbitrary",), 32),
        name="moe_combine",
    )(off, item_count, item_expert, item_chunk, x, ys, rank, gate, ln_g.reshape(1, -1), ln_b.reshape(1, -1))


def kernel(x, rel_bias, w_in, w_gate, b_gate, conv_w, conv_b, w_qk_m, w_v_m, w_if, b_if, m_norm_g, w_br_a, w_br_m,
           w_o, ln_g, ln_b, ffn_w13, ffn_w2, router_w, router_b, exp_w13, exp_w2):
    batch, seq, _ = x.shape
    assert batch == 1
    h = x.reshape(seq, D_MODEL)
    biases = [_attn_bias(rel_bias, window, dilation) for window, dilation in ATTN_PATTERNS]
    for l in range(DEPTH):
        h = _token_mixer_layer(h, biases, w_in, w_gate, b_gate[l], conv_w[l], conv_b[l], w_qk_m, w_v_m, w_if[l],
                               b_if[l], m_norm_g[l], w_br_a, w_br_m, w_o, ln_g[l, 0], ln_b[l, 0], l)
        j = l // 2
        if l % 2 == 0:
            h = _dense_ffn(h, ffn_w13, ffn_w2, ln_g[l, 1], ln_b[l, 1], j)
        else:
            h = _moe_layer(h, router_w[j], router_b[j], exp_w13[j], exp_w2[j], ln_g[l, 1], ln_b[l, 1])
    return h.reshape(batch, seq, D_MODEL)
```

```python
import functools
import math

import jax
import jax.numpy as jnp
from jax import lax
from jax.experimental import pallas as pl
from jax.experimental.pallas import tpu as pltpu

F32 = jnp.float32
BF16 = jnp.bfloat16
I32 = jnp.int32

D_MODEL = 1024
DEPTH = 2
N_HEADS_A = 8
HEAD_DIM_A = 64
D_A = N_HEADS_A * HEAD_DIM_A
ATTN_PATTERNS = ((128, 1), (512, 4), (2048, 16))
ATTN_BLOCK = 128
NUM_BUCKETS = 32
MAX_DISTANCE = 2048
N_HEADS_M = 4
HEAD_DIM_M = 128
D_M = N_HEADS_M * HEAD_DIM_M
CONV_K = 4
MLSTM_CHUNK = 128
N_PROJ = 5
P_IN = 3 * D_A + 2 * D_M
D_FF = 2816
N_EXPERTS = 8
TOP_K = 2
D_FF_E = 3584
ALPHA = (2.0 * DEPTH) ** 0.25
LN_EPS = 1e-5

NEG = -1e30
LOG2E = math.log2(math.e)
LANES = 128
HALO_ROWS = 16
MIB = 1024 * 1024

MOE_GROUP_TILE = 1024
MOE_FF_CHUNK = 512
MOE_GATHER_ROWS = 256
MOE_GATHER_TOKENS = 256
MOE_COMBINE_TOKENS = 512
MOE_COMBINE_ROWS = 256
MOE_DMA_DEPTH = 6
MERGE_COL_CHUNK = 256
MERGE_ROW_TILE = 1024
MERGE_LOAD_ROWS = 256
FFN_ROW_TILE = 1024
FFN_FF_CHUNK = 256


def _params(sem, vmem_mib):
    return pltpu.CompilerParams(dimension_semantics=sem, vmem_limit_bytes=vmem_mib * MIB)


def _sigmoid(x):
    return 1.0 / (1.0 + jnp.exp(-x))


def _layer_norm(r, g, b):
    mu = jnp.mean(r, axis=-1, keepdims=True)
    c = r - mu
    var = jnp.mean(c * c, axis=-1, keepdims=True)
    return c * lax.rsqrt(var + LN_EPS) * g + b


def _split3(a):
    hi = a.astype(BF16)
    r1 = a - hi.astype(F32)
    mid = r1.astype(BF16)
    lo = (r1 - mid.astype(F32)).astype(BF16)
    return hi, mid, lo


def _dot(a, b):
    return jnp.dot(a, b, preferred_element_type=F32)


def _dot_nt(a, b):
    return lax.dot_general(a, b, (((1,), (1,)), ((), ())), preferred_element_type=F32)


def _dot_tn(a, b):
    return lax.dot_general(a, b, (((0,), (0,)), ((), ())), preferred_element_type=F32)


IN_PROJ_ROWS = 1024
DILATIONS = tuple(d for _, d in ATTN_PATTERNS if d > 1)


def _in_proj_kernel(x_ref, w_ref, *refs):
    nat = refs[:N_PROJ]
    perm = refs[N_PROJ:N_PROJ + 3 * len(DILATIONS)]
    wb_ref, y_ref = refs[-2:]

    @pl.when(pl.program_id(0) == 0)
    def _cast_weights():
        wb_ref[...] = w_ref[...].astype(BF16)

    xb = x_ref[...].astype(BF16)
    for j in range(N_PROJ):
        y = _dot(xb, wb_ref[:, j * D_A:(j + 1) * D_A])
        if j == 0:
            y = y * (HEAD_DIM_A ** -0.5 * LOG2E)
        nat[j][...] = y.astype(BF16)
        if j >= 3:
            continue
        for c in range(D_A // LANES):
            y_ref[c] = y[:, c * LANES:(c + 1) * LANES]
        for di, d in enumerate(DILATIONS):
            out = perm[di * 3 + j]
            tiles, _, rpc, _ = out.shape
            for t in range(tiles):
                for r in range(d):
                    for c in range(D_A // LANES):
                        out[t, r, :, c * LANES:(c + 1) * LANES] = (
                            y_ref[c, pl.ds(t * d * rpc + r, rpc, stride=d), :].astype(BF16))


def _in_proj(x, w_in, layer):
    s = x.shape[0]
    tm = IN_PROJ_ROWS
    blk = ATTN_BLOCK
    out_shape = [jax.ShapeDtypeStruct((s, D_A), BF16)] * N_PROJ
    out_specs = [pl.BlockSpec((tm, D_A), lambda i: (i, 0))] * N_PROJ
    for d in DILATIONS:
        tile = d * blk
        if tile <= tm:
            spec = pl.BlockSpec((tm // tile, d, blk, D_A), lambda i: (i, 0, 0, 0))
        else:
            parts = tile // tm
            spec = pl.BlockSpec((1, d, blk // parts, D_A), lambda i, parts=parts: (i // parts, 0, i % parts, 0))
        out_shape += [jax.ShapeDtypeStruct((s // tile, d, blk, D_A), BF16)] * 3
        out_specs += [spec] * 3
    outs = pl.pallas_call(
        _in_proj_kernel,
        out_shape=tuple(out_shape),
        grid=(s // tm,),
        in_specs=[
            pl.BlockSpec((tm, D_MODEL), lambda i: (i, 0)),
            pl.BlockSpec((None, D_MODEL, P_IN), lambda i: (layer, 0, 0), pipeline_mode=pl.Buffered(1)),
        ],
        out_specs=tuple(out_specs),
        scratch_shapes=[pltpu.VMEM((D_MODEL, P_IN), BF16), pltpu.VMEM((D_A // LANES, tm, LANES), F32)],
        compiler_params=_params(("arbitrary",), 56),
        name="in_proj",
    )(x, w_in)
    nat = outs[:N_PROJ]
    perm = [tuple(t.reshape(s, D_A) for t in outs[N_PROJ + 3 * i:N_PROJ + 3 * i + 3]) for i in range(len(DILATIONS))]
    return nat, perm


ATTN_STEP_BLOCKS = 16


def _rel_bucket(dist):
    exact = NUM_BUCKETS // 2
    d = jnp.maximum(dist, exact).astype(F32)
    log_b = exact + (jnp.log(d / exact) / math.log(MAX_DISTANCE / exact) * (NUM_BUCKETS - exact)).astype(I32)
    return jnp.where(dist < exact, dist, jnp.minimum(log_b, NUM_BUCKETS - 1))


def _attn_bias(rel_bias, window, dilation):
    blk = ATTN_BLOCK
    qi = jnp.arange(blk)[:, None]
    kj = jnp.arange(2 * blk)[None, :]
    rel = qi + blk - kj
    bucket = _rel_bucket(jnp.maximum(rel, 0) * dilation)
    onehot = (bucket[..., None] == jnp.arange(NUM_BUCKETS)).astype(F32)
    bias = jnp.einsum("qkb,bh->hqk", onehot, rel_bias.astype(F32), precision=lax.Precision.HIGHEST)
    mask = (rel >= 0) & (rel <= window // dilation)
    return jnp.where(mask[None], bias * LOG2E, NEG)


def _attn_kernel(q_ref, kp_ref, kc_ref, vp_ref, vc_ref, bias_ref, o_ref, st_ref, on_ref, sn_ref, *, dilation):
    blk = ATTN_BLOCK
    nb = ATTN_STEP_BLOCKS
    step = pl.program_id(0)
    kj = lax.broadcasted_iota(I32, (2 * blk, 2 * blk), 1)
    lane = lax.broadcasted_iota(I32, (blk, LANES), 1)
    ones = jnp.ones((2 * blk, LANES), BF16)
    pairs = range(N_HEADS_A // 2)

    def block(g, carry):
        cur = pl.multiple_of(g * blk, blk)
        in_tile = g >= dilation
        prev_c = pl.multiple_of(jnp.maximum(g - dilation, 0) * blk, blk)
        prev_p = pl.multiple_of(jnp.minimum(nb + g - dilation, nb - 1) * blk, blk)
        pen = jnp.where(jnp.logical_and(step == 0, g < dilation), NEG, 0.0).astype(F32)
        prev_pen = jnp.where(kj < blk, pen, 0.0)
        tile_i, cls = g // dilation, g % dilation
        dst = pl.ds(tile_i * (blk * dilation) + cls, blk, stride=dilation)

        def keys(cur_ref, prev_ref, ls):
            prev = jnp.where(in_tile, cur_ref[pl.ds(prev_c, blk), ls], prev_ref[pl.ds(prev_p, blk), ls])
            return jnp.concatenate([prev, cur_ref[pl.ds(cur, blk), ls]], axis=0)

        even = (lane // HEAD_DIM_A) == 0
        lss = [slice(hp * LANES, (hp + 1) * LANES) for hp in pairs]
        qb = [q_ref[pl.ds(cur, blk), lss[hp]] for hp in pairs]
        qcat = [jnp.concatenate([jnp.where(even, qb[hp], jnp.zeros_like(qb[hp])),
                                 jnp.where(even, jnp.zeros_like(qb[hp]), qb[hp])], axis=0) for hp in pairs]
        kb = [keys(kc_ref, kp_ref, lss[hp]) for hp in pairs]
        v1 = [jnp.concatenate([keys(vc_ref, vp_ref, lss[hp]), ones], axis=1) for hp in pairs]
        logits = [_dot_nt(qcat[hp], kb[hp]) + bias_ref[hp] + prev_pen for hp in pairs]
        m = [jnp.max(logits[hp], axis=-1, keepdims=True) for hp in pairs]
        p = [jnp.exp2(logits[hp] - m[hp]).astype(BF16) for hp in pairs]
        ol = [_dot(p[hp], v1[hp]) for hp in pairs]
        stats = jnp.zeros((blk, LANES), F32)
        for hp in pairs:
            for par in range(2):
                h, rs = 2 * hp + par, slice(par * blk, (par + 1) * blk)
                stats = jnp.where(lane == h, m[hp][rs], stats)
                stats = jnp.where(lane == N_HEADS_A + h, ol[hp][rs, LANES:], stats)
            on_ref[hp, dst, :] = jnp.where(even, ol[hp][:blk, :LANES], ol[hp][blk:, :LANES])
        sn_ref[dst, :] = stats
        return carry

    lax.fori_loop(0, nb, block, 0, unroll=2)
    for hp in range(N_HEADS_A // 2):
        o_ref[:, hp * LANES:(hp + 1) * LANES] = on_ref[hp].astype(o_ref.dtype)
    st_ref[...] = sn_ref[...]


def _attn_pattern(q, k, v, bias, dilation):
    s = q.shape[0]
    rows = ATTN_STEP_BLOCKS * ATTN_BLOCK
    cur = pl.BlockSpec((rows, D_A), lambda i: (i, 0))
    prev = pl.BlockSpec((rows, D_A), lambda i: (jnp.maximum(i - 1, 0), 0))
    return pl.pallas_call(
        functools.partial(_attn_kernel, dilation=dilation),
        out_shape=(jax.ShapeDtypeStruct((s, D_A), BF16), jax.ShapeDtypeStruct((s, LANES), F32)),
        grid=(s // rows,),
        in_specs=[cur, prev, cur, prev, cur,
                  pl.BlockSpec((N_HEADS_A // 2, 2 * ATTN_BLOCK, 2 * ATTN_BLOCK), lambda i: (0, 0, 0))],
        out_specs=(pl.BlockSpec((rows, D_A), lambda i: (i, 0)), pl.BlockSpec((rows, LANES), lambda i: (i, 0))),
        scratch_shapes=[pltpu.VMEM((D_A // LANES, rows, LANES), F32), pltpu.VMEM((rows, LANES), F32)],
        compiler_params=_params(("arbitrary",), 48),
        name=f"attn_d{dilation}",
    )(q, k, k, v, v, bias.reshape(N_HEADS_A // 2, 2 * ATTN_BLOCK, 2 * ATTN_BLOCK))


def _log_sigmoid(x):
    return jnp.minimum(x, 0.0) - jnp.log(1.0 + jnp.exp(-jnp.abs(x)))


def _mlstm_prep_kernel(xm_ref, halo_ref, cw_ref, cb_ref, wqk_ref, wkt_ref, wv_ref, wif_ref, wift_ref, bif_ref,
                       bift_ref, q_ref, k_ref, kt_ref, v_ref, gc_ref, gr_ref):
    tm = xm_ref.shape[0]
    hm, hd, lc = N_HEADS_M, HEAD_DIM_M, MLSTM_CHUNK
    xmb = xm_ref[...]
    halo = jnp.where(pl.program_id(0) == 0, 0.0, halo_ref[...].astype(F32))
    xx = jnp.concatenate([halo, xmb.astype(F32)], axis=0)
    conv = jnp.zeros((tm, D_M), F32) + cb_ref[...]
    for j in range(CONV_K):
        start = HALO_ROWS - (CONV_K - 1) + j
        conv = conv + cw_ref[j:j + 1, :] * xx[start:start + tm, :]
    xcb = (conv * _sigmoid(conv)).astype(BF16)

    qs, ks, vs = [], [], []
    for h in range(hm):
        sl = slice(h * hd, (h + 1) * hd)
        qs.append(_dot(xcb[:, sl], wqk_ref[0, h].astype(BF16)))
        ks.append(_dot(xcb[:, sl], wqk_ref[1, h].astype(BF16)))
        vs.append(_dot(xmb[:, sl], wv_ref[h].astype(BF16)))
        kt = _dot_nt(wkt_ref[h].astype(BF16), xcb[:, sl])
        kt_ref[sl, :] = (kt * (hd ** -0.5)).astype(BF16)
    q = jnp.concatenate(qs, axis=1)
    k = jnp.concatenate(ks, axis=1)
    v = jnp.concatenate(vs, axis=1)
    q_ref[...] = q.astype(BF16)
    k_ref[...] = (k * (hd ** -0.5)).astype(BF16)
    v_ref[...] = v.astype(BF16)

    qkv = jnp.concatenate([q, k, v], axis=1).astype(BF16)
    gates_c = _dot(qkv, wif_ref[...].astype(BF16)) + bif_ref[...]
    gates_r = _dot_nt(wift_ref[...].astype(BF16), qkv) + bift_ref[...]
    lane = lax.broadcasted_iota(I32, gates_c.shape, 1)
    row = lax.broadcasted_iota(I32, gates_r.shape, 0)
    gc_ref[...] = jnp.where(lane < hm, gates_c, _log_sigmoid(gates_c))
    gr_ref[...] = jnp.where(row < hm, gates_r, _log_sigmoid(gates_r))

    ri = lax.broadcasted_iota(I32, (lc, lc), 0)
    ci = lax.broadcasted_iota(I32, (lc, lc), 1)
    lower = (ri >= ci).astype(BF16)
    upper = (ri <= ci).astype(BF16)
    lane_c = lax.broadcasted_iota(I32, (lc, 2 * hm), 1)
    row_c = lax.broadcasted_iota(I32, (2 * hm, lc), 0)
    for c in range(tm // lc):
        rs = slice(c * lc, (c + 1) * lc)
        gcc = gc_ref[rs, :]
        grc = gr_ref[:, rs]
        cum_c = sum(_dot(lower, part) for part in _split3(gcc))
        cum_r = sum(_dot(part, upper) for part in _split3(grc))
        gc_ref[rs, :] = jnp.where(lane_c < hm, gcc, cum_c)
        gr_ref[:, rs] = jnp.where(row_c < hm, grc, cum_r)


def _mlstm_prep(xm, conv_w, conv_b, w_qk_m, w_v_m, w_if, b_if, layer):
    s = xm.shape[0]
    tm = 512
    hpt = tm // HALO_ROWS
    full = lambda shape: pl.BlockSpec(shape, lambda i: (0,) * len(shape))
    stacked = lambda shape: pl.BlockSpec((None,) + shape, lambda i: (layer,) + (0,) * len(shape))
    row_spec = pl.BlockSpec((tm, D_M), lambda i: (i, 0))
    head_sq = (N_HEADS_M, HEAD_DIM_M, HEAD_DIM_M)
    wk_t = jnp.swapaxes(w_qk_m[layer, 1], -1, -2)
    return pl.pallas_call(
        _mlstm_prep_kernel,
        out_shape=(jax.ShapeDtypeStruct((s, D_M), BF16), jax.ShapeDtypeStruct((s, D_M), BF16),
                   jax.ShapeDtypeStruct((D_M, s), BF16), jax.ShapeDtypeStruct((s, D_M), BF16),
                   jax.ShapeDtypeStruct((s, 2 * N_HEADS_M), F32), jax.ShapeDtypeStruct((2 * N_HEADS_M, s), F32)),
        grid=(s // tm,),
        in_specs=[
            row_spec,
            pl.BlockSpec((HALO_ROWS, D_M), lambda i: (jnp.maximum(i * hpt - 1, 0), 0)),
            full((CONV_K, D_M)), full((1, D_M)),
            stacked((2,) + head_sq), full(head_sq), stacked(head_sq),
            full((3 * D_M, 2 * N_HEADS_M)), full((2 * N_HEADS_M, 3 * D_M)),
            full((1, 2 * N_HEADS_M)), full((2 * N_HEADS_M, 1)),
        ],
        out_specs=(row_spec, row_spec, pl.BlockSpec((D_M, tm), lambda i: (0, i)), row_spec,
                   pl.BlockSpec((tm, 2 * N_HEADS_M), lambda i: (i, 0)),
                   pl.BlockSpec((2 * N_HEADS_M, tm), lambda i: (0, i))),
        compiler_params=_params(("arbitrary",), 32),
        name="mlstm_prep",
    )(xm, xm, conv_w, conv_b.reshape(1, D_M), w_qk_m, wk_t, w_v_m, w_if, w_if.T,
      b_if.reshape(1, -1), b_if.reshape(-1, 1))


MLSTM_STEP_CHUNKS = 4


def _mlstm_scan_kernel(q_ref, k_ref, kt_ref, v_ref, gc_ref, gr_ref, z_ref, g_ref, y_ref, c_ref, m_ref, *, chunks):
    hm, hd, lc = N_HEADS_M, HEAD_DIM_M, MLSTM_CHUNK

    @pl.when(pl.program_id(0) == 0)
    def _init():
        c_ref[...] = jnp.zeros_like(c_ref)
        m_ref[...] = jnp.zeros_like(m_ref)

    ri = lax.broadcasted_iota(I32, (lc, lc), 0)
    ci = lax.broadcasted_iota(I32, (lc, lc), 1)
    causal = ri >= ci
    ones = jnp.ones((lc, hd), BF16)
    heads = range(hm)
    hsl = [slice(h * hd, (h + 1) * hd) for h in heads]
    m_state = [m_ref[h:h + 1, 0:1] for h in heads]
    for c in range(chunks):
        rs = slice(c * lc, (c + 1) * lc)
        gc = gc_ref[rs, :]
        gr = gr_ref[:, rs]
        qs = [q_ref[rs, hsl[h]] for h in heads]
        v1 = [jnp.concatenate([v_ref[rs, hsl[h]], ones], axis=1) for h in heads]
        i_row = [gr[h:h + 1, :] for h in heads]
        b_row = [gr[hm + h:hm + h + 1, :] for h in heads]
        b_col = [gc[:, hm + h:hm + h + 1] for h in heads]
        qk = [_dot_nt(qs[h], k_ref[rs, hsl[h]]) for h in heads]
        c_prev = [c_ref[h] for h in heads]
        qc = [_dot(qs[h], c_prev[h].astype(BF16)) for h in heads]
        dm = [jnp.where(causal, b_col[h] - b_row[h] + i_row[h], NEG) for h in heads]
        inter = [b_col[h] + m_state[h] for h in heads]
        m_loc = [jnp.maximum(inter[h], jnp.max(dm[h], axis=-1, keepdims=True)) for h in heads]
        sc = [(qk[h] * jnp.exp(dm[h] - m_loc[h])).astype(BF16) for h in heads]
        both = [jnp.exp(inter[h] - m_loc[h]) * qc[h] + _dot(sc[h], v1[h]) for h in heads]
        hval = [both[h][:, :hd] / jnp.maximum(jnp.abs(both[h][:, hd:]), jnp.exp(-m_loc[h])) for h in heads]

        b_last = [b_row[h][:, lc - 1:lc] for h in heads]
        g_row = [b_last[h] - b_row[h] + i_row[h] for h in heads]
        m_new = [jnp.maximum(b_last[h] + m_state[h], jnp.max(g_row[h], axis=-1, keepdims=True)) for h in heads]
        ktw = [(kt_ref[hsl[h], rs].astype(F32) * jnp.exp(g_row[h] - m_new[h])).astype(BF16) for h in heads]
        for h in heads:
            c_ref[h] = jnp.exp(b_last[h] + m_state[h] - m_new[h]) * c_prev[h] + _dot(ktw[h], v1[h])
        m_state = m_new

        mu = [jnp.mean(hval[h], axis=-1, keepdims=True) for h in heads]
        cen = [hval[h] - mu[h] for h in heads]
        var = [jnp.mean(cen[h] * cen[h], axis=-1, keepdims=True) for h in heads]
        for h in heads:
            hn = cen[h] * lax.rsqrt(var[h] + LN_EPS) * g_ref[:, hsl[h]]
            y_ref[rs, hsl[h]] = (_sigmoid(z_ref[rs, hsl[h]].astype(F32)) * hn).astype(y_ref.dtype)
    for h in heads:
        m_ref[h:h + 1, :] = jnp.broadcast_to(m_state[h], (1, LANES))


def _mlstm_scan(q, k, kt, v, gc, gr, z, m_norm_g):
    s = q.shape[0]
    chunks = MLSTM_STEP_CHUNKS
    tm = chunks * MLSTM_CHUNK
    row_spec = pl.BlockSpec((tm, D_M), lambda i: (i, 0))
    return pl.pallas_call(
        functools.partial(_mlstm_scan_kernel, chunks=chunks),
        out_shape=jax.ShapeDtypeStruct((s, D_M), BF16),
        grid=(s // tm,),
        in_specs=[row_spec, row_spec, pl.BlockSpec((D_M, tm), lambda i: (0, i)), row_spec,
                  pl.BlockSpec((tm, 2 * N_HEADS_M), lambda i: (i, 0)),
                  pl.BlockSpec((2 * N_HEADS_M, tm), lambda i: (0, i)),
                  row_spec,
                  pl.BlockSpec((1, D_M), lambda i: (0, 0))],
        out_specs=row_spec,
        scratch_shapes=[pltpu.VMEM((N_HEADS_M, HEAD_DIM_M, 2 * HEAD_DIM_M), F32),
                        pltpu.VMEM((8, LANES), F32)],
        compiler_params=_params(("arbitrary",), 32),
        name="mlstm_scan",
    )(q, k, kt, v, gc, gr, z, m_norm_g.reshape(1, D_M))


def _merge_kernel(x_ref, o1_ref, o2_ref, o3_ref, l1_ref, l2_ref, l3_ref, ym_ref,
                  wg_hbm, bg_ref, wa_hbm, wm_hbm, wo_hbm, lng_ref, lnb_ref, out_ref,
                  wgb_ref, wab_ref, wmb_ref, wob_ref, stg_ref, st_ref, sem_ref, *, layer):
    @pl.when(pl.program_id(0) == 0)
    def _load_weights():
        rc = st_ref.shape[1]
        rows = lambda c: pl.ds(c * rc, rc)
        _load_cast([(wg_hbm.at[layer, rows(c), :], wgb_ref.at[rows(c), :]) for c in range(D_MODEL // rc)],
                   stg_ref, sem_ref)
        _load_cast([(src.at[layer, rows(c), :], dst.at[rows(c), :])
                    for src, dst in ((wa_hbm, wab_ref), (wm_hbm, wmb_ref), (wo_hbm, wob_ref))
                    for c in range(dst.shape[0] // rc)], st_ref, sem_ref)

    x = x_ref[...]
    xb = x.astype(BF16)
    stats = (l1_ref[...], l2_ref[...], l3_ref[...])
    mx = jnp.maximum(jnp.maximum(stats[0], stats[1]), stats[2])
    es = [jnp.exp2(st - mx) for st in stats]
    ls = [pltpu.roll(st, LANES - N_HEADS_A, 1) for st in stats]
    den = es[0] * ls[0] + es[1] * ls[1] + es[2] * ls[2]
    head_lane = lax.broadcasted_iota(I32, den.shape, 1) < N_HEADS_A
    inv = jnp.where(head_lane, 1.0 / den, 0.0)
    hrow = lax.broadcasted_iota(I32, (LANES, D_A), 0)
    hcol = lax.broadcasted_iota(I32, (LANES, D_A), 1) // HEAD_DIM_A
    expand = (hrow == hcol).astype(BF16)
    ya = jnp.zeros((x.shape[0], D_A), F32)
    for e, o_ref in zip(es, (o1_ref, o2_ref, o3_ref)):
        w = sum(_dot(part, expand) for part in _split3(e * inv)[:2])
        ya = ya + w * o_ref[...].astype(F32)

    yab, ymb = ya.astype(BF16), ym_ref[...]
    mc = MERGE_COL_CHUNK
    y = jnp.zeros((x.shape[0], D_MODEL), F32)
    for c in range(D_MODEL // mc):
        ca = slice(c * mc, (c + 1) * mc)
        cm = slice(D_MODEL + c * mc, D_MODEL + (c + 1) * mc)
        g_a = _sigmoid(_dot(xb, wgb_ref[:, ca]) + bg_ref[:, ca])
        g_m = _sigmoid(_dot(xb, wgb_ref[:, cm]) + bg_ref[:, cm])
        merged = g_a * _dot(yab, wab_ref[:, ca]) + g_m * _dot(ymb, wmb_ref[:, ca])
        y = y + _dot(merged.astype(BF16), wob_ref[ca, :])
    out_ref[...] = _layer_norm(ALPHA * x + y, lng_ref[...], lnb_ref[...])


def _merge(x, outs, lses, ym, w_gate, b_gate, w_br_a, w_br_m, w_o, ln_g, ln_b, layer):
    s = x.shape[0]
    tm = MERGE_ROW_TILE
    hbm = pl.BlockSpec(memory_space=pl.ANY)
    small = lambda n: pl.BlockSpec((1, n), lambda i: (0, 0))
    rows = lambda n: pl.BlockSpec((tm, n), lambda i: (i, 0))
    rc = MERGE_LOAD_ROWS
    return pl.pallas_call(
        functools.partial(_merge_kernel, layer=layer),
        out_shape=jax.ShapeDtypeStruct((s, D_MODEL), F32),
        grid=(s // tm,),
        in_specs=[rows(D_MODEL), rows(D_A), rows(D_A), rows(D_A), rows(LANES), rows(LANES), rows(LANES), rows(D_M),
                  hbm, small(2 * D_MODEL), hbm, hbm, hbm, small(D_MODEL), small(D_MODEL)],
        out_specs=rows(D_MODEL),
        scratch_shapes=[pltpu.VMEM((D_MODEL, 2 * D_MODEL), BF16), pltpu.VMEM((D_A, D_MODEL), BF16),
                        pltpu.VMEM((D_M, D_MODEL), BF16), pltpu.VMEM((D_MODEL, D_MODEL), BF16),
                        pltpu.VMEM((2, rc, 2 * D_MODEL), F32), pltpu.VMEM((2, rc, D_MODEL), F32),
                        pltpu.SemaphoreType.DMA((2,))],
        compiler_params=_params(("arbitrary",), 56),
        name="merge",
    )(x, *outs, *lses, ym, w_gate, b_gate.reshape(1, -1), w_br_a, w_br_m, w_o,
      ln_g.reshape(1, -1), ln_b.reshape(1, -1))


def _token_mixer_layer(x, biases, w_in, w_gate, b_gate, conv_w, conv_b, w_qk_m, w_v_m, w_if, b_if, m_norm_g,
                       w_br_a, w_br_m, w_o, ln_g, ln_b, layer):
    (q, k, v, xm, zm), perm = _in_proj(x, w_in, layer)
    outs, stats = [], []
    for (_, dilation), bias in zip(ATTN_PATTERNS, biases):
        qd, kd, vd = (q, k, v) if dilation == 1 else perm[DILATIONS.index(dilation)]
        o, st = _attn_pattern(qd, kd, vd, bias, dilation)
        outs.append(o)
        stats.append(st)
    qm, km, ktm, vm, gc, gr = _mlstm_prep(xm, conv_w, conv_b, w_qk_m, w_v_m, w_if, b_if, layer)
    ym = _mlstm_scan(qm, km, ktm, vm, gc, gr, zm, m_norm_g)
    return _merge(x, outs, stats, ym, w_gate, b_gate, w_br_a, w_br_m, w_o, ln_g, ln_b, layer)


def _load_cast(chunks, stage_ref, sem_ref):
    copies = [pltpu.make_async_copy(src, stage_ref.at[k % 2], sem_ref.at[k % 2]) for k, (src, _) in enumerate(chunks)]
    copies[0].start()
    for k, (_, dst) in enumerate(chunks):
        if k + 1 < len(chunks):
            copies[k + 1].start()
        copies[k].wait()
        dst[...] = stage_ref[k % 2].astype(BF16)


def _ffn_kernel(x_ref, w13_hbm, w2_hbm, lng_ref, lnb_ref, out_ref, w13b_ref, w2b_ref, acc_ref, st13_ref, st2_ref,
                sem_ref, *, j):
    fc = FFN_FF_CHUNK

    @pl.when(pl.program_id(0) == 0)
    def _load_weights():
        cols = lambda c: pl.ds(c * fc, fc)
        _load_cast([(w13_hbm.at[j, :, cols(c)], w13b_ref.at[:, cols(c)]) for c in range(2 * D_FF // fc)],
                   st13_ref, sem_ref)
        _load_cast([(w2_hbm.at[j, cols(c), :], w2b_ref.at[cols(c), :]) for c in range(D_FF // fc)],
                   st2_ref, sem_ref)

    xb = x_ref[...].astype(BF16)
    for c in range(D_FF // fc):
        a = _dot(xb, w13b_ref[:, c * fc:(c + 1) * fc])
        g = _dot(xb, w13b_ref[:, D_FF + c * fc:D_FF + (c + 1) * fc])
        y = _dot((a * _sigmoid(a) * g).astype(BF16), w2b_ref[c * fc:(c + 1) * fc, :])
        if c == 0:
            acc_ref[...] = y
        else:
            acc_ref[...] += y
    out_ref[...] = _layer_norm(ALPHA * x_ref[...] + acc_ref[...], lng_ref[...], lnb_ref[...])


def _dense_ffn(x, w13, w2, ln_g, ln_b, j):
    s = x.shape[0]
    tm = min(FFN_ROW_TILE, s)
    fc = FFN_FF_CHUNK
    small = pl.BlockSpec((1, D_MODEL), lambda i: (0, 0))
    return pl.pallas_call(
        functools.partial(_ffn_kernel, j=j),
        out_shape=jax.ShapeDtypeStruct((s, D_MODEL), F32),
        grid=(s // tm,),
        in_specs=[pl.BlockSpec((tm, D_MODEL), lambda i: (i, 0)),
                  pl.BlockSpec(memory_space=pl.ANY), pl.BlockSpec(memory_space=pl.ANY),
                  small, small],
        out_specs=pl.BlockSpec((tm, D_MODEL), lambda i: (i, 0)),
        scratch_shapes=[pltpu.VMEM((D_MODEL, 2 * D_FF), BF16), pltpu.VMEM((D_FF, D_MODEL), BF16),
                        pltpu.VMEM((tm, D_MODEL), F32),
                        pltpu.VMEM((2, D_MODEL, fc), F32), pltpu.VMEM((2, fc, D_MODEL), F32),
                        pltpu.SemaphoreType.DMA((2,))],
        compiler_params=_params(("arbitrary",), 56),
        name="dense_ffn",
    )(x, w13, w2, ln_g.reshape(1, -1), ln_b.reshape(1, -1))


def _router_kernel(x_ref, rw_ref, rb_ref, gate_ref, rank_ref, cnt_ref, xb_ref, carry_ref):
    tm = x_ref.shape[0]
    ne = N_EXPERTS

    @pl.when(pl.program_id(0) == 0)
    def _init():
        carry_ref[...] = jnp.zeros_like(carry_ref)

    x = x_ref[...]
    xb_ref[...] = x.astype(BF16)
    xs = _split3(x)
    ws = _split3(rw_ref[...])
    logits = rb_ref[...] + sum(_dot(xs[a], ws[b]) for a, b in ((1, 0), (0, 1), (0, 0)))
    lane = lax.broadcasted_iota(I32, (tm, ne), 1)
    v1 = jnp.max(logits, axis=-1, keepdims=True)
    i1 = jnp.min(jnp.where(logits == v1, lane, ne), axis=-1, keepdims=True)
    rest = jnp.where(lane == i1, -jnp.inf, logits)
    v2 = jnp.max(rest, axis=-1, keepdims=True)
    i2 = jnp.min(jnp.where(rest == v2, lane, ne), axis=-1, keepdims=True)
    e2 = jnp.exp(v2 - v1)
    den = 1.0 + e2
    sel1, sel2 = lane == i1, lane == i2
    gate_ref[...] = jnp.where(sel1, 1.0 / den, 0.0) + jnp.where(sel2, e2 / den, 0.0)
    sel = jnp.where(sel1 | sel2, 1.0, 0.0)
    ri = lax.broadcasted_iota(I32, (tm, tm), 0)
    ci = lax.broadcasted_iota(I32, (tm, tm), 1)
    before = (ri > ci).astype(BF16)
    carry = carry_ref[0:1, 0:ne]
    rank = _dot(before, sel.astype(BF16)) + carry
    rank_ref[...] = jnp.where(sel > 0.0, rank, -1.0)
    total = carry + jnp.sum(sel, axis=0, keepdims=True)
    carry_ref[0:1, 0:ne] = total
    cnt_ref[...] = total


def _router(x, router_w, router_b):
    s = x.shape[0]
    tm = 512
    ne = N_EXPERTS
    return pl.pallas_call(
        _router_kernel,
        out_shape=(jax.ShapeDtypeStruct((s, ne), F32), jax.ShapeDtypeStruct((s, ne), F32),
                   jax.ShapeDtypeStruct((1, ne), F32), jax.ShapeDtypeStruct((s, D_MODEL), BF16)),
        grid=(s // tm,),
        in_specs=[pl.BlockSpec((tm, D_MODEL), lambda i: (i, 0)),
                  pl.BlockSpec((D_MODEL, ne), lambda i: (0, 0)),
                  pl.BlockSpec((1, ne), lambda i: (0, 0))],
        out_specs=(pl.BlockSpec((tm, ne), lambda i: (i, 0)), pl.BlockSpec((tm, ne), lambda i: (i, 0)),
                   pl.BlockSpec((1, ne), lambda i: (0, 0)), pl.BlockSpec((tm, D_MODEL), lambda i: (i, 0))),
        scratch_shapes=[pltpu.VMEM((8, LANES), F32)],
        compiler_params=_params(("arbitrary",), 32),
        name="moe_router",
    )(x, router_w, router_b.reshape(1, ne))


def _chunk_copy(src_hbm, buf_ref, sem_ref, chunk, slot):
    rows = buf_ref.shape[1]
    start = pl.multiple_of(chunk * rows, rows)
    return pltpu.make_async_copy(src_hbm.at[pl.ds(start, rows), :], buf_ref.at[slot], sem_ref.at[slot])


def _moe_gemm_kernel(te_ref, nu_ref, off_ref, ist_ref, clo_ref, pad_ref, xb_hbm, rank_ref, w1_ref, w3_ref, w2_ref,
                     y_ref, xs_ref, acc_ref, buf_ref, sem_ref):
    gr, ck = MOE_GATHER_ROWS, MOE_GATHER_TOKENS
    sub = MOE_GROUP_TILE // gr
    nbuf = buf_ref.shape[0]
    last_chunk = rank_ref.shape[1] - 1
    i, f = pl.program_id(0), pl.program_id(1)
    last_f = pl.num_programs(1) - 1

    def item(t, k):
        base = t * sub
        s = sum((k >= ist_ref[base + j]).astype(I32) for j in range(1, sub))
        return s, jnp.minimum(clo_ref[base + s] + (k - ist_ref[base + s]), last_chunk)

    def start(t, k):
        @pl.when(k < ist_ref[t * sub + sub])
        def _():
            _chunk_copy(xb_hbm, buf_ref, sem_ref, item(t, k)[1], (k - ist_ref[t * sub]) % nbuf).start()

    @pl.when(i < nu_ref[0])
    def _tile():
        @pl.when(f == 0)
        def _gather_rows():
            e = te_ref[i]
            base = i * sub
            k0, k1 = ist_ref[base], ist_ref[base + sub]
            k_real = k1 - pad_ref[i]
            acc_ref[...] = jnp.zeros_like(acc_ref)

            @pl.when(i == 0)
            def _first_tile():
                for d in range(nbuf):
                    start(i, k0 + d)

            def pair(t, carry):
                ks = (k0 + 2 * t, k0 + 2 * t + 1)
                its = [item(i, k) for k in ks]
                slots = [(k - k0) % nbuf for k in ks]
                for (_, chunk), slot in zip(its, slots):
                    _chunk_copy(xb_hbm, buf_ref, sem_ref, chunk, slot).wait()
                iota = lax.broadcasted_iota(I32, (gr, ck), 0).astype(F32)
                rank0 = [jnp.where(k < k_real, (base + s) * gr - off_ref[e], -(2 ** 30)).astype(F32)
                         for k, (s, _) in zip(ks, its)]
                hit = [(r0 + iota) == rank_ref[e, pl.ds(chunk, 1), :] for r0, (_, chunk) in zip(rank0, its)]
                ys = [_dot(jnp.where(h, 1.0, 0.0).astype(BF16), buf_ref[slot]) for h, slot in zip(hit, slots)]
                for (s, _), y in zip(its, ys):
                    acc_ref[pl.ds(pl.multiple_of(s * gr, gr), gr), :] += y
                for k in ks:
                    start(i, k + nbuf)
                return carry

            lax.fori_loop(0, (k1 - k0) // 2, pair, 0)

            @pl.when(i + 1 < nu_ref[0])
            def _prefetch_next_tile():
                for d in range(nbuf):
                    start(i + 1, k1 + d)

            xs_ref[...] = acc_ref[...].astype(BF16)
            acc_ref[...] = jnp.zeros_like(acc_ref)

        xb = xs_ref[...]
        a = _dot(xb, w1_ref[0].astype(BF16))
        g = _dot(xb, w3_ref[0].astype(BF16))
        hidden = (a * _sigmoid(a) * g).astype(BF16)
        acc_ref[...] += _dot(hidden, w2_ref[0].astype(BF16))

        @pl.when(f == last_f)
        def _finish():
            y_ref[...] = acc_ref[...].astype(y_ref.dtype)

    @pl.when(jnp.logical_and(i >= nu_ref[0], f == last_f))
    def _unused_tile():
        y_ref[...] = jnp.zeros_like(y_ref)


def _combine_kernel(off_ref, cn_ref, cie_ref, cic_ref, x_ref, ys_hbm, rank_ref, gate_ref, lng_ref, lnb_ref,
                    out_ref, acc_ref, buf_ref, sem_ref, *, slots):
    tt, cr = MOE_COMBINE_TOKENS, MOE_COMBINE_ROWS
    j = pl.program_id(0)
    n = cn_ref[j]
    base = j * slots
    acc_ref[...] = jnp.zeros_like(acc_ref)
    nbuf = buf_ref.shape[0]

    def start(k, step=j):
        @pl.when(k < cn_ref[step])
        def _():
            chunk = cic_ref[step * slots + jnp.minimum(k, slots - 1)]
            _chunk_copy(ys_hbm, buf_ref, sem_ref, chunk, k % nbuf).start()

    @pl.when(j == 0)
    def _first_step():
        for d in range(nbuf):
            start(d)

    lane = lax.broadcasted_iota(I32, (tt, N_EXPERTS), 1)
    iota = lax.broadcasted_iota(I32, (tt, cr), 1).astype(F32)

    def pair(t, carry):
        ks = (2 * t, 2 * t + 1)
        es = [cie_ref[base + k] for k in ks]
        chunks = [cic_ref[base + k] for k in ks]
        for k, chunk in zip(ks, chunks):
            _chunk_copy(ys_hbm, buf_ref, sem_ref, chunk, k % nbuf).wait()
        rank = [jnp.max(jnp.where(lane == e, rank_ref[...], -1.0), axis=-1, keepdims=True) for e in es]
        gate = [jnp.sum(jnp.where(lane == e, gate_ref[...], 0.0), axis=-1, keepdims=True) for e in es]
        pos = [jnp.where(r >= 0.0, r + off_ref[jnp.maximum(e, 0)].astype(F32), -1.0)
               for r, e in zip(rank, es)]
        ys = [_dot(jnp.where(p == (chunk * cr).astype(F32) + iota, 1.0, 0.0).astype(BF16), buf_ref[k % nbuf])
              for p, chunk, k in zip(pos, chunks, ks)]
        acc_ref[...] += gate[0] * ys[0] + gate[1] * ys[1]
        for k in ks:
            start(k + nbuf)
        return carry

    lax.fori_loop(0, n // 2, pair, 0)

    @pl.when(j + 1 < pl.num_programs(0))
    def _prefetch_next_step():
        for d in range(nbuf):
            start(d, j + 1)

    out_ref[...] = _layer_norm(ALPHA * x_ref[...] + acc_ref[...], lng_ref[...], lnb_ref[...])


def _moe_layer(x, router_w, router_b, w13, w2, ln_g, ln_b):
    s = x.shape[0]
    ne, gt, gr, ck = N_EXPERTS, MOE_GROUP_TILE, MOE_GATHER_ROWS, MOE_GATHER_TOKENS
    tt, cr = MOE_COMBINE_TOKENS, MOE_COMBINE_ROWS
    sub = gt // gr
    nck = s // ck
    n_group_tiles = (TOP_K * s) // gt + ne
    n_sub = n_group_tiles * sub

    gate, rank, counts, xb = _router(x, router_w, router_b)

    cnt = counts[0].astype(I32)
    padded = ((cnt + gt - 1) // gt) * gt
    off = (jnp.cumsum(padded) - padded).astype(I32)
    n_used = (jnp.sum(padded) // gt).astype(I32)
    group_end = jnp.cumsum(padded)
    tile_row = jnp.arange(n_group_tiles, dtype=I32) * gt
    tile_expert = jnp.minimum(jnp.sum(group_end[None, :] <= tile_row[:, None], axis=1), ne - 1).astype(I32)
    routed = (rank >= 0.0).reshape(nck, ck, ne)
    cum = jnp.cumsum(jnp.sum(routed, axis=1), axis=0).astype(I32).T

    u = jnp.arange(n_sub, dtype=I32)
    e_u = tile_expert[u // sub]
    r0 = u * gr - off[e_u]
    r1 = jnp.minimum(r0 + gr, cnt[e_u]) - 1
    live = (u // sub < n_used) & (r0 < cnt[e_u])
    cum_u = cum[e_u]
    c_lo = jnp.sum(cum_u <= r0[:, None], axis=1).astype(I32)
    c_hi = jnp.sum(cum_u <= r1[:, None], axis=1).astype(I32)
    n_items = jnp.where(live, c_hi - c_lo + 1, 0)
    item_pad = jnp.sum(n_items.reshape(n_group_tiles, sub), axis=1).astype(I32) % 2
    n_items = n_items + jnp.where(u % sub == sub - 1, item_pad[u // sub], 0)
    item_start = jnp.concatenate([jnp.zeros((1,), I32), jnp.cumsum(n_items).astype(I32)])
    c_lo = jnp.where(live, c_lo, 0)

    fc = MOE_FF_CHUNK
    nf = D_FF_E // fc
    tile_of = lambda i, nu: jnp.minimum(i, nu[0] - 1)
    chunk_of = lambda i, f, nu: jnp.where(i < nu[0], f, nf - 1)
    ys = pl.pallas_call(
        _moe_gemm_kernel,
        out_shape=jax.ShapeDtypeStruct((n_group_tiles * gt, D_MODEL), BF16),
        grid_spec=pltpu.PrefetchScalarGridSpec(
            num_scalar_prefetch=6,
            grid=(n_group_tiles, nf),
            in_specs=[pl.BlockSpec(memory_space=pl.ANY),
                      pl.BlockSpec((ne, nck, ck), lambda i, f, te, nu, *_: (0, 0, 0)),
                      pl.BlockSpec((1, D_MODEL, fc),
                                   lambda i, f, te, nu, *_: (te[tile_of(i, nu)], 0, chunk_of(i, f, nu))),
                      pl.BlockSpec((1, D_MODEL, fc),
                                   lambda i, f, te, nu, *_: (te[tile_of(i, nu)], 0, nf + chunk_of(i, f, nu))),
                      pl.BlockSpec((1, fc, D_MODEL),
                                   lambda i, f, te, nu, *_: (te[tile_of(i, nu)], chunk_of(i, f, nu), 0))],
            out_specs=pl.BlockSpec((gt, D_MODEL), lambda i, f, *_: (i, 0)),
            scratch_shapes=[pltpu.VMEM((gt, D_MODEL), BF16), pltpu.VMEM((gt, D_MODEL), F32),
                            pltpu.VMEM((MOE_DMA_DEPTH, ck, D_MODEL), BF16),
                            pltpu.SemaphoreType.DMA((MOE_DMA_DEPTH,))]),
        compiler_params=_params(("arbitrary", "arbitrary"), 58),
        name="moe_gemm",
    )(tile_expert, n_used.reshape(1), off, item_start, c_lo, item_pad, xb, rank.T.reshape(ne, nck, ck), w13, w13, w2)

    njc = s // tt
    per = tt // ck
    incl = cum[:, per - 1::per]
    lo = off[:, None] + jnp.concatenate([jnp.zeros((ne, 1), I32), incl[:, :-1]], axis=1)
    hi = off[:, None] + incl
    first, last = lo // cr, (hi - 1) // cr
    max_chunks = tt // cr + 1
    kk = jnp.arange(max_chunks, dtype=I32)
    slot_chunk = (first[:, :, None] + kk).transpose(1, 0, 2).reshape(njc, -1)
    slot_live = ((hi > lo)[:, :, None] & (first[:, :, None] + kk <= last[:, :, None])).transpose(1, 0, 2)
    slot_live = slot_live.reshape(njc, -1)
    slots = ne * max_chunks
    slot_expert = jnp.broadcast_to(jnp.repeat(jnp.arange(ne, dtype=I32), max_chunks)[None, :], (njc, slots))
    dest = jnp.cumsum(slot_live, axis=1) - 1
    place = slot_live[:, :, None] & (dest[:, :, None] == jnp.arange(slots)[None, None, :])
    item_chunk = jnp.sum(jnp.where(place, slot_chunk[:, :, None], 0), axis=1).astype(I32).reshape(-1)
    item_expert = jnp.sum(jnp.where(place, slot_expert[:, :, None], 0), axis=1).astype(I32)
    item_count = jnp.sum(slot_live, axis=1).astype(I32)
    item_expert = jnp.where(jnp.arange(slots)[None, :] < item_count[:, None], item_expert, -1).reshape(-1)
    item_count = item_count + item_count % 2

    small = pl.BlockSpec((1, D_MODEL), lambda j, *_: (0, 0))
    return pl.pallas_call(
        functools.partial(_combine_kernel, slots=slots),
        out_shape=jax.ShapeDtypeStruct((s, D_MODEL), F32),
        grid_spec=pltpu.PrefetchScalarGridSpec(
            num_scalar_prefetch=4,
            grid=(njc,),
            in_specs=[pl.BlockSpec((tt, D_MODEL), lambda j, *_: (j, 0)),
                      pl.BlockSpec(memory_space=pl.ANY),
                      pl.BlockSpec((tt, ne), lambda j, *_: (j, 0)),
                      pl.BlockSpec((tt, ne), lambda j, *_: (j, 0)),
                      small, small],
            out_specs=pl.BlockSpec((tt, D_MODEL), lambda j, *_: (j, 0)),
            scratch_shapes=[pltpu.VMEM((tt, D_MODEL), F32), pltpu.VMEM((MOE_DMA_DEPTH, cr, D_MODEL), BF16),
                            pltpu.SemaphoreType.DMA((MOE_DMA_DEPTH,))]),
        compiler_params=_params(("arbitrary",), 32),
        name="moe_combine",
    )(off, item_count, item_expert, item_chunk, x, ys, rank, gate, ln_g.reshape(1, -1), ln_b.reshape(1, -1))


def kernel(x, rel_bias, w_in, w_gate, b_gate, conv_w, conv_b, w_qk_m, w_v_m, w_if, b_if, m_norm_g, w_br_a, w_br_m,
           w_o, ln_g, ln_b, ffn_w13, ffn_w2, router_w, router_b, exp_w13, exp_w2):
    batch, seq, _ = x.shape
    assert batch == 1
    h = x.reshape(seq, D_MODEL)
    biases = [_attn_bias(rel_bias, window, dilation) for window, dilation in ATTN_PATTERNS]
    for l in range(DEPTH):
        h = _token_mixer_layer(h, biases, w_in, w_gate, b_gate[l], conv_w[l], conv_b[l], w_qk_m, w_v_m, w_if[l],
                               b_if[l], m_norm_g[l], w_br_a, w_br_m, w_o, ln_g[l, 0], ln_b[l, 0], l)
        j = l // 2
        if l % 2 == 0:
            h = _dense_ffn(h, ffn_w13, ffn_w2, ln_g[l, 1], ln_b[l, 1], j)
        else:
            h = _moe_layer(h, router_w[j], router_b[j], exp_w13[j], exp_w2[j], ln_g[l, 1], ln_b[l, 1])
    return h.reshape(batch, seq, D_MODEL)
```

```python
import functools
import math

import jax
import jax.numpy as jnp
from jax import lax
from jax.experimental import pallas as pl
from jax.experimental.pallas import tpu as pltpu

F32 = jnp.float32
BF16 = jnp.bfloat16
I32 = jnp.int32

D_MODEL = 1024
DEPTH = 2
N_HEADS_A = 8
HEAD_DIM_A = 64
D_A = N_HEADS_A * HEAD_DIM_A
ATTN_PATTERNS = ((128, 1), (512, 4), (2048, 16))
ATTN_BLOCK = 128
NUM_BUCKETS = 32
MAX_DISTANCE = 2048
N_HEADS_M = 4
HEAD_DIM_M = 128
D_M = N_HEADS_M * HEAD_DIM_M
CONV_K = 4
MLSTM_CHUNK = 128
N_PROJ = 5
P_IN = 3 * D_A + 2 * D_M
D_FF = 2816
N_EXPERTS = 8
TOP_K = 2
D_FF_E = 3584
ALPHA = (2.0 * DEPTH) ** 0.25
LN_EPS = 1e-5

NEG = -1e30
LOG2E = math.log2(math.e)
LANES = 128
HALO_ROWS = 16
MIB = 1024 * 1024

MOE_GROUP_TILE = 1024
MOE_FF_CHUNK = 512
MOE_GATHER_ROWS = 256
MOE_GATHER_TOKENS = 256
MOE_COMBINE_TOKENS = 512
MOE_COMBINE_ROWS = 256
MOE_DMA_DEPTH = 6
MERGE_COL_CHUNK = 256
MERGE_ROW_TILE = 1024
MERGE_LOAD_ROWS = 256
FFN_ROW_TILE = 1024
FFN_FF_CHUNK = 256


def _params(sem, vmem_mib):
    return pltpu.CompilerParams(dimension_semantics=sem, vmem_limit_bytes=vmem_mib * MIB)


def _sigmoid(x):
    return 1.0 / (1.0 + jnp.exp(-x))


def _layer_norm(r, g, b):
    mu = jnp.mean(r, axis=-1, keepdims=True)
    c = r - mu
    var = jnp.mean(c * c, axis=-1, keepdims=True)
    return c * lax.rsqrt(var + LN_EPS) * g + b


def _split3(a):
    hi = a.astype(BF16)
    r1 = a - hi.astype(F32)
    mid = r1.astype(BF16)
    lo = (r1 - mid.astype(F32)).astype(BF16)
    return hi, mid, lo


def _dot(a, b):
    return jnp.dot(a, b, preferred_element_type=F32)


def _dot_nt(a, b):
    return lax.dot_general(a, b, (((1,), (1,)), ((), ())), preferred_element_type=F32)


def _dot_tn(a, b):
    return lax.dot_general(a, b, (((0,), (0,)), ((), ())), preferred_element_type=F32)


IN_PROJ_ROWS = 1024
DILATIONS = tuple(d for _, d in ATTN_PATTERNS if d > 1)


def _in_proj_kernel(x_ref, w_ref, *refs):
    nat = refs[:N_PROJ]
    perm = refs[N_PROJ:N_PROJ + 3 * len(DILATIONS)]
    wb_ref, y_ref = refs[-2:]

    @pl.when(pl.program_id(0) == 0)
    def _cast_weights():
        wb_ref[...] = w_ref[...].astype(BF16)

    xb = x_ref[...].astype(BF16)
    for j in range(N_PROJ):
        y = _dot(xb, wb_ref[:, j * D_A:(j + 1) * D_A])
        if j == 0:
            y = y * (HEAD_DIM_A ** -0.5 * LOG2E)
        nat[j][...] = y.astype(BF16)
        if j >= 3:
            continue
        for c in range(D_A // LANES):
            y_ref[c] = y[:, c * LANES:(c + 1) * LANES]
        for di, d in enumerate(DILATIONS):
            out = perm[di * 3 + j]
            tiles, _, rpc, _ = out.shape
            for t in range(tiles):
                for r in range(d):
                    for c in range(D_A // LANES):
                        out[t, r, :, c * LANES:(c + 1) * LANES] = (
                            y_ref[c, pl.ds(t * d * rpc + r, rpc, stride=d), :].astype(BF16))


def _in_proj(x, w_in, layer):
    s = x.shape[0]
    tm = IN_PROJ_ROWS
    blk = ATTN_BLOCK
    out_shape = [jax.ShapeDtypeStruct((s, D_A), BF16)] * N_PROJ
    out_specs = [pl.BlockSpec((tm, D_A), lambda i: (i, 0))] * N_PROJ
    for d in DILATIONS:
        tile = d * blk
        if tile <= tm:
            spec = pl.BlockSpec((tm // tile, d, blk, D_A), lambda i: (i, 0, 0, 0))
        else:
            parts = tile // tm
            spec = pl.BlockSpec((1, d, blk // parts, D_A), lambda i, parts=parts: (i // parts, 0, i % parts, 0))
        out_shape += [jax.ShapeDtypeStruct((s // tile, d, blk, D_A), BF16)] * 3
        out_specs += [spec] * 3
    outs = pl.pallas_call(
        _in_proj_kernel,
        out_shape=tuple(out_shape),
        grid=(s // tm,),
        in_specs=[
            pl.BlockSpec((tm, D_MODEL), lambda i: (i, 0)),
            pl.BlockSpec((None, D_MODEL, P_IN), lambda i: (layer, 0, 0), pipeline_mode=pl.Buffered(1)),
        ],
        out_specs=tuple(out_specs),
        scratch_shapes=[pltpu.VMEM((D_MODEL, P_IN), BF16), pltpu.VMEM((D_A // LANES, tm, LANES), F32)],
        compiler_params=_params(("arbitrary",), 56),
        name="in_proj",
    )(x, w_in)
    nat = outs[:N_PROJ]
    perm = [tuple(t.reshape(s, D_A) for t in outs[N_PROJ + 3 * i:N_PROJ + 3 * i + 3]) for i in range(len(DILATIONS))]
    return nat, perm


ATTN_STEP_BLOCKS = 16


def _rel_bucket(dist):
    exact = NUM_BUCKETS // 2
    d = jnp.maximum(dist, exact).astype(F32)
    log_b = exact + (jnp.log(d / exact) / math.log(MAX_DISTANCE / exact) * (NUM_BUCKETS - exact)).astype(I32)
    return jnp.where(dist < exact, dist, jnp.minimum(log_b, NUM_BUCKETS - 1))


def _attn_bias(rel_bias, window, dilation):
    blk = ATTN_BLOCK
    qi = jnp.arange(blk)[:, None]
    kj = jnp.arange(2 * blk)[None, :]
    rel = qi + blk - kj
    bucket = _rel_bucket(jnp.maximum(rel, 0) * dilation)
    onehot = (bucket[..., None] == jnp.arange(NUM_BUCKETS)).astype(F32)
    bias = jnp.einsum("qkb,bh->hqk", onehot, rel_bias.astype(F32), precision=lax.Precision.HIGHEST)
    mask = (rel >= 0) & (rel <= window // dilation)
    return jnp.where(mask[None], bias * LOG2E, NEG)


def _attn_kernel(q_ref, kp_ref, kc_ref, vp_ref, vc_ref, bias_ref, o_ref, st_ref, on_ref, sn_ref, *, dilation):
    blk = ATTN_BLOCK
    nb = ATTN_STEP_BLOCKS
    step = pl.program_id(0)
    kj = lax.broadcasted_iota(I32, (2 * blk, 2 * blk), 1)
    lane = lax.broadcasted_iota(I32, (blk, LANES), 1)
    ones = jnp.ones((2 * blk, LANES), BF16)
    pairs = range(N_HEADS_A // 2)

    def block(g, carry):
        cur = pl.multiple_of(g * blk, blk)
        in_tile = g >= dilation
        prev_c = pl.multiple_of(jnp.maximum(g - dilation, 0) * blk, blk)
        prev_p = pl.multiple_of(jnp.minimum(nb + g - dilation, nb - 1) * blk, blk)
        pen = jnp.where(jnp.logical_and(step == 0, g < dilation), NEG, 0.0).astype(F32)
        prev_pen = jnp.where(kj < blk, pen, 0.0)
        tile_i, cls = g // dilation, g % dilation
        dst = pl.ds(tile_i * (blk * dilation) + cls, blk, stride=dilation)

        def keys(cur_ref, prev_ref, ls):
            prev = jnp.where(in_tile, cur_ref[pl.ds(prev_c, blk), ls], prev_ref[pl.ds(prev_p, blk), ls])
            return jnp.concatenate([prev, cur_ref[pl.ds(cur, blk), ls]], axis=0)

        even = (lane // HEAD_DIM_A) == 0
        lss = [slice(hp * LANES, (hp + 1) * LANES) for hp in pairs]
        qb = [q_ref[pl.ds(cur, blk), lss[hp]] for hp in pairs]
        qcat = [jnp.concatenate([jnp.where(even, qb[hp], jnp.zeros_like(qb[hp])),
                                 jnp.where(even, jnp.zeros_like(qb[hp]), qb[hp])], axis=0) for hp in pairs]
        kb = [keys(kc_ref, kp_ref, lss[hp]) for hp in pairs]
        v1 = [jnp.concatenate([keys(vc_ref, vp_ref, lss[hp]), ones], axis=1) for hp in pairs]
        logits = [_dot_nt(qcat[hp], kb[hp]) + bias_ref[hp] + prev_pen for hp in pairs]
        m = [jnp.max(logits[hp], axis=-1, keepdims=True) for hp in pairs]
        p = [jnp.exp2(logits[hp] - m[hp]).astype(BF16) for hp in pairs]
        ol = [_dot(p[hp], v1[hp]) for hp in pairs]
        stats = jnp.zeros((blk, LANES), F32)
        for hp in pairs:
            for par in range(2):
                h, rs = 2 * hp + par, slice(par * blk, (par + 1) * blk)
                stats = jnp.where(lane == h, m[hp][rs], stats)
                stats = jnp.where(lane == N_HEADS_A + h, ol[hp][rs, LANES:], stats)
            on_ref[hp, dst, :] = jnp.where(even, ol[hp][:blk, :LANES], ol[hp][blk:, :LANES])
        sn_ref[dst, :] = stats
        return carry

    lax.fori_loop(0, nb, block, 0, unroll=2)
    for hp in range(N_HEADS_A // 2):
        o_ref[:, hp * LANES:(hp + 1) * LANES] = on_ref[hp].astype(o_ref.dtype)
    st_ref[...] = sn_ref[...]


def _attn_pattern(q, k, v, bias, dilation):
    s = q.shape[0]
    rows = ATTN_STEP_BLOCKS * ATTN_BLOCK
    cur = pl.BlockSpec((rows, D_A), lambda i: (i, 0))
    prev = pl.BlockSpec((rows, D_A), lambda i: (jnp.maximum(i - 1, 0), 0))
    return pl.pallas_call(
        functools.partial(_attn_kernel, dilation=dilation),
        out_shape=(jax.ShapeDtypeStruct((s, D_A), BF16), jax.ShapeDtypeStruct((s, LANES), F32)),
        grid=(s // rows,),
        in_specs=[cur, prev, cur, prev, cur,
                  pl.BlockSpec((N_HEADS_A // 2, 2 * ATTN_BLOCK, 2 * ATTN_BLOCK), lambda i: (0, 0, 0))],
        out_specs=(pl.BlockSpec((rows, D_A), lambda i: (i, 0)), pl.BlockSpec((rows, LANES), lambda i: (i, 0))),
        scratch_shapes=[pltpu.VMEM((D_A // LANES, rows, LANES), F32), pltpu.VMEM((rows, LANES), F32)],
        compiler_params=_params(("arbitrary",), 48),
        name=f"attn_d{dilation}",
    )(q, k, k, v, v, bias.reshape(N_HEADS_A // 2, 2 * ATTN_BLOCK, 2 * ATTN_BLOCK))


def _log_sigmoid(x):
    return jnp.minimum(x, 0.0) - jnp.log(1.0 + jnp.exp(-jnp.abs(x)))


def _mlstm_prep_kernel(xm_ref, halo_ref, cw_ref, cb_ref, wqk_ref, wkt_ref, wv_ref, wif_ref, wift_ref, bif_ref,
                       bift_ref, q_ref, k_ref, kt_ref, v_ref, gc_ref, gr_ref):
    tm = xm_ref.shape[0]
    hm, hd, lc = N_HEADS_M, HEAD_DIM_M, MLSTM_CHUNK
    xmb = xm_ref[...]
    halo = jnp.where(pl.program_id(0) == 0, 0.0, halo_ref[...].astype(F32))
    xx = jnp.concatenate([halo, xmb.astype(F32)], axis=0)
    conv = jnp.zeros((tm, D_M), F32) + cb_ref[...]
    for j in range(CONV_K):
        start = HALO_ROWS - (CONV_K - 1) + j
        conv = conv + cw_ref[j:j + 1, :] * xx[start:start + tm, :]
    xcb = (conv * _sigmoid(conv)).astype(BF16)

    qs, ks, vs = [], [], []
    for h in range(hm):
        sl = slice(h * hd, (h + 1) * hd)
        qs.append(_dot(xcb[:, sl], wqk_ref[0, h].astype(BF16)))
        ks.append(_dot(xcb[:, sl], wqk_ref[1, h].astype(BF16)))
        vs.append(_dot(xmb[:, sl], wv_ref[h].astype(BF16)))
        kt = _dot_nt(wkt_ref[h].astype(BF16), xcb[:, sl])
        kt_ref[sl, :] = (kt * (hd ** -0.5)).astype(BF16)
    q = jnp.concatenate(qs, axis=1)
    k = jnp.concatenate(ks, axis=1)
    v = jnp.concatenate(vs, axis=1)
    q_ref[...] = q.astype(BF16)
    k_ref[...] = (k * (hd ** -0.5)).astype(BF16)
    v_ref[...] = v.astype(BF16)

    qkv = jnp.concatenate([q, k, v], axis=1).astype(BF16)
    gates_c = _dot(qkv, wif_ref[...].astype(BF16)) + bif_ref[...]
    gates_r = _dot_nt(wift_ref[...].astype(BF16), qkv) + bift_ref[...]
    lane = lax.broadcasted_iota(I32, gates_c.shape, 1)
    row = lax.broadcasted_iota(I32, gates_r.shape, 0)
    gc_ref[...] = jnp.where(lane < hm, gates_c, _log_sigmoid(gates_c))
    gr_ref[...] = jnp.where(row < hm, gates_r, _log_sigmoid(gates_r))

    ri = lax.broadcasted_iota(I32, (lc, lc), 0)
    ci = lax.broadcasted_iota(I32, (lc, lc), 1)
    lower = (ri >= ci).astype(BF16)
    upper = (ri <= ci).astype(BF16)
    lane_c = lax.broadcasted_iota(I32, (lc, 2 * hm), 1)
    row_c = lax.broadcasted_iota(I32, (2 * hm, lc), 0)
    for c in range(tm // lc):
        rs = slice(c * lc, (c + 1) * lc)
        gcc = gc_ref[rs, :]
        grc = gr_ref[:, rs]
        cum_c = sum(_dot(lower, part) for part in _split3(gcc))
        cum_r = sum(_dot(part, upper) for part in _split3(grc))
        gc_ref[rs, :] = jnp.where(lane_c < hm, gcc, cum_c)
        gr_ref[:, rs] = jnp.where(row_c < hm, grc, cum_r)


def _mlstm_prep(xm, conv_w, conv_b, w_qk_m, w_v_m, w_if, b_if, layer):
    s = xm.shape[0]
    tm = MLSTM_PREP_ROWS
    hpt = tm // HALO_ROWS
    full = lambda shape: pl.BlockSpec(shape, lambda i: (0,) * len(shape))
    stacked = lambda shape: pl.BlockSpec((None,) + shape, lambda i: (layer,) + (0,) * len(shape))
    row_spec = pl.BlockSpec((tm, D_M), lambda i: (i, 0))
    head_sq = (N_HEADS_M, HEAD_DIM_M, HEAD_DIM_M)
    wk_t = jnp.swapaxes(w_qk_m[layer, 1], -1, -2)
    return pl.pallas_call(
        _mlstm_prep_kernel,
        out_shape=(jax.ShapeDtypeStruct((s, D_M), BF16), jax.ShapeDtypeStruct((s, D_M), BF16),
                   jax.ShapeDtypeStruct((D_M, s), BF16), jax.ShapeDtypeStruct((s, D_M), BF16),
                   jax.ShapeDtypeStruct((s, 2 * N_HEADS_M), F32), jax.ShapeDtypeStruct((2 * N_HEADS_M, s), F32)),
        grid=(s // tm,),
        in_specs=[
            row_spec,
            pl.BlockSpec((HALO_ROWS, D_M), lambda i: (jnp.maximum(i * hpt - 1, 0), 0)),
            full((CONV_K, D_M)), full((1, D_M)),
            stacked((2,) + head_sq), full(head_sq), stacked(head_sq),
            full((3 * D_M, 2 * N_HEADS_M)), full((2 * N_HEADS_M, 3 * D_M)),
            full((1, 2 * N_HEADS_M)), full((2 * N_HEADS_M, 1)),
        ],
        out_specs=(row_spec, row_spec, pl.BlockSpec((D_M, tm), lambda i: (0, i)), row_spec,
                   pl.BlockSpec((tm, 2 * N_HEADS_M), lambda i: (i, 0)),
                   pl.BlockSpec((2 * N_HEADS_M, tm), lambda i: (0, i))),
        compiler_params=_params(("arbitrary",), 32),
        name="mlstm_prep",
    )(xm, xm, conv_w, conv_b.reshape(1, D_M), w_qk_m, wk_t, w_v_m, w_if, w_if.T,
      b_if.reshape(1, -1), b_if.reshape(-1, 1))


MLSTM_STEP_CHUNKS = 8
MLSTM_PREP_ROWS = 1024


def _mlstm_scan_kernel(q_ref, k_ref, kt_ref, v_ref, gc_ref, gr_ref, z_ref, g_ref, y_ref, c_ref, m_ref, *, chunks):
    hm, hd, lc = N_HEADS_M, HEAD_DIM_M, MLSTM_CHUNK

    @pl.when(pl.program_id(0) == 0)
    def _init():
        c_ref[...] = jnp.zeros_like(c_ref)
        m_ref[...] = jnp.zeros_like(m_ref)

    ri = lax.broadcasted_iota(I32, (lc, lc), 0)
    ci = lax.broadcasted_iota(I32, (lc, lc), 1)
    causal = ri >= ci
    ones = jnp.ones((lc, hd), BF16)
    heads = range(hm)
    hsl = [slice(h * hd, (h + 1) * hd) for h in heads]
    m_state = [m_ref[h:h + 1, 0:1] for h in heads]
    for c in range(chunks):
        rs = slice(c * lc, (c + 1) * lc)
        gc = gc_ref[rs, :]
        gr = gr_ref[:, rs]
        qs = [q_ref[rs, hsl[h]] for h in heads]
        v1 = [jnp.concatenate([v_ref[rs, hsl[h]], ones], axis=1) for h in heads]
        i_row = [gr[h:h + 1, :] for h in heads]
        b_row = [gr[hm + h:hm + h + 1, :] for h in heads]
        b_col = [gc[:, hm + h:hm + h + 1] for h in heads]
        qk = [_dot_nt(qs[h], k_ref[rs, hsl[h]]) for h in heads]
        c_prev = [c_ref[h] for h in heads]
        qc = [_dot(qs[h], c_prev[h].astype(BF16)) for h in heads]
        dm = [jnp.where(causal, b_col[h] - b_row[h] + i_row[h], NEG) for h in heads]
        inter = [b_col[h] + m_state[h] for h in heads]
        m_loc = [jnp.maximum(inter[h], jnp.max(dm[h], axis=-1, keepdims=True)) for h in heads]
        sc = [(qk[h] * jnp.exp(dm[h] - m_loc[h])).astype(BF16) for h in heads]
        both = [jnp.exp(inter[h] - m_loc[h]) * qc[h] + _dot(sc[h], v1[h]) for h in heads]
        hval = [both[h][:, :hd] / jnp.maximum(jnp.abs(both[h][:, hd:]), jnp.exp(-m_loc[h])) for h in heads]

        b_last = [b_row[h][:, lc - 1:lc] for h in heads]
        g_row = [b_last[h] - b_row[h] + i_row[h] for h in heads]
        m_new = [jnp.maximum(b_last[h] + m_state[h], jnp.max(g_row[h], axis=-1, keepdims=True)) for h in heads]
        ktw = [(kt_ref[hsl[h], rs].astype(F32) * jnp.exp(g_row[h] - m_new[h])).astype(BF16) for h in heads]
        for h in heads:
            c_ref[h] = jnp.exp(b_last[h] + m_state[h] - m_new[h]) * c_prev[h] + _dot(ktw[h], v1[h])
        m_state = m_new

        mu = [jnp.mean(hval[h], axis=-1, keepdims=True) for h in heads]
        cen = [hval[h] - mu[h] for h in heads]
        var = [jnp.mean(cen[h] * cen[h], axis=-1, keepdims=True) for h in heads]
        for h in heads:
            hn = cen[h] * lax.rsqrt(var[h] + LN_EPS) * g_ref[:, hsl[h]]
            y_ref[rs, hsl[h]] = (_sigmoid(z_ref[rs, hsl[h]].astype(F32)) * hn).astype(y_ref.dtype)
    for h in heads:
        m_ref[h:h + 1, :] = jnp.broadcast_to(m_state[h], (1, LANES))


def _mlstm_scan(q, k, kt, v, gc, gr, z, m_norm_g):
    s = q.shape[0]
    chunks = MLSTM_STEP_CHUNKS
    tm = chunks * MLSTM_CHUNK
    row_spec = pl.BlockSpec((tm, D_M), lambda i: (i, 0))
    return pl.pallas_call(
        functools.partial(_mlstm_scan_kernel, chunks=chunks),
        out_shape=jax.ShapeDtypeStruct((s, D_M), BF16),
        grid=(s // tm,),
        in_specs=[row_spec, row_spec, pl.BlockSpec((D_M, tm), lambda i: (0, i)), row_spec,
                  pl.BlockSpec((tm, 2 * N_HEADS_M), lambda i: (i, 0)),
                  pl.BlockSpec((2 * N_HEADS_M, tm), lambda i: (0, i)),
                  row_spec,
                  pl.BlockSpec((1, D_M), lambda i: (0, 0))],
        out_specs=row_spec,
        scratch_shapes=[pltpu.VMEM((N_HEADS_M, HEAD_DIM_M, 2 * HEAD_DIM_M), F32),
                        pltpu.VMEM((8, LANES), F32)],
        compiler_params=_params(("arbitrary",), 32),
        name="mlstm_scan",
    )(q, k, kt, v, gc, gr, z, m_norm_g.reshape(1, D_M))


def _merge_kernel(x_ref, o1_ref, o2_ref, o3_ref, l1_ref, l2_ref, l3_ref, ym_ref,
                  wg_hbm, bg_ref, wa_hbm, wm_hbm, wo_hbm, lng_ref, lnb_ref, out_ref,
                  wgb_ref, wab_ref, wmb_ref, wob_ref, stg_ref, st_ref, sem_ref, *, layer):
    @pl.when(pl.program_id(0) == 0)
    def _load_weights():
        rc = st_ref.shape[1]
        rows = lambda c: pl.ds(c * rc, rc)
        _load_cast([(wg_hbm.at[layer, rows(c), :], wgb_ref.at[rows(c), :]) for c in range(D_MODEL // rc)],
                   stg_ref, sem_ref)
        _load_cast([(src.at[layer, rows(c), :], dst.at[rows(c), :])
                    for src, dst in ((wa_hbm, wab_ref), (wm_hbm, wmb_ref), (wo_hbm, wob_ref))
                    for c in range(dst.shape[0] // rc)], st_ref, sem_ref)

    x = x_ref[...]
    xb = x.astype(BF16)
    stats = (l1_ref[...], l2_ref[...], l3_ref[...])
    mx = jnp.maximum(jnp.maximum(stats[0], stats[1]), stats[2])
    es = [jnp.exp2(st - mx) for st in stats]
    ls = [pltpu.roll(st, LANES - N_HEADS_A, 1) for st in stats]
    den = es[0] * ls[0] + es[1] * ls[1] + es[2] * ls[2]
    head_lane = lax.broadcasted_iota(I32, den.shape, 1) < N_HEADS_A
    inv = jnp.where(head_lane, 1.0 / den, 0.0)
    hrow = lax.broadcasted_iota(I32, (LANES, D_A), 0)
    hcol = lax.broadcasted_iota(I32, (LANES, D_A), 1) // HEAD_DIM_A
    expand = (hrow == hcol).astype(BF16)
    ya = jnp.zeros((x.shape[0], D_A), F32)
    for e, o_ref in zip(es, (o1_ref, o2_ref, o3_ref)):
        w = sum(_dot(part, expand) for part in _split3(e * inv)[:2])
        ya = ya + w * o_ref[...].astype(F32)

    yab, ymb = ya.astype(BF16), ym_ref[...]
    mc = MERGE_COL_CHUNK
    n_chunks = D_MODEL // mc

    def pre(c):
        ca = slice(c * mc, (c + 1) * mc)
        cm = slice(D_MODEL + c * mc, D_MODEL + (c + 1) * mc)
        return (_dot(xb, wgb_ref[:, ca]) + bg_ref[:, ca], _dot(xb, wgb_ref[:, cm]) + bg_ref[:, cm],
                _dot(yab, wab_ref[:, ca]), _dot(ymb, wmb_ref[:, ca]))

    nxt = pre(0)
    y = None
    for c in range(n_chunks):
        za, zm, pa, pm = nxt
        if c + 1 < n_chunks:
            nxt = pre(c + 1)
        merged = _sigmoid(za) * pa + _sigmoid(zm) * pm
        part = _dot(merged.astype(BF16), wob_ref[c * mc:(c + 1) * mc, :])
        y = part if y is None else y + part
    out_ref[...] = _layer_norm(ALPHA * x + y, lng_ref[...], lnb_ref[...])


def _merge(x, outs, lses, ym, w_gate, b_gate, w_br_a, w_br_m, w_o, ln_g, ln_b, layer):
    s = x.shape[0]
    tm = MERGE_ROW_TILE
    hbm = pl.BlockSpec(memory_space=pl.ANY)
    small = lambda n: pl.BlockSpec((1, n), lambda i: (0, 0))
    rows = lambda n: pl.BlockSpec((tm, n), lambda i: (i, 0))
    rc = MERGE_LOAD_ROWS
    return pl.pallas_call(
        functools.partial(_merge_kernel, layer=layer),
        out_shape=jax.ShapeDtypeStruct((s, D_MODEL), F32),
        grid=(s // tm,),
        in_specs=[rows(D_MODEL), rows(D_A), rows(D_A), rows(D_A), rows(LANES), rows(LANES), rows(LANES), rows(D_M),
                  hbm, small(2 * D_MODEL), hbm, hbm, hbm, small(D_MODEL), small(D_MODEL)],
        out_specs=rows(D_MODEL),
        scratch_shapes=[pltpu.VMEM((D_MODEL, 2 * D_MODEL), BF16), pltpu.VMEM((D_A, D_MODEL), BF16),
                        pltpu.VMEM((D_M, D_MODEL), BF16), pltpu.VMEM((D_MODEL, D_MODEL), BF16),
                        pltpu.VMEM((2, rc, 2 * D_MODEL), F32), pltpu.VMEM((2, rc, D_MODEL), F32),
                        pltpu.SemaphoreType.DMA((2,))],
        compiler_params=_params(("arbitrary",), 56),
        name="merge",
    )(x, *outs, *lses, ym, w_gate, b_gate.reshape(1, -1), w_br_a, w_br_m, w_o,
      ln_g.reshape(1, -1), ln_b.reshape(1, -1))


def _token_mixer_layer(x, biases, w_in, w_gate, b_gate, conv_w, conv_b, w_qk_m, w_v_m, w_if, b_if, m_norm_g,
                       w_br_a, w_br_m, w_o, ln_g, ln_b, layer):
    (q, k, v, xm, zm), perm = _in_proj(x, w_in, layer)
    outs, stats = [], []
    for (_, dilation), bias in zip(ATTN_PATTERNS, biases):
        qd, kd, vd = (q, k, v) if dilation == 1 else perm[DILATIONS.index(dilation)]
        o, st = _attn_pattern(qd, kd, vd, bias, dilation)
        outs.append(o)
        stats.append(st)
    qm, km, ktm, vm, gc, gr = _mlstm_prep(xm, conv_w, conv_b, w_qk_m, w_v_m, w_if, b_if, layer)
    ym = _mlstm_scan(qm, km, ktm, vm, gc, gr, zm, m_norm_g)
    return _merge(x, outs, stats, ym, w_gate, b_gate, w_br_a, w_br_m, w_o, ln_g, ln_b, layer)


def _load_cast(chunks, stage_ref, sem_ref):
    copies = [pltpu.make_async_copy(src, stage_ref.at[k % 2], sem_ref.at[k % 2]) for k, (src, _) in enumerate(chunks)]
    copies[0].start()
    for k, (_, dst) in enumerate(chunks):
        if k + 1 < len(chunks):
            copies[k + 1].start()
        copies[k].wait()
        dst[...] = stage_ref[k % 2].astype(BF16)


def _ffn_kernel(x_ref, w13_hbm, w2_hbm, lng_ref, lnb_ref, out_ref, w13b_ref, w2b_ref, acc_ref, st13_ref, st2_ref,
                sem_ref, *, j):
    fc = FFN_FF_CHUNK

    @pl.when(pl.program_id(0) == 0)
    def _load_weights():
        cols = lambda c: pl.ds(c * fc, fc)
        _load_cast([(w13_hbm.at[j, :, cols(c)], w13b_ref.at[:, cols(c)]) for c in range(2 * D_FF // fc)],
                   st13_ref, sem_ref)
        _load_cast([(w2_hbm.at[j, cols(c), :], w2b_ref.at[cols(c), :]) for c in range(D_FF // fc)],
                   st2_ref, sem_ref)

    xb = x_ref[...].astype(BF16)
    n_chunks = D_FF // fc

    def up(c):
        return (_dot(xb, w13b_ref[:, c * fc:(c + 1) * fc]),
                _dot(xb, w13b_ref[:, D_FF + c * fc:D_FF + (c + 1) * fc]))

    nxt = up(0)
    for c in range(n_chunks):
        a, g = nxt
        if c + 1 < n_chunks:
            nxt = up(c + 1)
        y = _dot((a * _sigmoid(a) * g).astype(BF16), w2b_ref[c * fc:(c + 1) * fc, :])
        if c == 0:
            acc_ref[...] = y
        else:
            acc_ref[...] += y
    out_ref[...] = _layer_norm(ALPHA * x_ref[...] + acc_ref[...], lng_ref[...], lnb_ref[...])


def _dense_ffn(x, w13, w2, ln_g, ln_b, j):
    s = x.shape[0]
    tm = min(FFN_ROW_TILE, s)
    fc = FFN_FF_CHUNK
    small = pl.BlockSpec((1, D_MODEL), lambda i: (0, 0))
    return pl.pallas_call(
        functools.partial(_ffn_kernel, j=j),
        out_shape=jax.ShapeDtypeStruct((s, D_MODEL), F32),
        grid=(s // tm,),
        in_specs=[pl.BlockSpec((tm, D_MODEL), lambda i: (i, 0)),
                  pl.BlockSpec(memory_space=pl.ANY), pl.BlockSpec(memory_space=pl.ANY),
                  small, small],
        out_specs=pl.BlockSpec((tm, D_MODEL), lambda i: (i, 0)),
        scratch_shapes=[pltpu.VMEM((D_MODEL, 2 * D_FF), BF16), pltpu.VMEM((D_FF, D_MODEL), BF16),
                        pltpu.VMEM((tm, D_MODEL), F32),
                        pltpu.VMEM((2, D_MODEL, fc), F32), pltpu.VMEM((2, fc, D_MODEL), F32),
                        pltpu.SemaphoreType.DMA((2,))],
        compiler_params=_params(("arbitrary",), 56),
        name="dense_ffn",
    )(x, w13, w2, ln_g.reshape(1, -1), ln_b.reshape(1, -1))


def _router_kernel(x_ref, rw_ref, rb_ref, gate_ref, rank_ref, cnt_ref, xb_ref, carry_ref):
    tm = x_ref.shape[0]
    ne = N_EXPERTS

    @pl.when(pl.program_id(0) == 0)
    def _init():
        carry_ref[...] = jnp.zeros_like(carry_ref)

    x = x_ref[...]
    xb_ref[...] = x.astype(BF16)
    xs = _split3(x)
    ws = _split3(rw_ref[...])
    logits = rb_ref[...] + sum(_dot(xs[a], ws[b]) for a, b in ((1, 0), (0, 1), (0, 0)))
    lane = lax.broadcasted_iota(I32, (tm, ne), 1)
    v1 = jnp.max(logits, axis=-1, keepdims=True)
    i1 = jnp.min(jnp.where(logits == v1, lane, ne), axis=-1, keepdims=True)
    rest = jnp.where(lane == i1, -jnp.inf, logits)
    v2 = jnp.max(rest, axis=-1, keepdims=True)
    i2 = jnp.min(jnp.where(rest == v2, lane, ne), axis=-1, keepdims=True)
    e2 = jnp.exp(v2 - v1)
    den = 1.0 + e2
    sel1, sel2 = lane == i1, lane == i2
    gate_ref[...] = jnp.where(sel1, 1.0 / den, 0.0) + jnp.where(sel2, e2 / den, 0.0)
    sel = jnp.where(sel1 | sel2, 1.0, 0.0)
    ri = lax.broadcasted_iota(I32, (tm, tm), 0)
    ci = lax.broadcasted_iota(I32, (tm, tm), 1)
    before = (ri > ci).astype(BF16)
    carry = carry_ref[0:1, 0:ne]
    rank = _dot(before, sel.astype(BF16)) + carry
    rank_ref[...] = jnp.where(sel > 0.0, rank, -1.0)
    total = carry + jnp.sum(sel, axis=0, keepdims=True)
    carry_ref[0:1, 0:ne] = total
    cnt_ref[...] = total


def _router(x, router_w, router_b):
    s = x.shape[0]
    tm = 512
    ne = N_EXPERTS
    return pl.pallas_call(
        _router_kernel,
        out_shape=(jax.ShapeDtypeStruct((s, ne), F32), jax.ShapeDtypeStruct((s, ne), F32),
                   jax.ShapeDtypeStruct((1, ne), F32), jax.ShapeDtypeStruct((s, D_MODEL), BF16)),
        grid=(s // tm,),
        in_specs=[pl.BlockSpec((tm, D_MODEL), lambda i: (i, 0)),
                  pl.BlockSpec((D_MODEL, ne), lambda i: (0, 0)),
                  pl.BlockSpec((1, ne), lambda i: (0, 0))],
        out_specs=(pl.BlockSpec((tm, ne), lambda i: (i, 0)), pl.BlockSpec((tm, ne), lambda i: (i, 0)),
                   pl.BlockSpec((1, ne), lambda i: (0, 0)), pl.BlockSpec((tm, D_MODEL), lambda i: (i, 0))),
        scratch_shapes=[pltpu.VMEM((8, LANES), F32)],
        compiler_params=_params(("arbitrary",), 32),
        name="moe_router",
    )(x, router_w, router_b.reshape(1, ne))


def _chunk_copy(src_hbm, buf_ref, sem_ref, chunk, slot):
    rows = buf_ref.shape[1]
    start = pl.multiple_of(chunk * rows, rows)
    return pltpu.make_async_copy(src_hbm.at[pl.ds(start, rows), :], buf_ref.at[slot], sem_ref.at[slot])


def _moe_gemm_kernel(te_ref, nu_ref, off_ref, ist_ref, clo_ref, pad_ref, nsub_ref, xb_hbm, rank_ref, w1_ref, w3_ref,
                     w2_ref, y_ref, xs_ref, acc_ref, buf_ref, sem_ref):
    gr, ck = MOE_GATHER_ROWS, MOE_GATHER_TOKENS
    sub = MOE_GROUP_TILE // gr
    nbuf = buf_ref.shape[0]
    last_chunk = rank_ref.shape[1] - 1
    i, f = pl.program_id(0), pl.program_id(1)
    last_f = pl.num_programs(1) - 1

    def item(t, k):
        base = t * sub
        s = sum((k >= ist_ref[base + j]).astype(I32) for j in range(1, sub))
        return s, jnp.minimum(clo_ref[base + s] + (k - ist_ref[base + s]), last_chunk)

    def start(t, k):
        @pl.when(k < ist_ref[t * sub + sub])
        def _():
            _chunk_copy(xb_hbm, buf_ref, sem_ref, item(t, k)[1], (k - ist_ref[t * sub]) % nbuf).start()

    @pl.when(i < nu_ref[0])
    def _tile():
        @pl.when(f == 0)
        def _gather_rows():
            e = te_ref[i]
            base = i * sub
            k0, k1 = ist_ref[base], ist_ref[base + sub]
            k_real = k1 - pad_ref[i]
            acc_ref[...] = jnp.zeros_like(acc_ref)

            @pl.when(i == 0)
            def _first_tile():
                for d in range(nbuf):
                    start(i, k0 + d)

            def pair(t, carry):
                ks = (k0 + 2 * t, k0 + 2 * t + 1)
                its = [item(i, k) for k in ks]
                slots = [(k - k0) % nbuf for k in ks]
                for (_, chunk), slot in zip(its, slots):
                    _chunk_copy(xb_hbm, buf_ref, sem_ref, chunk, slot).wait()
                iota = lax.broadcasted_iota(I32, (gr, ck), 0).astype(F32)
                rank0 = [jnp.where(k < k_real, (base + s) * gr - off_ref[e], -(2 ** 30)).astype(F32)
                         for k, (s, _) in zip(ks, its)]
                hit = [(r0 + iota) == rank_ref[e, pl.ds(chunk, 1), :] for r0, (_, chunk) in zip(rank0, its)]
                ys = [_dot(jnp.where(h, 1.0, 0.0).astype(BF16), buf_ref[slot]) for h, slot in zip(hit, slots)]
                for (s, _), y in zip(its, ys):
                    acc_ref[pl.ds(pl.multiple_of(s * gr, gr), gr), :] += y
                for k in ks:
                    start(i, k + nbuf)
                return carry

            lax.fori_loop(0, (k1 - k0) // 2, pair, 0)

            @pl.when(i + 1 < nu_ref[0])
            def _prefetch_next_tile():
                for d in range(nbuf):
                    start(i + 1, k1 + d)

            xs_ref[...] = acc_ref[...].astype(BF16)
            acc_ref[...] = jnp.zeros_like(acc_ref)

        def ffn_rows(m):
            def run():
                xb = xs_ref[0:m, :]
                a = _dot(xb, w1_ref[0].astype(BF16))
                g = _dot(xb, w3_ref[0].astype(BF16))
                acc_ref[0:m, :] += _dot((a * _sigmoid(a) * g).astype(BF16), w2_ref[0].astype(BF16))
            return run

        lax.switch(nsub_ref[i] - 1, [ffn_rows((r + 1) * gr) for r in range(sub)])

        @pl.when(f == last_f)
        def _finish():
            y_ref[...] = acc_ref[...].astype(y_ref.dtype)

    @pl.when(jnp.logical_and(i >= nu_ref[0], f == last_f))
    def _unused_tile():
        y_ref[...] = jnp.zeros_like(y_ref)


def _combine_kernel(off_ref, cn_ref, cie_ref, cic_ref, x_ref, ys_hbm, rank_ref, gate_ref, lng_ref, lnb_ref,
                    out_ref, acc_ref, buf_ref, sem_ref, *, slots):
    tt, cr = MOE_COMBINE_TOKENS, MOE_COMBINE_ROWS
    j = pl.program_id(0)
    n = cn_ref[j]
    base = j * slots
    acc_ref[...] = jnp.zeros_like(acc_ref)
    nbuf = buf_ref.shape[0]

    def start(k, step=j):
        @pl.when(k < cn_ref[step])
        def _():
            chunk = cic_ref[step * slots + jnp.minimum(k, slots - 1)]
            _chunk_copy(ys_hbm, buf_ref, sem_ref, chunk, k % nbuf).start()

    @pl.when(j == 0)
    def _first_step():
        for d in range(nbuf):
            start(d)

    lane = lax.broadcasted_iota(I32, (tt, N_EXPERTS), 1)
    iota = lax.broadcasted_iota(I32, (tt, cr), 1).astype(F32)

    def pair(t, carry):
        ks = (2 * t, 2 * t + 1)
        es = [cie_ref[base + k] for k in ks]
        chunks = [cic_ref[base + k] for k in ks]
        for k, chunk in zip(ks, chunks):
            _chunk_copy(ys_hbm, buf_ref, sem_ref, chunk, k % nbuf).wait()
        rank = [jnp.max(jnp.where(lane == e, rank_ref[...], -1.0), axis=-1, keepdims=True) for e in es]
        gate = [jnp.sum(jnp.where(lane == e, gate_ref[...], 0.0), axis=-1, keepdims=True) for e in es]
        pos = [jnp.where(r >= 0.0, r + off_ref[jnp.maximum(e, 0)].astype(F32), -1.0)
               for r, e in zip(rank, es)]
        ys = [_dot(jnp.where(p == (chunk * cr).astype(F32) + iota, 1.0, 0.0).astype(BF16), buf_ref[k % nbuf])
              for p, chunk, k in zip(pos, chunks, ks)]
        acc_ref[...] += gate[0] * ys[0] + gate[1] * ys[1]
        for k in ks:
            start(k + nbuf)
        return carry

    lax.fori_loop(0, n // 2, pair, 0)

    @pl.when(j + 1 < pl.num_programs(0))
    def _prefetch_next_step():
        for d in range(nbuf):
            start(d, j + 1)

    out_ref[...] = _layer_norm(ALPHA * x_ref[...] + acc_ref[...], lng_ref[...], lnb_ref[...])


def _moe_layer(x, router_w, router_b, w13, w2, ln_g, ln_b):
    s = x.shape[0]
    ne, gt, gr, ck = N_EXPERTS, MOE_GROUP_TILE, MOE_GATHER_ROWS, MOE_GATHER_TOKENS
    tt, cr = MOE_COMBINE_TOKENS, MOE_COMBINE_ROWS
    sub = gt // gr
    nck = s // ck
    n_group_tiles = (TOP_K * s) // gt + ne
    n_sub = n_group_tiles * sub

    gate, rank, counts, xb = _router(x, router_w, router_b)

    cnt = counts[0].astype(I32)
    padded = ((cnt + gt - 1) // gt) * gt
    off = (jnp.cumsum(padded) - padded).astype(I32)
    n_used = (jnp.sum(padded) // gt).astype(I32)
    group_end = jnp.cumsum(padded)
    tile_row = jnp.arange(n_group_tiles, dtype=I32) * gt
    tile_expert = jnp.minimum(jnp.sum(group_end[None, :] <= tile_row[:, None], axis=1), ne - 1).astype(I32)
    routed = (rank >= 0.0).reshape(nck, ck, ne)
    cum = jnp.cumsum(jnp.sum(routed, axis=1), axis=0).astype(I32).T

    u = jnp.arange(n_sub, dtype=I32)
    e_u = tile_expert[u // sub]
    r0 = u * gr - off[e_u]
    r1 = jnp.minimum(r0 + gr, cnt[e_u]) - 1
    live = (u // sub < n_used) & (r0 < cnt[e_u])
    cum_u = cum[e_u]
    c_lo = jnp.sum(cum_u <= r0[:, None], axis=1).astype(I32)
    c_hi = jnp.sum(cum_u <= r1[:, None], axis=1).astype(I32)
    n_items = jnp.where(live, c_hi - c_lo + 1, 0)
    item_pad = jnp.sum(n_items.reshape(n_group_tiles, sub), axis=1).astype(I32) % 2
    n_items = n_items + jnp.where(u % sub == sub - 1, item_pad[u // sub], 0)
    item_start = jnp.concatenate([jnp.zeros((1,), I32), jnp.cumsum(n_items).astype(I32)])
    c_lo = jnp.where(live, c_lo, 0)
    real_rows = jnp.clip(cnt[tile_expert] - (tile_row - off[tile_expert]), 0, gt)
    n_real_sub = jnp.clip((real_rows + gr - 1) // gr, 1, sub).astype(I32)

    fc = MOE_FF_CHUNK
    nf = D_FF_E // fc
    tile_of = lambda i, nu: jnp.minimum(i, nu[0] - 1)
    chunk_of = lambda i, f, nu: jnp.where(i < nu[0], f, nf - 1)
    ys = pl.pallas_call(
        _moe_gemm_kernel,
        out_shape=jax.ShapeDtypeStruct((n_group_tiles * gt, D_MODEL), BF16),
        grid_spec=pltpu.PrefetchScalarGridSpec(
            num_scalar_prefetch=7,
            grid=(n_group_tiles, nf),
            in_specs=[pl.BlockSpec(memory_space=pl.ANY),
                      pl.BlockSpec((ne, nck, ck), lambda i, f, te, nu, *_: (0, 0, 0)),
                      pl.BlockSpec((1, D_MODEL, fc),
                                   lambda i, f, te, nu, *_: (te[tile_of(i, nu)], 0, chunk_of(i, f, nu))),
                      pl.BlockSpec((1, D_MODEL, fc),
                                   lambda i, f, te, nu, *_: (te[tile_of(i, nu)], 0, nf + chunk_of(i, f, nu))),
                      pl.BlockSpec((1, fc, D_MODEL),
                                   lambda i, f, te, nu, *_: (te[tile_of(i, nu)], chunk_of(i, f, nu), 0))],
            out_specs=pl.BlockSpec((gt, D_MODEL), lambda i, f, *_: (i, 0)),
            scratch_shapes=[pltpu.VMEM((gt, D_MODEL), BF16), pltpu.VMEM((gt, D_MODEL), F32),
                            pltpu.VMEM((MOE_DMA_DEPTH, ck, D_MODEL), BF16),
                            pltpu.SemaphoreType.DMA((MOE_DMA_DEPTH,))]),
        compiler_params=_params(("arbitrary", "arbitrary"), 58),
        name="moe_gemm",
    )(tile_expert, n_used.reshape(1), off, item_start, c_lo, item_pad, n_real_sub, xb,
      rank.T.reshape(ne, nck, ck), w13, w13, w2)

    njc = s // tt
    per = tt // ck
    incl = cum[:, per - 1::per]
    lo = off[:, None] + jnp.concatenate([jnp.zeros((ne, 1), I32), incl[:, :-1]], axis=1)
    hi = off[:, None] + incl
    first, last = lo // cr, (hi - 1) // cr
    max_chunks = tt // cr + 1
    kk = jnp.arange(max_chunks, dtype=I32)
    slot_chunk = (first[:, :, None] + kk).transpose(1, 0, 2).reshape(njc, -1)
    slot_live = ((hi > lo)[:, :, None] & (first[:, :, None] + kk <= last[:, :, None])).transpose(1, 0, 2)
    slot_live = slot_live.reshape(njc, -1)
    slots = ne * max_chunks
    slot_expert = jnp.broadcast_to(jnp.repeat(jnp.arange(ne, dtype=I32), max_chunks)[None, :], (njc, slots))
    dest = jnp.cumsum(slot_live, axis=1) - 1
    place = slot_live[:, :, None] & (dest[:, :, None] == jnp.arange(slots)[None, None, :])
    item_chunk = jnp.sum(jnp.where(place, slot_chunk[:, :, None], 0), axis=1).astype(I32).reshape(-1)
    item_expert = jnp.sum(jnp.where(place, slot_expert[:, :, None], 0), axis=1).astype(I32)
    item_count = jnp.sum(slot_live, axis=1).astype(I32)
    item_expert = jnp.where(jnp.arange(slots)[None, :] < item_count[:, None], item_expert, -1).reshape(-1)
    item_count = item_count + item_count % 2

    small = pl.BlockSpec((1, D_MODEL), lambda j, *_: (0, 0))
    return pl.pallas_call(
        functools.partial(_combine_kernel, slots=slots),
        out_shape=jax.ShapeDtypeStruct((s, D_MODEL), F32),
        grid_spec=pltpu.PrefetchScalarGridSpec(
            num_scalar_prefetch=4,
            grid=(njc,),
            in_specs=[pl.BlockSpec((tt, D_MODEL), lambda j, *_: (j, 0)),
                      pl.BlockSpec(memory_space=pl.ANY),
                      pl.BlockSpec((tt, ne), lambda j, *_: (j, 0)),
                      pl.BlockSpec((tt, ne), lambda j, *_: (j, 0)),
                      small, small],
            out_specs=pl.BlockSpec((tt, D_MODEL), lambda j, *_: (j, 0)),
            scratch_shapes=[pltpu.VMEM((tt, D_MODEL), F32), pltpu.VMEM((MOE_DMA_DEPTH, cr, D_MODEL), BF16),
                            pltpu.SemaphoreType.DMA((MOE_DMA_DEPTH,))]),
        compiler_params=_params(("arbitrary",), 32),
        name="moe_combine",
    )(off, item_count, item_expert, item_chunk, x, ys, rank, gate, ln_g.reshape(1, -1), ln_b.reshape(1, -1))


def kernel(x, rel_bias, w_in, w_gate, b_gate, conv_w, conv_b, w_qk_m, w_v_m, w_if, b_if, m_norm_g, w_br_a, w_br_m,
           w_o, ln_g, ln_b, ffn_w13, ffn_w2, router_w, router_b, exp_w13, exp_w2):
    batch, seq, _ = x.shape
    assert batch == 1
    h = x.reshape(seq, D_MODEL)
    biases = [_attn_bias(rel_bias, window, dilation) for window, dilation in ATTN_PATTERNS]
    for l in range(DEPTH):
        h = _token_mixer_layer(h, biases, w_in, w_gate, b_gate[l], conv_w[l], conv_b[l], w_qk_m, w_v_m, w_if[l],
                               b_if[l], m_norm_g[l], w_br_a, w_br_m, w_o, ln_g[l, 0], ln_b[l, 0], l)
        j = l // 2
        if l % 2 == 0:
            h = _dense_ffn(h, ffn_w13, ffn_w2, ln_g[l, 1], ln_b[l, 1], j)
        else:
            h = _moe_layer(h, router_w[j], router_b[j], exp_w13[j], exp_w2[j], ln_g[l, 1], ln_b[l, 1])
    return h.reshape(batch, seq, D_MODEL)
```

```python
import functools
import math

import jax
import jax.numpy as jnp
from jax import lax
from jax.experimental import pallas as pl
from jax.experimental.pallas import tpu as pltpu

F32 = jnp.float32
BF16 = jnp.bfloat16
I32 = jnp.int32

D_MODEL = 1024
DEPTH = 2
N_HEADS_A = 8
HEAD_DIM_A = 64
D_A = N_HEADS_A * HEAD_DIM_A
ATTN_PATTERNS = ((128, 1), (512, 4), (2048, 16))
ATTN_BLOCK = 128
NUM_BUCKETS = 32
MAX_DISTANCE = 2048
N_HEADS_M = 4
HEAD_DIM_M = 128
D_M = N_HEADS_M * HEAD_DIM_M
CONV_K = 4
MLSTM_CHUNK = 128
N_PROJ = 5
P_IN = 3 * D_A + 2 * D_M
D_FF = 2816
N_EXPERTS = 8
TOP_K = 2
D_FF_E = 3584
ALPHA = (2.0 * DEPTH) ** 0.25
LN_EPS = 1e-5

NEG = -1e30
LOG2E = math.log2(math.e)
LANES = 128
HALO_ROWS = 16
MIB = 1024 * 1024

MOE_GROUP_TILE = 1024
MOE_FF_CHUNK = 512
MOE_GATHER_ROWS = 256
MOE_GATHER_TOKENS = 256
MOE_COMBINE_TOKENS = 512
MOE_COMBINE_ROWS = 256
MOE_DMA_DEPTH = 6
MLSTM_PREP_ROWS = 1024
MLSTM_STEP_CHUNKS = 8
MERGE_COL_CHUNK = 256
MERGE_ROW_TILE = 1024
MERGE_LOAD_ROWS = 256
FFN_ROW_TILE = 1024
FFN_FF_CHUNK = 256


def _params(sem, vmem_mib):
    return pltpu.CompilerParams(dimension_semantics=sem, vmem_limit_bytes=vmem_mib * MIB)


def _sigmoid(x):
    return 1.0 / (1.0 + jnp.exp(-x))


def _layer_norm(r, g, b):
    mu = jnp.mean(r, axis=-1, keepdims=True)
    c = r - mu
    var = jnp.mean(c * c, axis=-1, keepdims=True)
    return c * lax.rsqrt(var + LN_EPS) * g + b


def _split3(a):
    hi = a.astype(BF16)
    r1 = a - hi.astype(F32)
    mid = r1.astype(BF16)
    lo = (r1 - mid.astype(F32)).astype(BF16)
    return hi, mid, lo


def _dot(a, b):
    return jnp.dot(a, b, preferred_element_type=F32)


def _dot_nt(a, b):
    return lax.dot_general(a, b, (((1,), (1,)), ((), ())), preferred_element_type=F32)


IN_PROJ_ROWS = 1024
DILATIONS = tuple(d for _, d in ATTN_PATTERNS if d > 1)


def _in_proj_kernel(x_ref, w_ref, *refs):
    nat = refs[:N_PROJ]
    perm = refs[N_PROJ:N_PROJ + 3 * len(DILATIONS)]
    wb_ref, y_ref = refs[-2:]

    @pl.when(pl.program_id(0) == 0)
    def _cast_weights():
        wb_ref[...] = w_ref[...].astype(BF16)

    xb = x_ref[...].astype(BF16)
    for j in range(N_PROJ):
        y = _dot(xb, wb_ref[:, j * D_A:(j + 1) * D_A])
        if j == 0:
            y = y * (HEAD_DIM_A ** -0.5 * LOG2E)
        nat[j][...] = y.astype(BF16)
        if j >= 3:
            continue
        for c in range(D_A // LANES):
            y_ref[c] = y[:, c * LANES:(c + 1) * LANES]
        for di, d in enumerate(DILATIONS):
            out = perm[di * 3 + j]
            tiles, _, rpc, _ = out.shape
            for t in range(tiles):
                for r in range(d):
                    for c in range(D_A // LANES):
                        out[t, r, :, c * LANES:(c + 1) * LANES] = (
                            y_ref[c, pl.ds(t * d * rpc + r, rpc, stride=d), :].astype(BF16))


def _in_proj(x, w_in, layer):
    s = x.shape[0]
    tm = IN_PROJ_ROWS
    blk = ATTN_BLOCK
    out_shape = [jax.ShapeDtypeStruct((s, D_A), BF16)] * N_PROJ
    out_specs = [pl.BlockSpec((tm, D_A), lambda i: (i, 0))] * N_PROJ
    for d in DILATIONS:
        tile = d * blk
        if tile <= tm:
            spec = pl.BlockSpec((tm // tile, d, blk, D_A), lambda i: (i, 0, 0, 0))
        else:
            parts = tile // tm
            spec = pl.BlockSpec((1, d, blk // parts, D_A), lambda i, parts=parts: (i // parts, 0, i % parts, 0))
        out_shape += [jax.ShapeDtypeStruct((s // tile, d, blk, D_A), BF16)] * 3
        out_specs += [spec] * 3
    outs = pl.pallas_call(
        _in_proj_kernel,
        out_shape=tuple(out_shape),
        grid=(s // tm,),
        in_specs=[
            pl.BlockSpec((tm, D_MODEL), lambda i: (i, 0)),
            pl.BlockSpec((None, D_MODEL, P_IN), lambda i: (layer, 0, 0), pipeline_mode=pl.Buffered(1)),
        ],
        out_specs=tuple(out_specs),
        scratch_shapes=[pltpu.VMEM((D_MODEL, P_IN), BF16), pltpu.VMEM((D_A // LANES, tm, LANES), F32)],
        compiler_params=_params(("arbitrary",), 56),
        name="in_proj",
    )(x, w_in)
    nat = outs[:N_PROJ]
    perm = [tuple(t.reshape(s, D_A) for t in outs[N_PROJ + 3 * i:N_PROJ + 3 * i + 3]) for i in range(len(DILATIONS))]
    return nat, perm


ATTN_STEP_BLOCKS = 16


def _rel_bucket(dist):
    exact = NUM_BUCKETS // 2
    d = jnp.maximum(dist, exact).astype(F32)
    log_b = exact + (jnp.log(d / exact) / math.log(MAX_DISTANCE / exact) * (NUM_BUCKETS - exact)).astype(I32)
    return jnp.where(dist < exact, dist, jnp.minimum(log_b, NUM_BUCKETS - 1))


def _attn_bias(rel_bias, window, dilation):
    blk = ATTN_BLOCK
    qi = jnp.arange(blk)[:, None]
    kj = jnp.arange(2 * blk)[None, :]
    rel = qi + blk - kj
    bucket = _rel_bucket(jnp.maximum(rel, 0) * dilation)
    onehot = (bucket[..., None] == jnp.arange(NUM_BUCKETS)).astype(F32)
    bias = jnp.einsum("qkb,bh->hqk", onehot, rel_bias.astype(F32), precision=lax.Precision.HIGHEST)
    mask = (rel >= 0) & (rel <= window // dilation)
    return jnp.where(mask[None], bias * LOG2E, NEG)


def _attn_kernel(q_ref, kp_ref, kc_ref, vp_ref, vc_ref, bias_ref, o_ref, st_ref, on_ref, sn_ref, *, dilation):
    blk = ATTN_BLOCK
    nb = ATTN_STEP_BLOCKS
    step = pl.program_id(0)
    kj = lax.broadcasted_iota(I32, (2 * blk, 2 * blk), 1)
    lane = lax.broadcasted_iota(I32, (blk, LANES), 1)
    ones = jnp.ones((2 * blk, LANES), BF16)
    pairs = range(N_HEADS_A // 2)

    def block(g, carry):
        cur = pl.multiple_of(g * blk, blk)
        in_tile = g >= dilation
        prev_c = pl.multiple_of(jnp.maximum(g - dilation, 0) * blk, blk)
        prev_p = pl.multiple_of(jnp.minimum(nb + g - dilation, nb - 1) * blk, blk)
        pen = jnp.where(jnp.logical_and(step == 0, g < dilation), NEG, 0.0).astype(F32)
        prev_pen = jnp.where(kj < blk, pen, 0.0)
        tile_i, cls = g // dilation, g % dilation
        dst = pl.ds(tile_i * (blk * dilation) + cls, blk, stride=dilation)

        def keys(cur_ref, prev_ref, ls):
            prev = jnp.where(in_tile, cur_ref[pl.ds(prev_c, blk), ls], prev_ref[pl.ds(prev_p, blk), ls])
            return jnp.concatenate([prev, cur_ref[pl.ds(cur, blk), ls]], axis=0)

        even = (lane // HEAD_DIM_A) == 0
        lss = [slice(hp * LANES, (hp + 1) * LANES) for hp in pairs]
        qb = [q_ref[pl.ds(cur, blk), lss[hp]] for hp in pairs]
        qcat = [jnp.concatenate([jnp.where(even, qb[hp], jnp.zeros_like(qb[hp])),
                                 jnp.where(even, jnp.zeros_like(qb[hp]), qb[hp])], axis=0) for hp in pairs]
        kb = [keys(kc_ref, kp_ref, lss[hp]) for hp in pairs]
        v1 = [jnp.concatenate([keys(vc_ref, vp_ref, lss[hp]), ones], axis=1) for hp in pairs]
        logits = [_dot_nt(qcat[hp], kb[hp]) + bias_ref[hp] + prev_pen for hp in pairs]
        m = [jnp.max(logits[hp], axis=-1, keepdims=True) for hp in pairs]
        p = [jnp.exp2(logits[hp] - m[hp]).astype(BF16) for hp in pairs]
        ol = [_dot(p[hp], v1[hp]) for hp in pairs]
        stats = jnp.zeros((blk, LANES), F32)
        for hp in pairs:
            for par in range(2):
                h, rs = 2 * hp + par, slice(par * blk, (par + 1) * blk)
                stats = jnp.where(lane == h, m[hp][rs], stats)
                stats = jnp.where(lane == N_HEADS_A + h, ol[hp][rs, LANES:], stats)
            on_ref[hp, dst, :] = jnp.where(even, ol[hp][:blk, :LANES], ol[hp][blk:, :LANES])
        sn_ref[dst, :] = stats
        return carry

    lax.fori_loop(0, nb, block, 0, unroll=2)
    for hp in range(N_HEADS_A // 2):
        o_ref[:, hp * LANES:(hp + 1) * LANES] = on_ref[hp].astype(o_ref.dtype)
    st_ref[...] = sn_ref[...]


def _attn_pattern(q, k, v, bias, dilation):
    s = q.shape[0]
    rows = ATTN_STEP_BLOCKS * ATTN_BLOCK
    cur = pl.BlockSpec((rows, D_A), lambda i: (i, 0))
    prev = pl.BlockSpec((rows, D_A), lambda i: (jnp.maximum(i - 1, 0), 0))
    return pl.pallas_call(
        functools.partial(_attn_kernel, dilation=dilation),
        out_shape=(jax.ShapeDtypeStruct((s, D_A), BF16), jax.ShapeDtypeStruct((s, LANES), F32)),
        grid=(s // rows,),
        in_specs=[cur, prev, cur, prev, cur,
                  pl.BlockSpec((N_HEADS_A // 2, 2 * ATTN_BLOCK, 2 * ATTN_BLOCK), lambda i: (0, 0, 0))],
        out_specs=(pl.BlockSpec((rows, D_A), lambda i: (i, 0)), pl.BlockSpec((rows, LANES), lambda i: (i, 0))),
        scratch_shapes=[pltpu.VMEM((D_A // LANES, rows, LANES), F32), pltpu.VMEM((rows, LANES), F32)],
        compiler_params=_params(("arbitrary",), 48),
        name=f"attn_d{dilation}",
    )(q, k, k, v, v, bias.reshape(N_HEADS_A // 2, 2 * ATTN_BLOCK, 2 * ATTN_BLOCK))


def _log_sigmoid(x):
    return jnp.minimum(x, 0.0) - jnp.log(1.0 + jnp.exp(-jnp.abs(x)))


def _mlstm_prep_kernel(xm_ref, halo_ref, cw_ref, cb_ref, wqk_ref, wkt_ref, wv_ref, wif_ref, wift_ref, bif_ref,
                       bift_ref, q_ref, k_ref, kt_ref, v_ref, gc_ref, gr_ref):
    tm = xm_ref.shape[0]
    hm, hd, lc = N_HEADS_M, HEAD_DIM_M, MLSTM_CHUNK
    xmb = xm_ref[...]
    halo = jnp.where(pl.program_id(0) == 0, 0.0, halo_ref[...].astype(F32))
    xx = jnp.concatenate([halo, xmb.astype(F32)], axis=0)
    conv = jnp.zeros((tm, D_M), F32) + cb_ref[...]
    for j in range(CONV_K):
        start = HALO_ROWS - (CONV_K - 1) + j
        conv = conv + cw_ref[j:j + 1, :] * xx[start:start + tm, :]
    xcb = (conv * _sigmoid(conv)).astype(BF16)

    qs, ks, vs = [], [], []
    for h in range(hm):
        sl = slice(h * hd, (h + 1) * hd)
        qs.append(_dot(xcb[:, sl], wqk_ref[0, h].astype(BF16)))
        ks.append(_dot(xcb[:, sl], wqk_ref[1, h].astype(BF16)))
        vs.append(_dot(xmb[:, sl], wv_ref[h].astype(BF16)))
        kt = _dot_nt(wkt_ref[h].astype(BF16), xcb[:, sl])
        kt_ref[sl, :] = (kt * (hd ** -0.5)).astype(BF16)
    q = jnp.concatenate(qs, axis=1)
    k = jnp.concatenate(ks, axis=1)
    v = jnp.concatenate(vs, axis=1)
    q_ref[...] = q.astype(BF16)
    k_ref[...] = (k * (hd ** -0.5)).astype(BF16)
    v_ref[...] = v.astype(BF16)

    qkv = jnp.concatenate([q, k, v], axis=1).astype(BF16)
    gates_c = _dot(qkv, wif_ref[...].astype(BF16)) + bif_ref[...]
    gates_r = _dot_nt(wift_ref[...].astype(BF16), qkv) + bift_ref[...]
    lane = lax.broadcasted_iota(I32, gates_c.shape, 1)
    row = lax.broadcasted_iota(I32, gates_r.shape, 0)
    gc_ref[...] = jnp.where(lane < hm, gates_c, _log_sigmoid(gates_c))
    gr_ref[...] = jnp.where(row < hm, gates_r, _log_sigmoid(gates_r))

    ri = lax.broadcasted_iota(I32, (lc, lc), 0)
    ci = lax.broadcasted_iota(I32, (lc, lc), 1)
    lower = (ri >= ci).astype(BF16)
    upper = (ri <= ci).astype(BF16)
    lane_c = lax.broadcasted_iota(I32, (lc, 2 * hm), 1)
    row_c = lax.broadcasted_iota(I32, (2 * hm, lc), 0)
    for c in range(tm // lc):
        rs = slice(c * lc, (c + 1) * lc)
        gcc = gc_ref[rs, :]
        grc = gr_ref[:, rs]
        cum_c = sum(_dot(lower, part) for part in _split3(gcc))
        cum_r = sum(_dot(part, upper) for part in _split3(grc))
        gc_ref[rs, :] = jnp.where(lane_c < hm, gcc, cum_c)
        gr_ref[:, rs] = jnp.where(row_c < hm, grc, cum_r)


def _mlstm_prep(xm, conv_w, conv_b, w_qk_m, w_v_m, w_if, b_if, layer):
    s = xm.shape[0]
    tm = MLSTM_PREP_ROWS
    hpt = tm // HALO_ROWS
    full = lambda shape: pl.BlockSpec(shape, lambda i: (0,) * len(shape))
    stacked = lambda shape: pl.BlockSpec((None,) + shape, lambda i: (layer,) + (0,) * len(shape))
    row_spec = pl.BlockSpec((tm, D_M), lambda i: (i, 0))
    head_sq = (N_HEADS_M, HEAD_DIM_M, HEAD_DIM_M)
    wk_t = jnp.swapaxes(w_qk_m[layer, 1], -1, -2)
    return pl.pallas_call(
        _mlstm_prep_kernel,
        out_shape=(jax.ShapeDtypeStruct((s, D_M), BF16), jax.ShapeDtypeStruct((s, D_M), BF16),
                   jax.ShapeDtypeStruct((D_M, s), BF16), jax.ShapeDtypeStruct((s, D_M), BF16),
                   jax.ShapeDtypeStruct((s, 2 * N_HEADS_M), F32), jax.ShapeDtypeStruct((2 * N_HEADS_M, s), F32)),
        grid=(s // tm,),
        in_specs=[
            row_spec,
            pl.BlockSpec((HALO_ROWS, D_M), lambda i: (jnp.maximum(i * hpt - 1, 0), 0)),
            full((CONV_K, D_M)), full((1, D_M)),
            stacked((2,) + head_sq), full(head_sq), stacked(head_sq),
            full((3 * D_M, 2 * N_HEADS_M)), full((2 * N_HEADS_M, 3 * D_M)),
            full((1, 2 * N_HEADS_M)), full((2 * N_HEADS_M, 1)),
        ],
        out_specs=(row_spec, row_spec, pl.BlockSpec((D_M, tm), lambda i: (0, i)), row_spec,
                   pl.BlockSpec((tm, 2 * N_HEADS_M), lambda i: (i, 0)),
                   pl.BlockSpec((2 * N_HEADS_M, tm), lambda i: (0, i))),
        compiler_params=_params(("arbitrary",), 32),
        name="mlstm_prep",
    )(xm, xm, conv_w, conv_b.reshape(1, D_M), w_qk_m, wk_t, w_v_m, w_if, w_if.T,
      b_if.reshape(1, -1), b_if.reshape(-1, 1))


def _mlstm_scan_kernel(q_ref, k_ref, kt_ref, v_ref, gc_ref, gr_ref, z_ref, g_ref, y_ref, c_ref, m_ref, *, chunks):
    hm, hd, lc = N_HEADS_M, HEAD_DIM_M, MLSTM_CHUNK

    @pl.when(pl.program_id(0) == 0)
    def _init():
        c_ref[...] = jnp.zeros_like(c_ref)
        m_ref[...] = jnp.zeros_like(m_ref)

    ri = lax.broadcasted_iota(I32, (lc, lc), 0)
    ci = lax.broadcasted_iota(I32, (lc, lc), 1)
    causal = ri >= ci
    ones = jnp.ones((lc, hd), BF16)
    heads = range(hm)
    hsl = [slice(h * hd, (h + 1) * hd) for h in heads]
    m_state = [m_ref[h:h + 1, 0:1] for h in heads]
    for c in range(chunks):
        rs = slice(c * lc, (c + 1) * lc)
        gc = gc_ref[rs, :]
        gr = gr_ref[:, rs]
        qs = [q_ref[rs, hsl[h]] for h in heads]
        v1 = [jnp.concatenate([v_ref[rs, hsl[h]], ones], axis=1) for h in heads]
        i_row = [gr[h:h + 1, :] for h in heads]
        b_row = [gr[hm + h:hm + h + 1, :] for h in heads]
        b_col = [gc[:, hm + h:hm + h + 1] for h in heads]
        qk = [_dot_nt(qs[h], k_ref[rs, hsl[h]]) for h in heads]
        c_prev = [c_ref[h] for h in heads]
        qc = [_dot(qs[h], c_prev[h].astype(BF16)) for h in heads]
        dm = [jnp.where(causal, b_col[h] - b_row[h] + i_row[h], NEG) for h in heads]
        inter = [b_col[h] + m_state[h] for h in heads]
        m_loc = [jnp.maximum(inter[h], jnp.max(dm[h], axis=-1, keepdims=True)) for h in heads]
        sc = [(qk[h] * jnp.exp(dm[h] - m_loc[h])).astype(BF16) for h in heads]
        both = [jnp.exp(inter[h] - m_loc[h]) * qc[h] + _dot(sc[h], v1[h]) for h in heads]
        hval = [both[h][:, :hd] / jnp.maximum(jnp.abs(both[h][:, hd:]), jnp.exp(-m_loc[h])) for h in heads]

        b_last = [b_row[h][:, lc - 1:lc] for h in heads]
        g_row = [b_last[h] - b_row[h] + i_row[h] for h in heads]
        m_new = [jnp.maximum(b_last[h] + m_state[h], jnp.max(g_row[h], axis=-1, keepdims=True)) for h in heads]
        ktw = [(kt_ref[hsl[h], rs].astype(F32) * jnp.exp(g_row[h] - m_new[h])).astype(BF16) for h in heads]
        for h in heads:
            c_ref[h] = jnp.exp(b_last[h] + m_state[h] - m_new[h]) * c_prev[h] + _dot(ktw[h], v1[h])
        m_state = m_new

        mu = [jnp.mean(hval[h], axis=-1, keepdims=True) for h in heads]
        cen = [hval[h] - mu[h] for h in heads]
        var = [jnp.mean(cen[h] * cen[h], axis=-1, keepdims=True) for h in heads]
        for h in heads:
            hn = cen[h] * lax.rsqrt(var[h] + LN_EPS) * g_ref[:, hsl[h]]
            y_ref[rs, hsl[h]] = (_sigmoid(z_ref[rs, hsl[h]].astype(F32)) * hn).astype(y_ref.dtype)
    for h in heads:
        m_ref[h:h + 1, :] = jnp.broadcast_to(m_state[h], (1, LANES))


def _mlstm_scan(q, k, kt, v, gc, gr, z, m_norm_g):
    s = q.shape[0]
    chunks = MLSTM_STEP_CHUNKS
    tm = chunks * MLSTM_CHUNK
    row_spec = pl.BlockSpec((tm, D_M), lambda i: (i, 0))
    return pl.pallas_call(
        functools.partial(_mlstm_scan_kernel, chunks=chunks),
        out_shape=jax.ShapeDtypeStruct((s, D_M), BF16),
        grid=(s // tm,),
        in_specs=[row_spec, row_spec, pl.BlockSpec((D_M, tm), lambda i: (0, i)), row_spec,
                  pl.BlockSpec((tm, 2 * N_HEADS_M), lambda i: (i, 0)),
                  pl.BlockSpec((2 * N_HEADS_M, tm), lambda i: (0, i)),
                  row_spec,
                  pl.BlockSpec((1, D_M), lambda i: (0, 0))],
        out_specs=row_spec,
        scratch_shapes=[pltpu.VMEM((N_HEADS_M, HEAD_DIM_M, 2 * HEAD_DIM_M), F32),
                        pltpu.VMEM((8, LANES), F32)],
        compiler_params=_params(("arbitrary",), 32),
        name="mlstm_scan",
    )(q, k, kt, v, gc, gr, z, m_norm_g.reshape(1, D_M))


def _merge_kernel(x_ref, o1_ref, o2_ref, o3_ref, l1_ref, l2_ref, l3_ref, ym_ref,
                  wg_hbm, bg_ref, wa_hbm, wm_hbm, wo_hbm, lng_ref, lnb_ref, out_ref,
                  wgb_ref, wab_ref, wmb_ref, wob_ref, stg_ref, st_ref, sem_ref, *, layer):
    @pl.when(pl.program_id(0) == 0)
    def _load_weights():
        rc = st_ref.shape[1]
        rows = lambda c: pl.ds(c * rc, rc)
        _load_cast([(wg_hbm.at[layer, rows(c), :], wgb_ref.at[rows(c), :]) for c in range(D_MODEL // rc)],
                   stg_ref, sem_ref)
        _load_cast([(src.at[layer, rows(c), :], dst.at[rows(c), :])
                    for src, dst in ((wa_hbm, wab_ref), (wm_hbm, wmb_ref), (wo_hbm, wob_ref))
                    for c in range(dst.shape[0] // rc)], st_ref, sem_ref)

    x = x_ref[...]
    xb = x.astype(BF16)
    stats = (l1_ref[...], l2_ref[...], l3_ref[...])
    mx = jnp.maximum(jnp.maximum(stats[0], stats[1]), stats[2])
    es = [jnp.exp2(st - mx) for st in stats]
    ls = [pltpu.roll(st, LANES - N_HEADS_A, 1) for st in stats]
    den = es[0] * ls[0] + es[1] * ls[1] + es[2] * ls[2]
    head_lane = lax.broadcasted_iota(I32, den.shape, 1) < N_HEADS_A
    inv = jnp.where(head_lane, 1.0 / den, 0.0)
    hrow = lax.broadcasted_iota(I32, (LANES, D_A), 0)
    hcol = lax.broadcasted_iota(I32, (LANES, D_A), 1) // HEAD_DIM_A
    expand = (hrow == hcol).astype(BF16)
    ya = jnp.zeros((x.shape[0], D_A), F32)
    for e, o_ref in zip(es, (o1_ref, o2_ref, o3_ref)):
        w = sum(_dot(part, expand) for part in _split3(e * inv)[:2])
        ya = ya + w * o_ref[...].astype(F32)

    yab, ymb = ya.astype(BF16), ym_ref[...]
    mc = MERGE_COL_CHUNK
    n_chunks = D_MODEL // mc

    def pre(c):
        ca = slice(c * mc, (c + 1) * mc)
        cm = slice(D_MODEL + c * mc, D_MODEL + (c + 1) * mc)
        return (_dot(xb, wgb_ref[:, ca]) + bg_ref[:, ca], _dot(xb, wgb_ref[:, cm]) + bg_ref[:, cm],
                _dot(yab, wab_ref[:, ca]), _dot(ymb, wmb_ref[:, ca]))

    nxt = pre(0)
    y = None
    for c in range(n_chunks):
        za, zm, pa, pm = nxt
        if c + 1 < n_chunks:
            nxt = pre(c + 1)
        merged = _sigmoid(za) * pa + _sigmoid(zm) * pm
        part = _dot(merged.astype(BF16), wob_ref[c * mc:(c + 1) * mc, :])
        y = part if y is None else y + part
    out_ref[...] = _layer_norm(ALPHA * x + y, lng_ref[...], lnb_ref[...])


def _merge(x, outs, lses, ym, w_gate, b_gate, w_br_a, w_br_m, w_o, ln_g, ln_b, layer):
    s = x.shape[0]
    tm = MERGE_ROW_TILE
    hbm = pl.BlockSpec(memory_space=pl.ANY)
    small = lambda n: pl.BlockSpec((1, n), lambda i: (0, 0))
    rows = lambda n: pl.BlockSpec((tm, n), lambda i: (i, 0))
    rc = MERGE_LOAD_ROWS
    return pl.pallas_call(
        functools.partial(_merge_kernel, layer=layer),
        out_shape=jax.ShapeDtypeStruct((s, D_MODEL), F32),
        grid=(s // tm,),
        in_specs=[rows(D_MODEL), rows(D_A), rows(D_A), rows(D_A), rows(LANES), rows(LANES), rows(LANES), rows(D_M),
                  hbm, small(2 * D_MODEL), hbm, hbm, hbm, small(D_MODEL), small(D_MODEL)],
        out_specs=rows(D_MODEL),
        scratch_shapes=[pltpu.VMEM((D_MODEL, 2 * D_MODEL), BF16), pltpu.VMEM((D_A, D_MODEL), BF16),
                        pltpu.VMEM((D_M, D_MODEL), BF16), pltpu.VMEM((D_MODEL, D_MODEL), BF16),
                        pltpu.VMEM((2, rc, 2 * D_MODEL), F32), pltpu.VMEM((2, rc, D_MODEL), F32),
                        pltpu.SemaphoreType.DMA((2,))],
        compiler_params=_params(("arbitrary",), 56),
        name="merge",
    )(x, *outs, *lses, ym, w_gate, b_gate.reshape(1, -1), w_br_a, w_br_m, w_o,
      ln_g.reshape(1, -1), ln_b.reshape(1, -1))


def _token_mixer_layer(x, biases, w_in, w_gate, b_gate, conv_w, conv_b, w_qk_m, w_v_m, w_if, b_if, m_norm_g,
                       w_br_a, w_br_m, w_o, ln_g, ln_b, layer):
    (q, k, v, xm, zm), perm = _in_proj(x, w_in, layer)
    outs, stats = [], []
    for (_, dilation), bias in zip(ATTN_PATTERNS, biases):
        qd, kd, vd = (q, k, v) if dilation == 1 else perm[DILATIONS.index(dilation)]
        o, st = _attn_pattern(qd, kd, vd, bias, dilation)
        outs.append(o)
        stats.append(st)
    qm, km, ktm, vm, gc, gr = _mlstm_prep(xm, conv_w, conv_b, w_qk_m, w_v_m, w_if, b_if, layer)
    ym = _mlstm_scan(qm, km, ktm, vm, gc, gr, zm, m_norm_g)
    return _merge(x, outs, stats, ym, w_gate, b_gate, w_br_a, w_br_m, w_o, ln_g, ln_b, layer)


def _load_cast(chunks, stage_ref, sem_ref):
    copies = [pltpu.make_async_copy(src, stage_ref.at[k % 2], sem_ref.at[k % 2]) for k, (src, _) in enumerate(chunks)]
    copies[0].start()
    for k, (_, dst) in enumerate(chunks):
        if k + 1 < len(chunks):
            copies[k + 1].start()
        copies[k].wait()
        dst[...] = stage_ref[k % 2].astype(BF16)


def _ffn_kernel(x_ref, w13_hbm, w2_hbm, lng_ref, lnb_ref, out_ref, w13b_ref, w2b_ref, acc_ref, st13_ref, st2_ref,
                sem_ref, *, j):
    fc = FFN_FF_CHUNK

    @pl.when(pl.program_id(0) == 0)
    def _load_weights():
        cols = lambda c: pl.ds(c * fc, fc)
        _load_cast([(w13_hbm.at[j, :, cols(c)], w13b_ref.at[:, cols(c)]) for c in range(2 * D_FF // fc)],
                   st13_ref, sem_ref)
        _load_cast([(w2_hbm.at[j, cols(c), :], w2b_ref.at[cols(c), :]) for c in range(D_FF // fc)],
                   st2_ref, sem_ref)

    xb = x_ref[...].astype(BF16)
    n_chunks = D_FF // fc

    def up(c):
        return (_dot(xb, w13b_ref[:, c * fc:(c + 1) * fc]),
                _dot(xb, w13b_ref[:, D_FF + c * fc:D_FF + (c + 1) * fc]))

    nxt = up(0)
    for c in range(n_chunks):
        a, g = nxt
        if c + 1 < n_chunks:
            nxt = up(c + 1)
        y = _dot((a * _sigmoid(a) * g).astype(BF16), w2b_ref[c * fc:(c + 1) * fc, :])
        if c == 0:
            acc_ref[...] = y
        else:
            acc_ref[...] += y
    out_ref[...] = _layer_norm(ALPHA * x_ref[...] + acc_ref[...], lng_ref[...], lnb_ref[...])


def _dense_ffn(x, w13, w2, ln_g, ln_b, j):
    s = x.shape[0]
    tm = min(FFN_ROW_TILE, s)
    fc = FFN_FF_CHUNK
    small = pl.BlockSpec((1, D_MODEL), lambda i: (0, 0))
    return pl.pallas_call(
        functools.partial(_ffn_kernel, j=j),
        out_shape=jax.ShapeDtypeStruct((s, D_MODEL), F32),
        grid=(s // tm,),
        in_specs=[pl.BlockSpec((tm, D_MODEL), lambda i: (i, 0)),
                  pl.BlockSpec(memory_space=pl.ANY), pl.BlockSpec(memory_space=pl.ANY),
                  small, small],
        out_specs=pl.BlockSpec((tm, D_MODEL), lambda i: (i, 0)),
        scratch_shapes=[pltpu.VMEM((D_MODEL, 2 * D_FF), BF16), pltpu.VMEM((D_FF, D_MODEL), BF16),
                        pltpu.VMEM((tm, D_MODEL), F32),
                        pltpu.VMEM((2, D_MODEL, fc), F32), pltpu.VMEM((2, fc, D_MODEL), F32),
                        pltpu.SemaphoreType.DMA((2,))],
        compiler_params=_params(("arbitrary",), 56),
        name="dense_ffn",
    )(x, w13, w2, ln_g.reshape(1, -1), ln_b.reshape(1, -1))


def _router_kernel(x_ref, rw_ref, rb_ref, gate_ref, rank_ref, cnt_ref, xb_ref, carry_ref):
    tm = x_ref.shape[0]
    ne = N_EXPERTS

    @pl.when(pl.program_id(0) == 0)
    def _init():
        carry_ref[...] = jnp.zeros_like(carry_ref)

    x = x_ref[...]
    xb_ref[...] = x.astype(BF16)
    xs = _split3(x)
    ws = _split3(rw_ref[...])
    logits = rb_ref[...] + sum(_dot(xs[a], ws[b]) for a, b in ((1, 0), (0, 1), (0, 0)))
    lane = lax.broadcasted_iota(I32, (tm, ne), 1)
    v1 = jnp.max(logits, axis=-1, keepdims=True)
    i1 = jnp.min(jnp.where(logits == v1, lane, ne), axis=-1, keepdims=True)
    rest = jnp.where(lane == i1, -jnp.inf, logits)
    v2 = jnp.max(rest, axis=-1, keepdims=True)
    i2 = jnp.min(jnp.where(rest == v2, lane, ne), axis=-1, keepdims=True)
    e2 = jnp.exp(v2 - v1)
    den = 1.0 + e2
    sel1, sel2 = lane == i1, lane == i2
    gate_ref[...] = jnp.where(sel1, 1.0 / den, 0.0) + jnp.where(sel2, e2 / den, 0.0)
    sel = jnp.where(sel1 | sel2, 1.0, 0.0)
    ri = lax.broadcasted_iota(I32, (tm, tm), 0)
    ci = lax.broadcasted_iota(I32, (tm, tm), 1)
    before = (ri > ci).astype(BF16)
    carry = carry_ref[0:1, 0:ne]
    rank = _dot(before, sel.astype(BF16)) + carry
    rank_ref[...] = jnp.where(sel > 0.0, rank, -1.0)
    total = carry + jnp.sum(sel, axis=0, keepdims=True)
    carry_ref[0:1, 0:ne] = total
    cnt_ref[...] = total


def _router(x, router_w, router_b):
    s = x.shape[0]
    tm = 512
    ne = N_EXPERTS
    return pl.pallas_call(
        _router_kernel,
        out_shape=(jax.ShapeDtypeStruct((s, ne), F32), jax.ShapeDtypeStruct((s, ne), F32),
                   jax.ShapeDtypeStruct((1, ne), F32), jax.ShapeDtypeStruct((s, D_MODEL), BF16)),
        grid=(s // tm,),
        in_specs=[pl.BlockSpec((tm, D_MODEL), lambda i: (i, 0)),
                  pl.BlockSpec((D_MODEL, ne), lambda i: (0, 0)),
                  pl.BlockSpec((1, ne), lambda i: (0, 0))],
        out_specs=(pl.BlockSpec((tm, ne), lambda i: (i, 0)), pl.BlockSpec((tm, ne), lambda i: (i, 0)),
                   pl.BlockSpec((1, ne), lambda i: (0, 0)), pl.BlockSpec((tm, D_MODEL), lambda i: (i, 0))),
        scratch_shapes=[pltpu.VMEM((8, LANES), F32)],
        compiler_params=_params(("arbitrary",), 32),
        name="moe_router",
    )(x, router_w, router_b.reshape(1, ne))


def _chunk_copy(src_hbm, buf_ref, sem_ref, chunk, slot):
    rows = buf_ref.shape[1]
    start = pl.multiple_of(chunk * rows, rows)
    return pltpu.make_async_copy(src_hbm.at[pl.ds(start, rows), :], buf_ref.at[slot], sem_ref.at[slot])


def _moe_gemm_kernel(te_ref, nu_ref, off_ref, ist_ref, clo_ref, pad_ref, nsub_ref, xb_hbm, rank_ref, w1_ref, w3_ref,
                     w2_ref, y_ref, xs_ref, acc_ref, buf_ref, sem_ref):
    gr, ck = MOE_GATHER_ROWS, MOE_GATHER_TOKENS
    sub = MOE_GROUP_TILE // gr
    nbuf = buf_ref.shape[0]
    last_chunk = rank_ref.shape[1] - 1
    i, f = pl.program_id(0), pl.program_id(1)
    last_f = pl.num_programs(1) - 1

    def item(t, k):
        base = t * sub
        s = sum((k >= ist_ref[base + j]).astype(I32) for j in range(1, sub))
        return s, jnp.minimum(clo_ref[base + s] + (k - ist_ref[base + s]), last_chunk)

    def start(t, k):
        @pl.when(k < ist_ref[t * sub + sub])
        def _():
            _chunk_copy(xb_hbm, buf_ref, sem_ref, item(t, k)[1], (k - ist_ref[t * sub]) % nbuf).start()

    @pl.when(i < nu_ref[0])
    def _tile():
        @pl.when(f == 0)
        def _gather_rows():
            e = te_ref[i]
            base = i * sub
            k0, k1 = ist_ref[base], ist_ref[base + sub]
            k_real = k1 - pad_ref[i]
            xs_ref[...] = jnp.zeros_like(xs_ref)
            acc_ref[...] = jnp.zeros_like(acc_ref)

            @pl.when(i == 0)
            def _first_tile():
                for d in range(nbuf):
                    start(i, k0 + d)

            def pair(t, carry):
                ks = (k0 + 2 * t, k0 + 2 * t + 1)
                its = [item(i, k) for k in ks]
                slots = [(k - k0) % nbuf for k in ks]
                for (_, chunk), slot in zip(its, slots):
                    _chunk_copy(xb_hbm, buf_ref, sem_ref, chunk, slot).wait()
                iota = lax.broadcasted_iota(I32, (gr, ck), 0).astype(F32)
                rank0 = [jnp.where(k < k_real, (base + s) * gr - off_ref[e], -(2 ** 30)).astype(F32)
                         for k, (s, _) in zip(ks, its)]
                hit = [(r0 + iota) == rank_ref[e, pl.ds(chunk, 1), :] for r0, (_, chunk) in zip(rank0, its)]
                ys = [_dot(jnp.where(h, 1.0, 0.0).astype(BF16), buf_ref[slot]) for h, slot in zip(hit, slots)]
                for (s, _), y in zip(its, ys):
                    rows = pl.ds(pl.multiple_of(s * gr, gr), gr)
                    xs_ref[rows, :] = xs_ref[rows, :] + y.astype(BF16)
                for k in ks:
                    start(i, k + nbuf)
                return carry

            lax.fori_loop(0, (k1 - k0) // 2, pair, 0)

            @pl.when(i + 1 < nu_ref[0])
            def _prefetch_next_tile():
                for d in range(nbuf):
                    start(i + 1, k1 + d)

        def ffn_rows(m):
            def run():
                xb = xs_ref[0:m, :]
                a = _dot(xb, w1_ref[0].astype(BF16))
                g = _dot(xb, w3_ref[0].astype(BF16))
                acc_ref[0:m, :] += _dot((a * _sigmoid(a) * g).astype(BF16), w2_ref[0].astype(BF16))
            return run

        lax.switch(nsub_ref[i] - 1, [ffn_rows((r + 1) * gr) for r in range(sub)])

        @pl.when(f == last_f)
        def _finish():
            y_ref[...] = acc_ref[...].astype(y_ref.dtype)

    @pl.when(jnp.logical_and(i >= nu_ref[0], f == last_f))
    def _unused_tile():
        y_ref[...] = jnp.zeros_like(y_ref)


def _combine_kernel(off_ref, cn_ref, cie_ref, cic_ref, x_ref, ys_hbm, rank_ref, gate_ref, lng_ref, lnb_ref,
                    out_ref, acc_ref, buf_ref, sem_ref, *, slots):
    tt, cr = MOE_COMBINE_TOKENS, MOE_COMBINE_ROWS
    j = pl.program_id(0)
    n = cn_ref[j]
    base = j * slots
    acc_ref[...] = jnp.zeros_like(acc_ref)
    nbuf = buf_ref.shape[0]

    def start(k, step=j):
        @pl.when(k < cn_ref[step])
        def _():
            chunk = cic_ref[step * slots + jnp.minimum(k, slots - 1)]
            _chunk_copy(ys_hbm, buf_ref, sem_ref, chunk, k % nbuf).start()

    @pl.when(j == 0)
    def _first_step():
        for d in range(nbuf):
            start(d)

    lane = lax.broadcasted_iota(I32, (tt, N_EXPERTS), 1)
    iota = lax.broadcasted_iota(I32, (tt, cr), 1).astype(F32)

    def pair(t, carry):
        ks = (2 * t, 2 * t + 1)
        es = [cie_ref[base + k] for k in ks]
        chunks = [cic_ref[base + k] for k in ks]
        for k, chunk in zip(ks, chunks):
            _chunk_copy(ys_hbm, buf_ref, sem_ref, chunk, k % nbuf).wait()
        rank = [jnp.max(jnp.where(lane == e, rank_ref[...], -1.0), axis=-1, keepdims=True) for e in es]
        gate = [jnp.sum(jnp.where(lane == e, gate_ref[...], 0.0), axis=-1, keepdims=True) for e in es]
        pos = [jnp.where(r >= 0.0, r + off_ref[jnp.maximum(e, 0)].astype(F32), -1.0)
               for r, e in zip(rank, es)]
        ys = [_dot(jnp.where(p == (chunk * cr).astype(F32) + iota, 1.0, 0.0).astype(BF16), buf_ref[k % nbuf])
              for p, chunk, k in zip(pos, chunks, ks)]
        acc_ref[...] += gate[0] * ys[0] + gate[1] * ys[1]
        for k in ks:
            start(k + nbuf)
        return carry

    lax.fori_loop(0, n // 2, pair, 0)

    @pl.when(j + 1 < pl.num_programs(0))
    def _prefetch_next_step():
        for d in range(nbuf):
            start(d, j + 1)

    out_ref[...] = _layer_norm(ALPHA * x_ref[...] + acc_ref[...], lng_ref[...], lnb_ref[...])


def _moe_layer(x, router_w, router_b, w13, w2, ln_g, ln_b):
    s = x.shape[0]
    ne, gt, gr, ck = N_EXPERTS, MOE_GROUP_TILE, MOE_GATHER_ROWS, MOE_GATHER_TOKENS
    tt, cr = MOE_COMBINE_TOKENS, MOE_COMBINE_ROWS
    sub = gt // gr
    nck = s // ck
    n_group_tiles = (TOP_K * s) // gt + ne
    n_sub = n_group_tiles * sub

    gate, rank, counts, xb = _router(x, router_w, router_b)

    cnt = counts[0].astype(I32)
    padded = ((cnt + gt - 1) // gt) * gt
    off = (jnp.cumsum(padded) - padded).astype(I32)
    n_used = (jnp.sum(padded) // gt).astype(I32)
    group_end = jnp.cumsum(padded)
    tile_row = jnp.arange(n_group_tiles, dtype=I32) * gt
    tile_expert = jnp.minimum(jnp.sum(group_end[None, :] <= tile_row[:, None], axis=1), ne - 1).astype(I32)
    routed = (rank >= 0.0).reshape(nck, ck, ne)
    cum = jnp.cumsum(jnp.sum(routed, axis=1), axis=0).astype(I32).T

    u = jnp.arange(n_sub, dtype=I32)
    e_u = tile_expert[u // sub]
    r0 = u * gr - off[e_u]
    r1 = jnp.minimum(r0 + gr, cnt[e_u]) - 1
    live = (u // sub < n_used) & (r0 < cnt[e_u])
    cum_u = cum[e_u]
    c_lo = jnp.sum(cum_u <= r0[:, None], axis=1).astype(I32)
    c_hi = jnp.sum(cum_u <= r1[:, None], axis=1).astype(I32)
    n_items = jnp.where(live, c_hi - c_lo + 1, 0)
    item_pad = jnp.sum(n_items.reshape(n_group_tiles, sub), axis=1).astype(I32) % 2
    n_items = n_items + jnp.where(u % sub == sub - 1, item_pad[u // sub], 0)
    item_start = jnp.concatenate([jnp.zeros((1,), I32), jnp.cumsum(n_items).astype(I32)])
    c_lo = jnp.where(live, c_lo, 0)
    real_rows = jnp.clip(cnt[tile_expert] - (tile_row - off[tile_expert]), 0, gt)
    n_real_sub = jnp.clip((real_rows + gr - 1) // gr, 1, sub).astype(I32)

    fc = MOE_FF_CHUNK
    nf = D_FF_E // fc
    tile_of = lambda i, nu: jnp.minimum(i, nu[0] - 1)
    chunk_of = lambda i, f, nu: jnp.where(i < nu[0], f, nf - 1)
    ys = pl.pallas_call(
        _moe_gemm_kernel,
        out_shape=jax.ShapeDtypeStruct((n_group_tiles * gt, D_MODEL), BF16),
        grid_spec=pltpu.PrefetchScalarGridSpec(
            num_scalar_prefetch=7,
            grid=(n_group_tiles, nf),
            in_specs=[pl.BlockSpec(memory_space=pl.ANY),
                      pl.BlockSpec((ne, nck, ck), lambda i, f, te, nu, *_: (0, 0, 0)),
                      pl.BlockSpec((1, D_MODEL, fc),
                                   lambda i, f, te, nu, *_: (te[tile_of(i, nu)], 0, chunk_of(i, f, nu))),
                      pl.BlockSpec((1, D_MODEL, fc),
                                   lambda i, f, te, nu, *_: (te[tile_of(i, nu)], 0, nf + chunk_of(i, f, nu))),
                      pl.BlockSpec((1, fc, D_MODEL),
                                   lambda i, f, te, nu, *_: (te[tile_of(i, nu)], chunk_of(i, f, nu), 0))],
            out_specs=pl.BlockSpec((gt, D_MODEL), lambda i, f, *_: (i, 0)),
            scratch_shapes=[pltpu.VMEM((gt, D_MODEL), BF16), pltpu.VMEM((gt, D_MODEL), F32),
                            pltpu.VMEM((MOE_DMA_DEPTH, ck, D_MODEL), BF16),
                            pltpu.SemaphoreType.DMA((MOE_DMA_DEPTH,))]),
        compiler_params=_params(("arbitrary", "arbitrary"), 58),
        name="moe_gemm",
    )(tile_expert, n_used.reshape(1), off, item_start, c_lo, item_pad, n_real_sub, xb,
      rank.T.reshape(ne, nck, ck), w13, w13, w2)

    njc = s // tt
    per = tt // ck
    incl = cum[:, per - 1::per]
    lo = off[:, None] + jnp.concatenate([jnp.zeros((ne, 1), I32), incl[:, :-1]], axis=1)
    hi = off[:, None] + incl
    first, last = lo // cr, (hi - 1) // cr
    max_chunks = tt // cr + 1
    kk = jnp.arange(max_chunks, dtype=I32)
    slot_chunk = (first[:, :, None] + kk).transpose(1, 0, 2).reshape(njc, -1)
    slot_live = ((hi > lo)[:, :, None] & (first[:, :, None] + kk <= last[:, :, None])).transpose(1, 0, 2)
    slot_live = slot_live.reshape(njc, -1)
    slots = ne * max_chunks
    slot_expert = jnp.broadcast_to(jnp.repeat(jnp.arange(ne, dtype=I32), max_chunks)[None, :], (njc, slots))
    dest = jnp.cumsum(slot_live, axis=1) - 1
    place = slot_live[:, :, None] & (dest[:, :, None] == jnp.arange(slots)[None, None, :])
    item_chunk = jnp.sum(jnp.where(place, slot_chunk[:, :, None], 0), axis=1).astype(I32).reshape(-1)
    item_expert = jnp.sum(jnp.where(place, slot_expert[:, :, None], 0), axis=1).astype(I32)
    item_count = jnp.sum(slot_live, axis=1).astype(I32)
    item_expert = jnp.where(jnp.arange(slots)[None, :] < item_count[:, None], item_expert, -1).reshape(-1)
    item_count = item_count + item_count % 2

    small = pl.BlockSpec((1, D_MODEL), lambda j, *_: (0, 0))
    return pl.pallas_call(
        functools.partial(_combine_kernel, slots=slots),
        out_shape=jax.ShapeDtypeStruct((s, D_MODEL), F32),
        grid_spec=pltpu.PrefetchScalarGridSpec(
            num_scalar_prefetch=4,
            grid=(njc,),
            in_specs=[pl.BlockSpec((tt, D_MODEL), lambda j, *_: (j, 0)),
                      pl.BlockSpec(memory_space=pl.ANY),
                      pl.BlockSpec((tt, ne), lambda j, *_: (j, 0)),
                      pl.BlockSpec((tt, ne), lambda j, *_: (j, 0)),
                      small, small],
            out_specs=pl.BlockSpec((tt, D_MODEL), lambda j, *_: (j, 0)),
            scratch_shapes=[pltpu.VMEM((tt, D_MODEL), F32), pltpu.VMEM((MOE_DMA_DEPTH, cr, D_MODEL), BF16),
                            pltpu.SemaphoreType.DMA((MOE_DMA_DEPTH,))]),
        compiler_params=_params(("arbitrary",), 32),
        name="moe_combine",
    )(off, item_count, item_expert, item_chunk, x, ys, rank, gate, ln_g.reshape(1, -1), ln_b.reshape(1, -1))


def kernel(x, rel_bias, w_in, w_gate, b_gate, conv_w, conv_b, w_qk_m, w_v_m, w_if, b_if, m_norm_g, w_br_a, w_br_m,
           w_o, ln_g, ln_b, ffn_w13, ffn_w2, router_w, router_b, exp_w13, exp_w2):
    batch, seq, _ = x.shape
    assert batch == 1
    h = x.reshape(seq, D_MODEL)
    biases = [_attn_bias(rel_bias, window, dilation) for window, dilation in ATTN_PATTERNS]
    for l in range(DEPTH):
        h = _token_mixer_layer(h, biases, w_in, w_gate, b_gate[l], conv_w[l], conv_b[l], w_qk_m, w_v_m, w_if[l],
                               b_if[l], m_norm_g[l], w_br_a, w_br_m, w_o, ln_g[l, 0], ln_b[l, 0], l)
        j = l // 2
        if l % 2 == 0:
            h = _dense_ffn(h, ffn_w13, ffn_w2, ln_g[l, 1], ln_b[l, 1], j)
        else:
            h = _moe_layer(h, router_w[j], router_b[j], exp_w13[j], exp_w2[j], ln_g[l, 1], ln_b[l, 1])
    return h.reshape(batch, seq, D_MODEL)
```

```python
import functools
import math

import jax
import jax.numpy as jnp
from jax import lax
from jax.experimental import pallas as pl
from jax.experimental.pallas import tpu as pltpu

F32 = jnp.float32
BF16 = jnp.bfloat16
I32 = jnp.int32

D_MODEL = 1024
DEPTH = 2
N_HEADS_A = 8
HEAD_DIM_A = 64
D_A = N_HEADS_A * HEAD_DIM_A
ATTN_PATTERNS = ((128, 1), (512, 4), (2048, 16))
ATTN_BLOCK = 128
NUM_BUCKETS = 32
MAX_DISTANCE = 2048
N_HEADS_M = 4
HEAD_DIM_M = 128
D_M = N_HEADS_M * HEAD_DIM_M
CONV_K = 4
MLSTM_CHUNK = 128
N_PROJ = 5
P_IN = 3 * D_A + 2 * D_M
D_FF = 2816
N_EXPERTS = 8
TOP_K = 2
D_FF_E = 3584
ALPHA = (2.0 * DEPTH) ** 0.25
LN_EPS = 1e-5

NEG = -1e30
LOG2E = math.log2(math.e)
LANES = 128
HALO_ROWS = 16
MIB = 1024 * 1024

MOE_GROUP_TILE = 1024
MOE_FF_CHUNK = 512
MOE_GATHER_ROWS = 256
MOE_GATHER_TOKENS = 256
MOE_COMBINE_TOKENS = 512
MOE_COMBINE_ROWS = 256
MOE_DMA_DEPTH = 6
MLSTM_PREP_ROWS = 1024
MLSTM_STEP_CHUNKS = 8
MERGE_COL_CHUNK = 256
MERGE_ROW_TILE = 1024
MERGE_LOAD_ROWS = 256
FFN_ROW_TILE = 1024
FFN_FF_CHUNK = 256


def _params(sem, vmem_mib):
    return pltpu.CompilerParams(dimension_semantics=sem, vmem_limit_bytes=vmem_mib * MIB)


def _sigmoid(x):
    return 1.0 / (1.0 + jnp.exp(-x))


def _layer_norm(r, g, b):
    mu = jnp.mean(r, axis=-1, keepdims=True)
    c = r - mu
    var = jnp.mean(c * c, axis=-1, keepdims=True)
    return c * lax.rsqrt(var + LN_EPS) * g + b


def _split3(a):
    hi = a.astype(BF16)
    r1 = a - hi.astype(F32)
    mid = r1.astype(BF16)
    lo = (r1 - mid.astype(F32)).astype(BF16)
    return hi, mid, lo


def _dot(a, b):
    return jnp.dot(a, b, preferred_element_type=F32)


def _dot_nt(a, b):
    return lax.dot_general(a, b, (((1,), (1,)), ((), ())), preferred_element_type=F32)


IN_PROJ_ROWS = 1024
DILATIONS = tuple(d for _, d in ATTN_PATTERNS if d > 1)


def _in_proj_kernel(x_ref, w_ref, *refs):
    nat = refs[:N_PROJ]
    perm = refs[N_PROJ:N_PROJ + 3 * len(DILATIONS)]
    wb_ref, y_ref = refs[-2:]

    @pl.when(pl.program_id(0) == 0)
    def _cast_weights():
        wb_ref[...] = w_ref[...].astype(BF16)

    xb = x_ref[...].astype(BF16)
    for j in range(N_PROJ):
        y = _dot(xb, wb_ref[:, j * D_A:(j + 1) * D_A])
        if j == 0:
            y = y * (HEAD_DIM_A ** -0.5 * LOG2E)
        nat[j][...] = y.astype(BF16)
        if j >= 3:
            continue
        for c in range(D_A // LANES):
            y_ref[c] = y[:, c * LANES:(c + 1) * LANES]
        for di, d in enumerate(DILATIONS):
            out = perm[di * 3 + j]
            tiles, _, rpc, _ = out.shape
            for t in range(tiles):
                for r in range(d):
                    for c in range(D_A // LANES):
                        out[t, r, :, c * LANES:(c + 1) * LANES] = (
                            y_ref[c, pl.ds(t * d * rpc + r, rpc, stride=d), :].astype(BF16))


def _in_proj(x, w_in, layer):
    s = x.shape[0]
    tm = IN_PROJ_ROWS
    blk = ATTN_BLOCK
    out_shape = [jax.ShapeDtypeStruct((s, D_A), BF16)] * N_PROJ
    out_specs = [pl.BlockSpec((tm, D_A), lambda i: (i, 0))] * N_PROJ
    for d in DILATIONS:
        tile = d * blk
        if tile <= tm:
            spec = pl.BlockSpec((tm // tile, d, blk, D_A), lambda i: (i, 0, 0, 0))
        else:
            parts = tile // tm
            spec = pl.BlockSpec((1, d, blk // parts, D_A), lambda i, parts=parts: (i // parts, 0, i % parts, 0))
        out_shape += [jax.ShapeDtypeStruct((s // tile, d, blk, D_A), BF16)] * 3
        out_specs += [spec] * 3
    outs = pl.pallas_call(
        _in_proj_kernel,
        out_shape=tuple(out_shape),
        grid=(s // tm,),
        in_specs=[
            pl.BlockSpec((tm, D_MODEL), lambda i: (i, 0)),
            pl.BlockSpec((None, D_MODEL, P_IN), lambda i: (layer, 0, 0), pipeline_mode=pl.Buffered(1)),
        ],
        out_specs=tuple(out_specs),
        scratch_shapes=[pltpu.VMEM((D_MODEL, P_IN), BF16), pltpu.VMEM((D_A // LANES, tm, LANES), F32)],
        compiler_params=_params(("arbitrary",), 56),
        name="in_proj",
    )(x, w_in)
    nat = outs[:N_PROJ]
    perm = [tuple(t.reshape(s, D_A) for t in outs[N_PROJ + 3 * i:N_PROJ + 3 * i + 3]) for i in range(len(DILATIONS))]
    return nat, perm


ATTN_STEP_BLOCKS = 16
ATTN_UNROLL = 4


def _rel_bucket(dist):
    exact = NUM_BUCKETS // 2
    d = jnp.maximum(dist, exact).astype(F32)
    log_b = exact + (jnp.log(d / exact) / math.log(MAX_DISTANCE / exact) * (NUM_BUCKETS - exact)).astype(I32)
    return jnp.where(dist < exact, dist, jnp.minimum(log_b, NUM_BUCKETS - 1))


def _attn_bias(rel_bias, window, dilation):
    blk = ATTN_BLOCK
    qi = jnp.arange(blk)[:, None]
    kj = jnp.arange(2 * blk)[None, :]
    rel = qi + blk - kj
    bucket = _rel_bucket(jnp.maximum(rel, 0) * dilation)
    onehot = (bucket[..., None] == jnp.arange(NUM_BUCKETS)).astype(F32)
    bias = jnp.einsum("qkb,bh->hqk", onehot, rel_bias.astype(F32), precision=lax.Precision.HIGHEST)
    mask = (rel >= 0) & (rel <= window // dilation)
    return jnp.where(mask[None], bias * LOG2E, NEG)


def _attn_kernel(q_ref, kp_ref, kc_ref, vp_ref, vc_ref, bias_ref, o_ref, st_ref, on_ref, sn_ref, *, dilation):
    blk = ATTN_BLOCK
    nb = ATTN_STEP_BLOCKS
    step = pl.program_id(0)
    kj = lax.broadcasted_iota(I32, (2 * blk, 2 * blk), 1)
    lane = lax.broadcasted_iota(I32, (blk, LANES), 1)
    ones = jnp.ones((2 * blk, LANES), BF16)
    pairs = range(N_HEADS_A // 2)

    def block(g, carry, first_step):
        cur = pl.multiple_of(g * blk, blk)
        in_tile = g >= dilation
        prev_c = pl.multiple_of(jnp.maximum(g - dilation, 0) * blk, blk)
        prev_p = pl.multiple_of(jnp.minimum(nb + g - dilation, nb - 1) * blk, blk)
        if first_step:
            pen = jnp.where(g < dilation, NEG, 0.0).astype(F32)
            prev_pen = jnp.where(kj < blk, pen, 0.0)
        tile_i, cls = g // dilation, g % dilation
        dst = pl.ds(tile_i * (blk * dilation) + cls, blk, stride=dilation)

        def keys(cur_ref, prev_ref, ls):
            prev = jnp.where(in_tile, cur_ref[pl.ds(prev_c, blk), ls], prev_ref[pl.ds(prev_p, blk), ls])
            return jnp.concatenate([prev, cur_ref[pl.ds(cur, blk), ls]], axis=0)

        even = (lane // HEAD_DIM_A) == 0
        lss = [slice(hp * LANES, (hp + 1) * LANES) for hp in pairs]
        qb = [q_ref[pl.ds(cur, blk), lss[hp]] for hp in pairs]
        qcat = [jnp.concatenate([jnp.where(even, qb[hp], jnp.zeros_like(qb[hp])),
                                 jnp.where(even, jnp.zeros_like(qb[hp]), qb[hp])], axis=0) for hp in pairs]
        kb = [keys(kc_ref, kp_ref, lss[hp]) for hp in pairs]
        v1 = [jnp.concatenate([keys(vc_ref, vp_ref, lss[hp]), ones], axis=1) for hp in pairs]
        logits = [_dot_nt(qcat[hp], kb[hp]) + bias_ref[hp] for hp in pairs]
        if first_step:
            logits = [lg + prev_pen for lg in logits]
        m = [jnp.max(logits[hp], axis=-1, keepdims=True) for hp in pairs]
        p = [jnp.exp2(logits[hp] - m[hp]).astype(BF16) for hp in pairs]
        ol = [_dot(p[hp], v1[hp]) for hp in pairs]
        stats = jnp.zeros((blk, LANES), F32)
        for hp in pairs:
            for par in range(2):
                h, rs = 2 * hp + par, slice(par * blk, (par + 1) * blk)
                stats = jnp.where(lane == h, m[hp][rs], stats)
                stats = jnp.where(lane == N_HEADS_A + h, ol[hp][rs, LANES:], stats)
            on_ref[hp, dst, :] = jnp.where(even, ol[hp][:blk, :LANES], ol[hp][blk:, :LANES])
        sn_ref[dst, :] = stats
        return carry

    def run(first_step):
        def loop():
            lax.fori_loop(0, nb, functools.partial(block, first_step=first_step), 0, unroll=ATTN_UNROLL)
        return loop

    lax.cond(step == 0, run(True), run(False))
    for hp in range(N_HEADS_A // 2):
        o_ref[:, hp * LANES:(hp + 1) * LANES] = on_ref[hp].astype(o_ref.dtype)
    st_ref[...] = sn_ref[...]


def _attn_pattern(q, k, v, bias, dilation):
    s = q.shape[0]
    rows = ATTN_STEP_BLOCKS * ATTN_BLOCK
    cur = pl.BlockSpec((rows, D_A), lambda i: (i, 0))
    prev = pl.BlockSpec((rows, D_A), lambda i: (jnp.maximum(i - 1, 0), 0))
    return pl.pallas_call(
        functools.partial(_attn_kernel, dilation=dilation),
        out_shape=(jax.ShapeDtypeStruct((s, D_A), BF16), jax.ShapeDtypeStruct((s, LANES), F32)),
        grid=(s // rows,),
        in_specs=[cur, prev, cur, prev, cur,
                  pl.BlockSpec((N_HEADS_A // 2, 2 * ATTN_BLOCK, 2 * ATTN_BLOCK), lambda i: (0, 0, 0))],
        out_specs=(pl.BlockSpec((rows, D_A), lambda i: (i, 0)), pl.BlockSpec((rows, LANES), lambda i: (i, 0))),
        scratch_shapes=[pltpu.VMEM((D_A // LANES, rows, LANES), F32), pltpu.VMEM((rows, LANES), F32)],
        compiler_params=_params(("arbitrary",), 48),
        name=f"attn_d{dilation}",
    )(q, k, k, v, v, bias.reshape(N_HEADS_A // 2, 2 * ATTN_BLOCK, 2 * ATTN_BLOCK))


def _log_sigmoid(x):
    return jnp.minimum(x, 0.0) - jnp.log(1.0 + jnp.exp(-jnp.abs(x)))


def _mlstm_prep_kernel(xm_ref, halo_ref, cw_ref, cb_ref, wqk_ref, wkt_ref, wv_ref, wif_ref, wift_ref, bif_ref,
                       bift_ref, q_ref, k_ref, kt_ref, v_ref, gc_ref, gr_ref):
    tm = xm_ref.shape[0]
    hm, hd, lc = N_HEADS_M, HEAD_DIM_M, MLSTM_CHUNK
    xmb = xm_ref[...]
    halo = jnp.where(pl.program_id(0) == 0, 0.0, halo_ref[...].astype(F32))
    xx = jnp.concatenate([halo, xmb.astype(F32)], axis=0)
    conv = jnp.zeros((tm, D_M), F32) + cb_ref[...]
    for j in range(CONV_K):
        start = HALO_ROWS - (CONV_K - 1) + j
        conv = conv + cw_ref[j:j + 1, :] * xx[start:start + tm, :]
    xcb = (conv * _sigmoid(conv)).astype(BF16)

    qs, ks, vs = [], [], []
    for h in range(hm):
        sl = slice(h * hd, (h + 1) * hd)
        qs.append(_dot(xcb[:, sl], wqk_ref[0, h].astype(BF16)))
        ks.append(_dot(xcb[:, sl], wqk_ref[1, h].astype(BF16)))
        vs.append(_dot(xmb[:, sl], wv_ref[h].astype(BF16)))
        kt = _dot_nt(wkt_ref[h].astype(BF16), xcb[:, sl])
        kt_ref[sl, :] = (kt * (hd ** -0.5)).astype(BF16)
    q = jnp.concatenate(qs, axis=1)
    k = jnp.concatenate(ks, axis=1)
    v = jnp.concatenate(vs, axis=1)
    q_ref[...] = q.astype(BF16)
    k_ref[...] = (k * (hd ** -0.5)).astype(BF16)
    v_ref[...] = v.astype(BF16)

    qkv = jnp.concatenate([q, k, v], axis=1).astype(BF16)
    gates_c = _dot(qkv, wif_ref[...].astype(BF16)) + bif_ref[...]
    gates_r = _dot_nt(wift_ref[...].astype(BF16), qkv) + bift_ref[...]
    lane = lax.broadcasted_iota(I32, gates_c.shape, 1)
    row = lax.broadcasted_iota(I32, gates_r.shape, 0)
    gc_ref[...] = jnp.where(lane < hm, gates_c, _log_sigmoid(gates_c))
    gr_ref[...] = jnp.where(row < hm, gates_r, _log_sigmoid(gates_r))

    ri = lax.broadcasted_iota(I32, (lc, lc), 0)
    ci = lax.broadcasted_iota(I32, (lc, lc), 1)
    lower = (ri >= ci).astype(BF16)
    upper = (ri <= ci).astype(BF16)
    lane_c = lax.broadcasted_iota(I32, (lc, 2 * hm), 1)
    row_c = lax.broadcasted_iota(I32, (2 * hm, lc), 0)
    for c in range(tm // lc):
        rs = slice(c * lc, (c + 1) * lc)
        gcc = gc_ref[rs, :]
        grc = gr_ref[:, rs]
        cum_c = sum(_dot(lower, part) for part in _split3(gcc))
        cum_r = sum(_dot(part, upper) for part in _split3(grc))
        gc_ref[rs, :] = jnp.where(lane_c < hm, gcc, cum_c)
        gr_ref[:, rs] = jnp.where(row_c < hm, grc, cum_r)


def _mlstm_prep(xm, conv_w, conv_b, w_qk_m, w_v_m, w_if, b_if, layer):
    s = xm.shape[0]
    tm = MLSTM_PREP_ROWS
    hpt = tm // HALO_ROWS
    full = lambda shape: pl.BlockSpec(shape, lambda i: (0,) * len(shape))
    stacked = lambda shape: pl.BlockSpec((None,) + shape, lambda i: (layer,) + (0,) * len(shape))
    row_spec = pl.BlockSpec((tm, D_M), lambda i: (i, 0))
    head_sq = (N_HEADS_M, HEAD_DIM_M, HEAD_DIM_M)
    wk_t = jnp.swapaxes(w_qk_m[layer, 1], -1, -2)
    return pl.pallas_call(
        _mlstm_prep_kernel,
        out_shape=(jax.ShapeDtypeStruct((s, D_M), BF16), jax.ShapeDtypeStruct((s, D_M), BF16),
                   jax.ShapeDtypeStruct((D_M, s), BF16), jax.ShapeDtypeStruct((s, D_M), BF16),
                   jax.ShapeDtypeStruct((s, 2 * N_HEADS_M), F32), jax.ShapeDtypeStruct((2 * N_HEADS_M, s), F32)),
        grid=(s // tm,),
        in_specs=[
            row_spec,
            pl.BlockSpec((HALO_ROWS, D_M), lambda i: (jnp.maximum(i * hpt - 1, 0), 0)),
            full((CONV_K, D_M)), full((1, D_M)),
            stacked((2,) + head_sq), full(head_sq), stacked(head_sq),
            full((3 * D_M, 2 * N_HEADS_M)), full((2 * N_HEADS_M, 3 * D_M)),
            full((1, 2 * N_HEADS_M)), full((2 * N_HEADS_M, 1)),
        ],
        out_specs=(row_spec, row_spec, pl.BlockSpec((D_M, tm), lambda i: (0, i)), row_spec,
                   pl.BlockSpec((tm, 2 * N_HEADS_M), lambda i: (i, 0)),
                   pl.BlockSpec((2 * N_HEADS_M, tm), lambda i: (0, i))),
        compiler_params=_params(("arbitrary",), 32),
        name="mlstm_prep",
    )(xm, xm, conv_w, conv_b.reshape(1, D_M), w_qk_m, wk_t, w_v_m, w_if, w_if.T,
      b_if.reshape(1, -1), b_if.reshape(-1, 1))


def _mlstm_scan_kernel(q_ref, k_ref, kt_ref, v_ref, gc_ref, gr_ref, z_ref, g_ref, y_ref, c_ref, m_ref, *, chunks):
    hm, hd, lc = N_HEADS_M, HEAD_DIM_M, MLSTM_CHUNK

    @pl.when(pl.program_id(0) == 0)
    def _init():
        c_ref[...] = jnp.zeros_like(c_ref)
        m_ref[...] = jnp.zeros_like(m_ref)

    ri = lax.broadcasted_iota(I32, (lc, lc), 0)
    ci = lax.broadcasted_iota(I32, (lc, lc), 1)
    causal = ri >= ci
    ones = jnp.ones((lc, hd), BF16)
    heads = range(hm)
    hsl = [slice(h * hd, (h + 1) * hd) for h in heads]
    m_state = [m_ref[h:h + 1, 0:1] for h in heads]
    for c in range(chunks):
        rs = slice(c * lc, (c + 1) * lc)
        gc = gc_ref[rs, :]
        gr = gr_ref[:, rs]
        qs = [q_ref[rs, hsl[h]] for h in heads]
        v1 = [jnp.concatenate([v_ref[rs, hsl[h]], ones], axis=1) for h in heads]
        i_row = [gr[h:h + 1, :] for h in heads]
        b_row = [gr[hm + h:hm + h + 1, :] for h in heads]
        b_col = [gc[:, hm + h:hm + h + 1] for h in heads]
        qk = [_dot_nt(qs[h], k_ref[rs, hsl[h]]) for h in heads]
        c_prev = [c_ref[h] for h in heads]
        qc = [_dot(qs[h], c_prev[h].astype(BF16)) for h in heads]
        dm = [jnp.where(causal, b_col[h] - b_row[h] + i_row[h], NEG) for h in heads]
        inter = [b_col[h] + m_state[h] for h in heads]
        m_loc = [jnp.maximum(inter[h], jnp.max(dm[h], axis=-1, keepdims=True)) for h in heads]
        sc = [(qk[h] * jnp.exp(dm[h] - m_loc[h])).astype(BF16) for h in heads]
        both = [jnp.exp(inter[h] - m_loc[h]) * qc[h] + _dot(sc[h], v1[h]) for h in heads]
        hval = [both[h][:, :hd] / jnp.maximum(jnp.abs(both[h][:, hd:]), jnp.exp(-m_loc[h])) for h in heads]

        b_last = [b_row[h][:, lc - 1:lc] for h in heads]
        g_row = [b_last[h] - b_row[h] + i_row[h] for h in heads]
        m_new = [jnp.maximum(b_last[h] + m_state[h], jnp.max(g_row[h], axis=-1, keepdims=True)) for h in heads]
        ktw = [(kt_ref[hsl[h], rs].astype(F32) * jnp.exp(g_row[h] - m_new[h])).astype(BF16) for h in heads]
        for h in heads:
            c_ref[h] = jnp.exp(b_last[h] + m_state[h] - m_new[h]) * c_prev[h] + _dot(ktw[h], v1[h])
        m_state = m_new

        mu = [jnp.mean(hval[h], axis=-1, keepdims=True) for h in heads]
        cen = [hval[h] - mu[h] for h in heads]
        var = [jnp.mean(cen[h] * cen[h], axis=-1, keepdims=True) for h in heads]
        for h in heads:
            hn = cen[h] * lax.rsqrt(var[h] + LN_EPS) * g_ref[:, hsl[h]]
            y_ref[rs, hsl[h]] = (_sigmoid(z_ref[rs, hsl[h]].astype(F32)) * hn).astype(y_ref.dtype)
    for h in heads:
        m_ref[h:h + 1, :] = jnp.broadcast_to(m_state[h], (1, LANES))


def _mlstm_scan(q, k, kt, v, gc, gr, z, m_norm_g):
    s = q.shape[0]
    chunks = MLSTM_STEP_CHUNKS
    tm = chunks * MLSTM_CHUNK
    row_spec = pl.BlockSpec((tm, D_M), lambda i: (i, 0))
    return pl.pallas_call(
        functools.partial(_mlstm_scan_kernel, chunks=chunks),
        out_shape=jax.ShapeDtypeStruct((s, D_M), BF16),
        grid=(s // tm,),
        in_specs=[row_spec, row_spec, pl.BlockSpec((D_M, tm), lambda i: (0, i)), row_spec,
                  pl.BlockSpec((tm, 2 * N_HEADS_M), lambda i: (i, 0)),
                  pl.BlockSpec((2 * N_HEADS_M, tm), lambda i: (0, i)),
                  row_spec,
                  pl.BlockSpec((1, D_M), lambda i: (0, 0))],
        out_specs=row_spec,
        scratch_shapes=[pltpu.VMEM((N_HEADS_M, HEAD_DIM_M, 2 * HEAD_DIM_M), F32),
                        pltpu.VMEM((8, LANES), F32)],
        compiler_params=_params(("arbitrary",), 32),
        name="mlstm_scan",
    )(q, k, kt, v, gc, gr, z, m_norm_g.reshape(1, D_M))


def _merge_kernel(x_ref, o1_ref, o2_ref, o3_ref, l1_ref, l2_ref, l3_ref, ym_ref,
                  wg_hbm, bg_ref, wa_hbm, wm_hbm, wo_hbm, lng_ref, lnb_ref, out_ref,
                  wgb_ref, wab_ref, wmb_ref, wob_ref, stg_ref, st_ref, sem_ref, *, layer):
    @pl.when(pl.program_id(0) == 0)
    def _load_weights():
        rc = st_ref.shape[1]
        rows = lambda c: pl.ds(c * rc, rc)
        _load_cast([(wg_hbm.at[layer, rows(c), :], wgb_ref.at[rows(c), :]) for c in range(D_MODEL // rc)],
                   stg_ref, sem_ref)
        _load_cast([(src.at[layer, rows(c), :], dst.at[rows(c), :])
                    for src, dst in ((wa_hbm, wab_ref), (wm_hbm, wmb_ref), (wo_hbm, wob_ref))
                    for c in range(dst.shape[0] // rc)], st_ref, sem_ref)

    x = x_ref[...]
    xb = x.astype(BF16)
    stats = (l1_ref[...], l2_ref[...], l3_ref[...])
    mx = jnp.maximum(jnp.maximum(stats[0], stats[1]), stats[2])
    es = [jnp.exp2(st - mx) for st in stats]
    ls = [pltpu.roll(st, LANES - N_HEADS_A, 1) for st in stats]
    den = es[0] * ls[0] + es[1] * ls[1] + es[2] * ls[2]
    head_lane = lax.broadcasted_iota(I32, den.shape, 1) < N_HEADS_A
    inv = jnp.where(head_lane, 1.0 / den, 0.0)
    n_pat = len(es)
    packed = jnp.zeros_like(den)
    for p, e in enumerate(es):
        for t, part in enumerate(_split3(e * inv)[:2]):
            g = 2 * p + t
            term = part.astype(F32)
            packed = packed + (term if g == 0 else pltpu.roll(term, N_HEADS_A * g, 1))
    row = lax.broadcasted_iota(I32, (LANES, n_pat * D_A), 0)
    col = lax.broadcasted_iota(I32, (LANES, n_pat * D_A), 1)
    expand = ((row < 2 * n_pat * N_HEADS_A) & (row // (2 * N_HEADS_A) == col // D_A)
              & (row % N_HEADS_A == (col % D_A) // HEAD_DIM_A)).astype(BF16)
    w_all = _dot(packed.astype(BF16), expand)
    ya = jnp.zeros((x.shape[0], D_A), F32)
    for p, o_ref in enumerate((o1_ref, o2_ref, o3_ref)):
        ya = ya + w_all[:, p * D_A:(p + 1) * D_A] * o_ref[...].astype(F32)

    yab, ymb = ya.astype(BF16), ym_ref[...]
    mc = MERGE_COL_CHUNK
    n_chunks = D_MODEL // mc

    def pre(c):
        ca = slice(c * mc, (c + 1) * mc)
        cm = slice(D_MODEL + c * mc, D_MODEL + (c + 1) * mc)
        return (_dot(xb, wgb_ref[:, ca]) + bg_ref[:, ca], _dot(xb, wgb_ref[:, cm]) + bg_ref[:, cm],
                _dot(yab, wab_ref[:, ca]), _dot(ymb, wmb_ref[:, ca]))

    nxt = pre(0)
    y = None
    for c in range(n_chunks):
        za, zm, pa, pm = nxt
        if c + 1 < n_chunks:
            nxt = pre(c + 1)
        merged = _sigmoid(za) * pa + _sigmoid(zm) * pm
        part = _dot(merged.astype(BF16), wob_ref[c * mc:(c + 1) * mc, :])
        y = part if y is None else y + part
    out_ref[...] = _layer_norm(ALPHA * x + y, lng_ref[...], lnb_ref[...])


def _merge(x, outs, lses, ym, w_gate, b_gate, w_br_a, w_br_m, w_o, ln_g, ln_b, layer):
    s = x.shape[0]
    tm = MERGE_ROW_TILE
    hbm = pl.BlockSpec(memory_space=pl.ANY)
    small = lambda n: pl.BlockSpec((1, n), lambda i: (0, 0))
    rows = lambda n: pl.BlockSpec((tm, n), lambda i: (i, 0))
    rc = MERGE_LOAD_ROWS
    return pl.pallas_call(
        functools.partial(_merge_kernel, layer=layer),
        out_shape=jax.ShapeDtypeStruct((s, D_MODEL), F32),
        grid=(s // tm,),
        in_specs=[rows(D_MODEL), rows(D_A), rows(D_A), rows(D_A), rows(LANES), rows(LANES), rows(LANES), rows(D_M),
                  hbm, small(2 * D_MODEL), hbm, hbm, hbm, small(D_MODEL), small(D_MODEL)],
        out_specs=rows(D_MODEL),
        scratch_shapes=[pltpu.VMEM((D_MODEL, 2 * D_MODEL), BF16), pltpu.VMEM((D_A, D_MODEL), BF16),
                        pltpu.VMEM((D_M, D_MODEL), BF16), pltpu.VMEM((D_MODEL, D_MODEL), BF16),
                        pltpu.VMEM((2, rc, 2 * D_MODEL), F32), pltpu.VMEM((2, rc, D_MODEL), F32),
                        pltpu.SemaphoreType.DMA((2,))],
        compiler_params=_params(("arbitrary",), 56),
        name="merge",
    )(x, *outs, *lses, ym, w_gate, b_gate.reshape(1, -1), w_br_a, w_br_m, w_o,
      ln_g.reshape(1, -1), ln_b.reshape(1, -1))


def _token_mixer_layer(x, biases, w_in, w_gate, b_gate, conv_w, conv_b, w_qk_m, w_v_m, w_if, b_if, m_norm_g,
                       w_br_a, w_br_m, w_o, ln_g, ln_b, layer):
    (q, k, v, xm, zm), perm = _in_proj(x, w_in, layer)
    outs, stats = [], []
    for (_, dilation), bias in zip(ATTN_PATTERNS, biases):
        qd, kd, vd = (q, k, v) if dilation == 1 else perm[DILATIONS.index(dilation)]
        o, st = _attn_pattern(qd, kd, vd, bias, dilation)
        outs.append(o)
        stats.append(st)
    qm, km, ktm, vm, gc, gr = _mlstm_prep(xm, conv_w, conv_b, w_qk_m, w_v_m, w_if, b_if, layer)
    ym = _mlstm_scan(qm, km, ktm, vm, gc, gr, zm, m_norm_g)
    return _merge(x, outs, stats, ym, w_gate, b_gate, w_br_a, w_br_m, w_o, ln_g, ln_b, layer)


def _load_cast(chunks, stage_ref, sem_ref):
    copies = [pltpu.make_async_copy(src, stage_ref.at[k % 2], sem_ref.at[k % 2]) for k, (src, _) in enumerate(chunks)]
    copies[0].start()
    for k, (_, dst) in enumerate(chunks):
        if k + 1 < len(chunks):
            copies[k + 1].start()
        copies[k].wait()
        dst[...] = stage_ref[k % 2].astype(BF16)


def _ffn_kernel(x_ref, w13_hbm, w2_hbm, lng_ref, lnb_ref, out_ref, w13b_ref, w2b_ref, acc_ref, st13_ref, st2_ref,
                sem_ref, *, j):
    fc = FFN_FF_CHUNK

    @pl.when(pl.program_id(0) == 0)
    def _load_weights():
        cols = lambda c: pl.ds(c * fc, fc)
        _load_cast([(w13_hbm.at[j, :, cols(c)], w13b_ref.at[:, cols(c)]) for c in range(2 * D_FF // fc)],
                   st13_ref, sem_ref)
        _load_cast([(w2_hbm.at[j, cols(c), :], w2b_ref.at[cols(c), :]) for c in range(D_FF // fc)],
                   st2_ref, sem_ref)

    xb = x_ref[...].astype(BF16)
    n_chunks = D_FF // fc

    def up(c):
        return (_dot(xb, w13b_ref[:, c * fc:(c + 1) * fc]),
                _dot(xb, w13b_ref[:, D_FF + c * fc:D_FF + (c + 1) * fc]))

    nxt = up(0)
    for c in range(n_chunks):
        a, g = nxt
        if c + 1 < n_chunks:
            nxt = up(c + 1)
        y = _dot((a * _sigmoid(a) * g).astype(BF16), w2b_ref[c * fc:(c + 1) * fc, :])
        if c == 0:
            acc_ref[...] = y
        else:
            acc_ref[...] += y
    out_ref[...] = _layer_norm(ALPHA * x_ref[...] + acc_ref[...], lng_ref[...], lnb_ref[...])


def _dense_ffn(x, w13, w2, ln_g, ln_b, j):
    s = x.shape[0]
    tm = min(FFN_ROW_TILE, s)
    fc = FFN_FF_CHUNK
    small = pl.BlockSpec((1, D_MODEL), lambda i: (0, 0))
    return pl.pallas_call(
        functools.partial(_ffn_kernel, j=j),
        out_shape=jax.ShapeDtypeStruct((s, D_MODEL), F32),
        grid=(s // tm,),
        in_specs=[pl.BlockSpec((tm, D_MODEL), lambda i: (i, 0)),
                  pl.BlockSpec(memory_space=pl.ANY), pl.BlockSpec(memory_space=pl.ANY),
                  small, small],
        out_specs=pl.BlockSpec((tm, D_MODEL), lambda i: (i, 0)),
        scratch_shapes=[pltpu.VMEM((D_MODEL, 2 * D_FF), BF16), pltpu.VMEM((D_FF, D_MODEL), BF16),
                        pltpu.VMEM((tm, D_MODEL), F32),
                        pltpu.VMEM((2, D_MODEL, fc), F32), pltpu.VMEM((2, fc, D_MODEL), F32),
                        pltpu.SemaphoreType.DMA((2,))],
        compiler_params=_params(("arbitrary",), 56),
        name="dense_ffn",
    )(x, w13, w2, ln_g.reshape(1, -1), ln_b.reshape(1, -1))


def _router_kernel(x_ref, rw_ref, rb_ref, gate_ref, rank_ref, cnt_ref, xb_ref, carry_ref):
    tm = x_ref.shape[0]
    ne = N_EXPERTS

    @pl.when(pl.program_id(0) == 0)
    def _init():
        carry_ref[...] = jnp.zeros_like(carry_ref)

    x = x_ref[...]
    xb_ref[...] = x.astype(BF16)
    xs = _split3(x)
    ws = _split3(rw_ref[...])
    logits = rb_ref[...] + sum(_dot(xs[a], ws[b]) for a, b in ((1, 0), (0, 1), (0, 0)))
    lane = lax.broadcasted_iota(I32, (tm, ne), 1)
    v1 = jnp.max(logits, axis=-1, keepdims=True)
    i1 = jnp.min(jnp.where(logits == v1, lane, ne), axis=-1, keepdims=True)
    rest = jnp.where(lane == i1, -jnp.inf, logits)
    v2 = jnp.max(rest, axis=-1, keepdims=True)
    i2 = jnp.min(jnp.where(rest == v2, lane, ne), axis=-1, keepdims=True)
    e2 = jnp.exp(v2 - v1)
    den = 1.0 + e2
    sel1, sel2 = lane == i1, lane == i2
    gate_ref[...] = jnp.where(sel1, 1.0 / den, 0.0) + jnp.where(sel2, e2 / den, 0.0)
    sel = jnp.where(sel1 | sel2, 1.0, 0.0)
    ri = lax.broadcasted_iota(I32, (tm, tm), 0)
    ci = lax.broadcasted_iota(I32, (tm, tm), 1)
    before = (ri > ci).astype(BF16)
    carry = carry_ref[0:1, 0:ne]
    rank = _dot(before, sel.astype(BF16)) + carry
    rank_ref[...] = jnp.where(sel > 0.0, rank, -1.0)
    total = carry + jnp.sum(sel, axis=0, keepdims=True)
    carry_ref[0:1, 0:ne] = total
    cnt_ref[...] = total


def _router(x, router_w, router_b):
    s = x.shape[0]
    tm = 512
    ne = N_EXPERTS
    return pl.pallas_call(
        _router_kernel,
        out_shape=(jax.ShapeDtypeStruct((s, ne), F32), jax.ShapeDtypeStruct((s, ne), F32),
                   jax.ShapeDtypeStruct((1, ne), F32), jax.ShapeDtypeStruct((s, D_MODEL), BF16)),
        grid=(s // tm,),
        in_specs=[pl.BlockSpec((tm, D_MODEL), lambda i: (i, 0)),
                  pl.BlockSpec((D_MODEL, ne), lambda i: (0, 0)),
                  pl.BlockSpec((1, ne), lambda i: (0, 0))],
        out_specs=(pl.BlockSpec((tm, ne), lambda i: (i, 0)), pl.BlockSpec((tm, ne), lambda i: (i, 0)),
                   pl.BlockSpec((1, ne), lambda i: (0, 0)), pl.BlockSpec((tm, D_MODEL), lambda i: (i, 0))),
        scratch_shapes=[pltpu.VMEM((8, LANES), F32)],
        compiler_params=_params(("arbitrary",), 32),
        name="moe_router",
    )(x, router_w, router_b.reshape(1, ne))


def _chunk_copy(src_hbm, buf_ref, sem_ref, chunk, slot):
    rows = buf_ref.shape[1]
    start = pl.multiple_of(chunk * rows, rows)
    return pltpu.make_async_copy(src_hbm.at[pl.ds(start, rows), :], buf_ref.at[slot], sem_ref.at[slot])


def _moe_gemm_kernel(te_ref, nu_ref, off_ref, ist_ref, clo_ref, pad_ref, nsub_ref, xb_hbm, rank_ref, w1_ref, w3_ref,
                     w2_ref, y_ref, xs_ref, acc_ref, buf_ref, sem_ref):
    gr, ck = MOE_GATHER_ROWS, MOE_GATHER_TOKENS
    sub = MOE_GROUP_TILE // gr
    nbuf = buf_ref.shape[0]
    last_chunk = rank_ref.shape[1] - 1
    i, f = pl.program_id(0), pl.program_id(1)
    last_f = pl.num_programs(1) - 1

    def item(t, k):
        base = t * sub
        s = sum((k >= ist_ref[base + j]).astype(I32) for j in range(1, sub))
        return s, jnp.minimum(clo_ref[base + s] + (k - ist_ref[base + s]), last_chunk)

    def start(t, k):
        @pl.when(k < ist_ref[t * sub + sub])
        def _():
            _chunk_copy(xb_hbm, buf_ref, sem_ref, item(t, k)[1], (k - ist_ref[t * sub]) % nbuf).start()

    @pl.when(i < nu_ref[0])
    def _tile():
        @pl.when(f == 0)
        def _gather_rows():
            e = te_ref[i]
            base = i * sub
            k0, k1 = ist_ref[base], ist_ref[base + sub]
            k_real = k1 - pad_ref[i]
            xs_ref[...] = jnp.zeros_like(xs_ref)
            acc_ref[...] = jnp.zeros_like(acc_ref)

            @pl.when(i == 0)
            def _first_tile():
                for d in range(nbuf):
                    start(i, k0 + d)

            def pair(t, carry):
                ks = (k0 + 2 * t, k0 + 2 * t + 1)
                its = [item(i, k) for k in ks]
                slots = [(k - k0) % nbuf for k in ks]
                for (_, chunk), slot in zip(its, slots):
                    _chunk_copy(xb_hbm, buf_ref, sem_ref, chunk, slot).wait()
                iota = lax.broadcasted_iota(I32, (gr, ck), 0).astype(F32)
                rank0 = [jnp.where(k < k_real, (base + s) * gr - off_ref[e], -(2 ** 30)).astype(F32)
                         for k, (s, _) in zip(ks, its)]
                hit = [(r0 + iota) == rank_ref[e, pl.ds(chunk, 1), :] for r0, (_, chunk) in zip(rank0, its)]
                ys = [_dot(jnp.where(h, 1.0, 0.0).astype(BF16), buf_ref[slot]) for h, slot in zip(hit, slots)]
                for (s, _), y in zip(its, ys):
                    rows = pl.ds(pl.multiple_of(s * gr, gr), gr)
                    xs_ref[rows, :] = xs_ref[rows, :] + y.astype(BF16)
                for k in ks:
                    start(i, k + nbuf)
                return carry

            lax.fori_loop(0, (k1 - k0) // 2, pair, 0)

            @pl.when(i + 1 < nu_ref[0])
            def _prefetch_next_tile():
                for d in range(nbuf):
                    start(i + 1, k1 + d)

        def ffn_rows(m):
            def run():
                xb = xs_ref[0:m, :]
                a = _dot(xb, w1_ref[0].astype(BF16))
                g = _dot(xb, w3_ref[0].astype(BF16))
                acc_ref[0:m, :] += _dot((a * _sigmoid(a) * g).astype(BF16), w2_ref[0].astype(BF16))
            return run

        lax.switch(nsub_ref[i] - 1, [ffn_rows((r + 1) * gr) for r in range(sub)])

        @pl.when(f == last_f)
        def _finish():
            y_ref[...] = acc_ref[...].astype(y_ref.dtype)

    @pl.when(jnp.logical_and(i >= nu_ref[0], f == last_f))
    def _unused_tile():
        y_ref[...] = jnp.zeros_like(y_ref)


def _combine_kernel(off_ref, cn_ref, cie_ref, cic_ref, x_ref, ys_hbm, rank_ref, gate_ref, lng_ref, lnb_ref,
                    out_ref, acc_ref, buf_ref, sem_ref, *, slots):
    tt, cr = MOE_COMBINE_TOKENS, MOE_COMBINE_ROWS
    j = pl.program_id(0)
    n = cn_ref[j]
    base = j * slots
    acc_ref[...] = jnp.zeros_like(acc_ref)
    nbuf = buf_ref.shape[0]

    def start(k, step=j):
        @pl.when(k < cn_ref[step])
        def _():
            chunk = cic_ref[step * slots + jnp.minimum(k, slots - 1)]
            _chunk_copy(ys_hbm, buf_ref, sem_ref, chunk, k % nbuf).start()

    @pl.when(j == 0)
    def _first_step():
        for d in range(nbuf):
            start(d)

    lane = lax.broadcasted_iota(I32, (tt, N_EXPERTS), 1)
    iota = lax.broadcasted_iota(I32, (tt, cr), 1).astype(F32)

    def pair(t, carry):
        ks = (2 * t, 2 * t + 1)
        es = [cie_ref[base + k] for k in ks]
        chunks = [cic_ref[base + k] for k in ks]
        for k, chunk in zip(ks, chunks):
            _chunk_copy(ys_hbm, buf_ref, sem_ref, chunk, k % nbuf).wait()
        rank = [jnp.max(jnp.where(lane == e, rank_ref[...], -1.0), axis=-1, keepdims=True) for e in es]
        gate = [jnp.sum(jnp.where(lane == e, gate_ref[...], 0.0), axis=-1, keepdims=True) for e in es]
        pos = [jnp.where(r >= 0.0, r + off_ref[jnp.maximum(e, 0)].astype(F32), -1.0)
               for r, e in zip(rank, es)]
        ys = [_dot(jnp.where(p == (chunk * cr).astype(F32) + iota, 1.0, 0.0).astype(BF16), buf_ref[k % nbuf])
              for p, chunk, k in zip(pos, chunks, ks)]
        acc_ref[...] += gate[0] * ys[0] + gate[1] * ys[1]
        for k in ks:
            start(k + nbuf)
        return carry

    lax.fori_loop(0, n // 2, pair, 0)

    @pl.when(j + 1 < pl.num_programs(0))
    def _prefetch_next_step():
        for d in range(nbuf):
            start(d, j + 1)

    out_ref[...] = _layer_norm(ALPHA * x_ref[...] + acc_ref[...], lng_ref[...], lnb_ref[...])


def _moe_layer(x, router_w, router_b, w13, w2, ln_g, ln_b):
    s = x.shape[0]
    ne, gt, gr, ck = N_EXPERTS, MOE_GROUP_TILE, MOE_GATHER_ROWS, MOE_GATHER_TOKENS
    tt, cr = MOE_COMBINE_TOKENS, MOE_COMBINE_ROWS
    sub = gt // gr
    nck = s // ck
    n_group_tiles = (TOP_K * s) // gt + ne
    n_sub = n_group_tiles * sub

    gate, rank, counts, xb = _router(x, router_w, router_b)

    cnt = counts[0].astype(I32)
    padded = ((cnt + gt - 1) // gt) * gt
    off = (jnp.cumsum(padded) - padded).astype(I32)
    n_used = (jnp.sum(padded) // gt).astype(I32)
    group_end = jnp.cumsum(padded)
    tile_row = jnp.arange(n_group_tiles, dtype=I32) * gt
    tile_expert = jnp.minimum(jnp.sum(group_end[None, :] <= tile_row[:, None], axis=1), ne - 1).astype(I32)
    routed = (rank >= 0.0).reshape(nck, ck, ne)
    cum = jnp.cumsum(jnp.sum(routed, axis=1), axis=0).astype(I32).T

    u = jnp.arange(n_sub, dtype=I32)
    e_u = tile_expert[u // sub]
    r0 = u * gr - off[e_u]
    r1 = jnp.minimum(r0 + gr, cnt[e_u]) - 1
    live = (u // sub < n_used) & (r0 < cnt[e_u])
    cum_u = cum[e_u]
    c_lo = jnp.sum(cum_u <= r0[:, None], axis=1).astype(I32)
    c_hi = jnp.sum(cum_u <= r1[:, None], axis=1).astype(I32)
    n_items = jnp.where(live, c_hi - c_lo + 1, 0)
    item_pad = jnp.sum(n_items.reshape(n_group_tiles, sub), axis=1).astype(I32) % 2
    n_items = n_items + jnp.where(u % sub == sub - 1, item_pad[u // sub], 0)
    item_start = jnp.concatenate([jnp.zeros((1,), I32), jnp.cumsum(n_items).astype(I32)])
    c_lo = jnp.where(live, c_lo, 0)
    real_rows = jnp.clip(cnt[tile_expert] - (tile_row - off[tile_expert]), 0, gt)
    n_real_sub = jnp.clip((real_rows + gr - 1) // gr, 1, sub).astype(I32)

    fc = MOE_FF_CHUNK
    nf = D_FF_E // fc
    tile_of = lambda i, nu: jnp.minimum(i, nu[0] - 1)
    chunk_of = lambda i, f, nu: jnp.where(i < nu[0], f, nf - 1)
    ys = pl.pallas_call(
        _moe_gemm_kernel,
        out_shape=jax.ShapeDtypeStruct((n_group_tiles * gt, D_MODEL), BF16),
        grid_spec=pltpu.PrefetchScalarGridSpec(
            num_scalar_prefetch=7,
            grid=(n_group_tiles, nf),
            in_specs=[pl.BlockSpec(memory_space=pl.ANY),
                      pl.BlockSpec((ne, nck, ck), lambda i, f, te, nu, *_: (0, 0, 0)),
                      pl.BlockSpec((1, D_MODEL, fc),
                                   lambda i, f, te, nu, *_: (te[tile_of(i, nu)], 0, chunk_of(i, f, nu))),
                      pl.BlockSpec((1, D_MODEL, fc),
                                   lambda i, f, te, nu, *_: (te[tile_of(i, nu)], 0, nf + chunk_of(i, f, nu))),
                      pl.BlockSpec((1, fc, D_MODEL),
                                   lambda i, f, te, nu, *_: (te[tile_of(i, nu)], chunk_of(i, f, nu), 0))],
            out_specs=pl.BlockSpec((gt, D_MODEL), lambda i, f, *_: (i, 0)),
            scratch_shapes=[pltpu.VMEM((gt, D_MODEL), BF16), pltpu.VMEM((gt, D_MODEL), F32),
                            pltpu.VMEM((MOE_DMA_DEPTH, ck, D_MODEL), BF16),
                            pltpu.SemaphoreType.DMA((MOE_DMA_DEPTH,))]),
        compiler_params=_params(("arbitrary", "arbitrary"), 58),
        name="moe_gemm",
    )(tile_expert, n_used.reshape(1), off, item_start, c_lo, item_pad, n_real_sub, xb,
      rank.T.reshape(ne, nck, ck), w13, w13, w2)

    njc = s // tt
    per = tt // ck
    incl = cum[:, per - 1::per]
    lo = off[:, None] + jnp.concatenate([jnp.zeros((ne, 1), I32), incl[:, :-1]], axis=1)
    hi = off[:, None] + incl
    first, last = lo // cr, (hi - 1) // cr
    max_chunks = tt // cr + 1
    kk = jnp.arange(max_chunks, dtype=I32)
    slot_chunk = (first[:, :, None] + kk).transpose(1, 0, 2).reshape(njc, -1)
    slot_live = ((hi > lo)[:, :, None] & (first[:, :, None] + kk <= last[:, :, None])).transpose(1, 0, 2)
    slot_live = slot_live.reshape(njc, -1)
    slots = ne * max_chunks
    slot_expert = jnp.broadcast_to(jnp.repeat(jnp.arange(ne, dtype=I32), max_chunks)[None, :], (njc, slots))
    dest = jnp.cumsum(slot_live, axis=1) - 1
    place = slot_live[:, :, None] & (dest[:, :, None] == jnp.arange(slots)[None, None, :])
    item_chunk = jnp.sum(jnp.where(place, slot_chunk[:, :, None], 0), axis=1).astype(I32).reshape(-1)
    item_expert = jnp.sum(jnp.where(place, slot_expert[:, :, None], 0), axis=1).astype(I32)
    item_count = jnp.sum(slot_live, axis=1).astype(I32)
    item_expert = jnp.where(jnp.arange(slots)[None, :] < item_count[:, None], item_expert, -1).reshape(-1)
    item_count = item_count + item_count % 2

    small = pl.BlockSpec((1, D_MODEL), lambda j, *_: (0, 0))
    return pl.pallas_call(
        functools.partial(_combine_kernel, slots=slots),
        out_shape=jax.ShapeDtypeStruct((s, D_MODEL), F32),
        grid_spec=pltpu.PrefetchScalarGridSpec(
            num_scalar_prefetch=4,
            grid=(njc,),
            in_specs=[pl.BlockSpec((tt, D_MODEL), lambda j, *_: (j, 0)),
                      pl.BlockSpec(memory_space=pl.ANY),
                      pl.BlockSpec((tt, ne), lambda j, *_: (j, 0)),
                      pl.BlockSpec((tt, ne), lambda j, *_: (j, 0)),
                      small, small],
            out_specs=pl.BlockSpec((tt, D_MODEL), lambda j, *_: (j, 0)),
            scratch_shapes=[pltpu.VMEM((tt, D_MODEL), F32), pltpu.VMEM((MOE_DMA_DEPTH, cr, D_MODEL), BF16),
                            pltpu.SemaphoreType.DMA((MOE_DMA_DEPTH,))]),
        compiler_params=_params(("arbitrary",), 32),
        name="moe_combine",
    )(off, item_count, item_expert, item_chunk, x, ys, rank, gate, ln_g.reshape(1, -1), ln_b.reshape(1, -1))


def kernel(x, rel_bias, w_in, w_gate, b_gate, conv_w, conv_b, w_qk_m, w_v_m, w_if, b_if, m_norm_g, w_br_a, w_br_m,
           w_o, ln_g, ln_b, ffn_w13, ffn_w2, router_w, router_b, exp_w13, exp_w2):
    batch, seq, _ = x.shape
    assert batch == 1
    h = x.reshape(seq, D_MODEL)
    biases = [_attn_bias(rel_bias, window, dilation) for window, dilation in ATTN_PATTERNS]
    for l in range(DEPTH):
        h = _token_mixer_layer(h, biases, w_in, w_gate, b_gate[l], conv_w[l], conv_b[l], w_qk_m, w_v_m, w_if[l],
                               b_if[l], m_norm_g[l], w_br_a, w_br_m, w_o, ln_g[l, 0], ln_b[l, 0], l)
        j = l // 2
        if l % 2 == 0:
            h = _dense_ffn(h, ffn_w13, ffn_w2, ln_g[l, 1], ln_b[l, 1], j)
        else:
            h = _moe_layer(h, router_w[j], router_b[j], exp_w13[j], exp_w2[j], ln_g[l, 1], ln_b[l, 1])
    return h.reshape(batch, seq, D_MODEL)
```
